```python
import math
import jax, jax.numpy as jnp
from jax import lax
import numpy as np

D_MODEL = 2048
BATCH = 8
SEQ = 4096
DEPTH = 1

HEAD_DIM = 128
ATTN_HEADS = D_MODEL // (2 * HEAD_DIM)
ATTN_W = ATTN_HEADS * HEAD_DIM
HY_GROUP_DIM = 128
HY_GROUPS = D_MODEL // (2 * HY_GROUP_DIM)
HY_W = HY_GROUPS * HY_GROUP_DIM
IN_W = 3 * ATTN_W + 3 * HY_W
ATTN_PATTERNS = ((128, 1), (512, 4), (2048, 16))
N_BUCKETS = 32
REL_MAX_DIST = 1024
HY_EMB = 33
HY_FILTER_WIDTH = 64
HY_INNER = 2
HY_SHORT_CONV = 3
HY_DECAY_MIN = 3.07
HY_DECAY_MAX = 15.35
FFN_HIDDEN = ((8 * D_MODEL // 3 + 255) // 256) * 256
PLE_DIM = 256
EPS = 1e-6
NEG = -1e30

kernel_name = "hybrid_hyena_dilated_attn_encoder_layer"


def rms_norm(x, gain):
    xf = x.astype(jnp.float32)
    y = xf * lax.rsqrt(jnp.mean(xf * xf, axis=-1, keepdims=True) + EPS)
    return (y * gain.astype(jnp.float32)).astype(x.dtype)


def group_rms_norm(y, gain, n_groups):
    B, S, W = y.shape
    yg = rms_norm(y.reshape(B, S, n_groups, W // n_groups), gain.reshape(n_groups, W // n_groups))
    return yg.reshape(B, S, W)


def t5_bucket(rel):
    half = N_BUCKETS // 2
    exact = half // 2
    n = jnp.abs(rel)
    large = exact + (jnp.log(jnp.maximum(n, 1).astype(jnp.float32) / exact)
                     / math.log(REL_MAX_DIST / exact) * (half - exact)).astype(jnp.int32)
    large = jnp.minimum(large, half - 1)
    return jnp.where(rel > 0, half, 0) + jnp.where(n < exact, n, large)


def dilated_window_attention(q, k, v, rel_bias, window, dilation):
    B, S, H, Dh = q.shape
    d = dilation
    n = (window // 2) // d
    Q = n
    Ls = S // d
    nb = -(-Ls // Q)
    Lp = nb * Q

    def strided(t):
        return t.reshape(B, Ls, d, H, Dh).transpose(0, 2, 3, 1, 4)

    qs = jnp.pad(strided(q), ((0, 0), (0, 0), (0, 0), (0, Lp - Ls), (0, 0)))
    pad_kv = ((0, 0), (0, 0), (0, 0), (Q, Lp - Ls + Q), (0, 0))
    ks = jnp.pad(strided(k), pad_kv).reshape(B, d, H, nb + 2, Q, Dh)
    vs = jnp.pad(strided(v), pad_kv).reshape(B, d, H, nb + 2, Q, Dh)
    qb = qs.reshape(B, d, H, nb, Q, Dh)
    kb = jnp.concatenate([ks[:, :, :, :-2], ks[:, :, :, 1:-1], ks[:, :, :, 2:]], axis=4)
    vb = jnp.concatenate([vs[:, :, :, :-2], vs[:, :, :, 1:-1], vs[:, :, :, 2:]], axis=4)

    qi = jnp.arange(Q)[:, None]
    kj = jnp.arange(3 * Q)[None, :]
    rel = kj - Q - qi
    band = jnp.abs(rel) <= n
    j_abs = (jnp.arange(nb)[:, None] - 1) * Q + jnp.arange(3 * Q)[None, :]
    valid = (j_abs >= 0) & (j_abs < Ls)
    mask = band[None] & valid[:, None, :]
    bias = jnp.moveaxis(rel_bias.astype(jnp.float32)[t5_bucket(rel * d)], -1, 0)

    s = jnp.einsum('brhnqc,brhnkc->brhnqk', qb, kb).astype(jnp.float32) * (Dh ** -0.5)
    s = jnp.where(mask, s + bias[:, None], NEG)
    m = jnp.max(s, axis=-1, keepdims=True)
    e = jnp.exp(s - m)
    l = jnp.sum(e, axis=-1)
    o = jnp.einsum('brhnqk,brhnkc->brhnqc', e.astype(vb.dtype), vb).astype(jnp.float32) / l[..., None]
    lse = m[..., 0] + jnp.log(l)

    o = o.reshape(B, d, H, Lp, Dh)[:, :, :, :Ls].transpose(0, 3, 1, 2, 4).reshape(B, S, H, Dh)
    lse = lse.reshape(B, d, H, Lp)[:, :, :, :Ls].transpose(0, 3, 1, 2).reshape(B, S, H)
    return o, lse


def hyena_filter(L, w1, b1, wi, bi, wo, freq, decay):
    f32 = jnp.float32
    pos = jnp.arange(L, dtype=f32)
    t = pos / max(L - 1, 1)
    bands = (HY_EMB - 1) // 2
    fr = jnp.linspace(1e-4, bands - 1, bands, dtype=f32)
    ang = (2.0 * math.pi / L) * pos[:, None] * fr[None, :]
    z = jnp.concatenate([t[:, None], jnp.cos(ang), -jnp.sin(ang)], axis=-1)
    fq = freq.astype(f32)
    hdn = jnp.sin(fq * (z @ w1.astype(f32) + b1.astype(f32)))
    for j in range(HY_INNER):
        hdn = jnp.sin(fq * (hdn @ wi[j].astype(f32) + bi[j].astype(f32)))
    filt = hdn @ wo.astype(f32)
    offs = jnp.abs(pos - (L // 2)) / (L / 2)
    return filt * jnp.exp(-offs[:, None] * jnp.abs(decay.astype(f32))[None, :])


def centred_long_conv(v, filt):
    L = v.shape[1]
    vf = jnp.fft.rfft(v.astype(jnp.float32), n=2 * L, axis=1)
    hf = jnp.fft.rfft(filt, n=2 * L, axis=0)
    y = jnp.fft.irfft(vf * hf[None], n=2 * L, axis=1)
    return y[:, L // 2: L // 2 + L]


def short_conv(u, w, b):
    C = u.shape[-1]
    y = lax.conv_general_dilated(u, w[:, None, :].astype(u.dtype), window_strides=(1,),
                                 padding=[(1, 1)], dimension_numbers=('NWC', 'WIO', 'NWC'),
                                 feature_group_count=C)
    return y + b.astype(u.dtype)


def setup_inputs(seed: int = 0) -> dict:
    key = jax.random.key(seed)
    ks = jax.random.split(key, 32)
    f32 = jnp.float32
    nrm = lambda k, shape, scale: jax.random.normal(k, shape, f32) * scale
    gain = lambda k, shape: 1.0 + 0.05 * jax.random.normal(k, shape, f32)
    L = DEPTH
    return {
        "x": jax.random.normal(ks[0], (BATCH, SEQ, D_MODEL), f32),
        "p": jax.random.normal(ks[1], (DEPTH, BATCH, SEQ, PLE_DIM), f32),
        "rel_bias": nrm(ks[2], (N_BUCKETS, ATTN_HEADS), 0.5),
        "norm1": gain(ks[3], (L, D_MODEL)),
        "w_in": nrm(ks[4], (L, D_MODEL, IN_W), D_MODEL ** -0.5),
        "q_norm": gain(ks[5], (L, HEAD_DIM)),
        "k_norm": gain(ks[6], (L, HEAD_DIM)),
        "conv_w": nrm(ks[7], (L, HY_SHORT_CONV, 3 * HY_W), HY_SHORT_CONV ** -0.5),
        "conv_b": nrm(ks[8], (L, 3 * HY_W), 0.02),
        "hy_w1": nrm(ks[9], (L, HY_EMB, HY_FILTER_WIDTH), HY_EMB ** -0.5),
        "hy_b1": nrm(ks[10], (L, HY_FILTER_WIDTH), 0.1),
        "hy_wi": nrm(ks[11], (L, HY_INNER, HY_FILTER_WIDTH, HY_FILTER_WIDTH), HY_FILTER_WIDTH ** -0.5),
        "hy_bi": nrm(ks[12], (L, HY_INNER, HY_FILTER_WIDTH), 0.1),
        "hy_wo": nrm(ks[13], (L, HY_FILTER_WIDTH, HY_W), HY_FILTER_WIDTH ** -0.5),
        "hy_freq": gain(ks[14], (L, HY_FILTER_WIDTH)),
        "hy_decay": jnp.exp(jax.random.uniform(ks[15], (L, HY_W), f32,
                                                math.log(HY_DECAY_MIN), math.log(HY_DECAY_MAX))),
        "hy_bias": nrm(ks[16], (L, HY_W), 1.0),
        "attn_out_norm": gain(ks[17], (L, ATTN_W)),
        "hy_out_norm": gain(ks[18], (L, HY_W)),
        "w_out": nrm(ks[19], (L, ATTN_W + HY_W, D_MODEL), (ATTN_W + HY_W) ** -0.5),
        "norm2": gain(ks[20], (L, D_MODEL)),
        "w_gu": nrm(ks[21], (L, D_MODEL, 2 * FFN_HIDDEN), D_MODEL ** -0.5),
        "w_down": nrm(ks[22], (L, FFN_HIDDEN, D_MODEL), FFN_HIDDEN ** -0.5),
        "ple_norm": gain(ks[23], (L, D_MODEL)),
        "w_ple_gate": nrm(ks[24], (L, D_MODEL, D_MODEL), D_MODEL ** -0.5),
        "w_ple_proj": nrm(ks[25], (L, PLE_DIM, D_MODEL), PLE_DIM ** -0.5),
        "ple_post_norm": gain(ks[26], (L, D_MODEL)),
    }


def reference(x, p, rel_bias, norm1, w_in, q_norm, k_norm, conv_w, conv_b, hy_w1, hy_b1,
              hy_wi, hy_bi, hy_wo, hy_freq, hy_decay, hy_bias, attn_out_norm, hy_out_norm,
              w_out, norm2, w_gu, w_down, ple_norm, w_ple_gate, w_ple_proj, ple_post_norm):
    B, S, _ = x.shape
    h = x
    for i in range(DEPTH):
        u = rms_norm(h, norm1[i]) @ w_in[i]
        q, k, v, hy = jnp.split(u, [ATTN_W, 2 * ATTN_W, 3 * ATTN_W], axis=-1)

        q = rms_norm(q.reshape(B, S, ATTN_HEADS, HEAD_DIM), q_norm[i])
        k = rms_norm(k.reshape(B, S, ATTN_HEADS, HEAD_DIM), k_norm[i])
        v = v.reshape(B, S, ATTN_HEADS, HEAD_DIM)
        outs, lses = [], []
        for window, dilation in ATTN_PATTERNS:
            o_g, lse_g = dilated_window_attention(q, k, v, rel_bias, window, dilation)
            outs.append(o_g)
            lses.append(lse_g)
        wts = jax.nn.softmax(jnp.stack(lses), axis=0)
        y_att = jnp.einsum('gbsh,gbshc->bshc', wts, jnp.stack(outs))
        y_att = y_att.reshape(B, S, ATTN_W).astype(h.dtype)

        hy = short_conv(hy, conv_w[i], conv_b[i])
        x0, x1, hv = jnp.split(hy, 3, axis=-1)
        filt = hyena_filter(S, hy_w1[i], hy_b1[i], hy_wi[i], hy_bi[i], hy_wo[i], hy_freq[i], hy_decay[i])
        z = (hv * x1).astype(jnp.float32)
        z = centred_long_conv(z, filt) + z * hy_bias[i].astype(jnp.float32)
        y_hy = (z * x0.astype(jnp.float32)).astype(h.dtype)

        y = jnp.concatenate([group_rms_norm(y_att, attn_out_norm[i], ATTN_HEADS),
                             group_rms_norm(y_hy, hy_out_norm[i], HY_GROUPS)], axis=-1)
        h = h + y @ w_out[i]

        a, g = jnp.split(rms_norm(h, norm2[i]) @ w_gu[i], 2, axis=-1)
        h = h + (jax.nn.silu(a) * g) @ w_down[i]

        gate = jax.nn.sigmoid(rms_norm(h, ple_norm[i]) @ w_ple_gate[i])
        e = rms_norm(p[i] @ w_ple_proj[i], ple_post_norm[i])
        h = h + gate * e
    return h
```

```python
import functools
import math

import jax
import jax.numpy as jnp
import numpy as np
from jax import lax
from jax.experimental import pallas as pl
from jax.experimental.pallas import tpu as pltpu

F32 = jnp.float32
BF16 = jnp.bfloat16

LANES = 128
HEAD_DIM = 128
ATTN_PATTERNS = ((128, 1), (512, 4), (2048, 16))
BAND = 64
QBLK = 128
KBLK = QBLK + 2 * BAND
N_BUCKETS = 32
REL_MAX_DIST = 1024
HY_EMB = 33
HY_FILTER_WIDTH = 64
HY_INNER = 2
EPS = 1e-6
NEG = -1e30
PLE_DIM = 256

FFT_R = 64
FFT_PITCH = FFT_R + 8
VMEM_LIMIT = 56 * 1024 * 1024


def _cparams(sem, vmem=VMEM_LIMIT):
    return pltpu.CompilerParams(dimension_semantics=sem, vmem_limit_bytes=vmem)


def _rms(x, gain):
    ms = jnp.mean(x * x, axis=-1, keepdims=True)
    return x * lax.rsqrt(ms + EPS) * gain


def _in_proj_kernel(x_ref, g_ref, w_ref, qg_ref, kg_ref, o_ref, xn_ref, *, nq, hpt):
    j = pl.program_id(1)

    @pl.when(j == 0)
    def _():
        xn_ref[...] = _rms(x_ref[...], g_ref[...]).astype(BF16)

    acc = jnp.dot(xn_ref[...], w_ref[...], preferred_element_type=F32)

    def write(gain):
        for hh in range(hpt):
            a = acc[:, hh * LANES:(hh + 1) * LANES]
            if gain is not None:
                a = _rms(a, gain)
            o_ref[hh] = a.astype(BF16)

    @pl.when(j < nq)
    def _():
        write(qg_ref[...] * (HEAD_DIM ** -0.5))

    @pl.when((j >= nq) & (j < 2 * nq))
    def _():
        write(kg_ref[...])

    @pl.when(j >= 2 * nq)
    def _():
        write(None)


def _in_proj(x2, norm1, w_in, q_norm, k_norm, attn_w, tm):
    m, d = x2.shape
    in_w = w_in.shape[1]
    tn = min(512, attn_w)
    hpt = tn // LANES
    nq = attn_w // tn
    return pl.pallas_call(
        functools.partial(_in_proj_kernel, nq=nq, hpt=hpt),
        out_shape=jax.ShapeDtypeStruct((in_w // LANES, m, LANES), BF16),
        grid=(m // tm, in_w // tn),
        in_specs=[
            pl.BlockSpec((tm, d), lambda i, j: (i, 0)),
            pl.BlockSpec((1, d), lambda i, j: (0, 0)),
            pl.BlockSpec((d, tn), lambda i, j: (0, j)),
            pl.BlockSpec((1, LANES), lambda i, j: (0, 0)),
            pl.BlockSpec((1, LANES), lambda i, j: (0, 0)),
        ],
        out_specs=pl.BlockSpec((hpt, tm, LANES), lambda i, j: (j, i, 0)),
        scratch_shapes=[pltpu.VMEM((tm, d), BF16)],
        compiler_params=_cparams(("parallel", "arbitrary")),
        name="in_proj",
    )(x2, norm1.reshape(1, d), w_in.astype(BF16), q_norm.reshape(1, LANES),
      k_norm.reshape(1, LANES))


def _t5_bucket(rel):
    half = N_BUCKETS // 2
    exact = half // 2
    n = jnp.abs(rel)
    large = exact + (jnp.log(jnp.maximum(n, 1).astype(F32) / exact)
                     / math.log(REL_MAX_DIST / exact) * (half - exact)).astype(jnp.int32)
    large = jnp.minimum(large, half - 1)
    return jnp.where(rel > 0, half, 0) + jnp.where(n < exact, n, large)


def _attn_bias_tables(rel_bias):
    qi = jnp.arange(QBLK)[:, None]
    kc = jnp.arange(KBLK)[None, :]
    rel = kc - BAND - qi
    band = jnp.abs(rel) <= BAND
    first_ok = jnp.broadcast_to(kc >= BAND, rel.shape)
    last_ok = jnp.broadcast_to(kc < QBLK + BAND, rel.shape)
    tabs = []
    for _, dil in ATTN_PATTERNS:
        bias = jnp.moveaxis(rel_bias.astype(F32)[_t5_bucket(rel * dil)], -1, 0)
        variants = [jnp.where(band & ok, bias, NEG)
                    for ok in (first_ok, jnp.ones_like(band), last_ok)]
        tabs.append(jnp.stack(variants, axis=1))
    return jnp.stack(tabs, axis=1)


def _attn_kernel(q_ref, k_ref, v_ref, bias_ref, gain_ref, o_ref,
                 xf_ref, qp_ref, kp_ref, vp_ref, acc_ref, m_ref, l_ref, *, seq):
    zeros_pad = jnp.zeros((BAND, LANES), BF16)

    for pidx, (_, dil) in enumerate(ATTN_PATTERNS):
        ls = seq // dil
        nb = ls // QBLK
        seg = ls + 2 * BAND

        if dil == 1:
            kp_ref[pl.ds(BAND, seq), :] = k_ref[...]
            vp_ref[pl.ds(BAND, seq), :] = v_ref[...]
        else:
            if pidx == 1:
                xf_ref[0] = q_ref[...].astype(F32)
                xf_ref[1] = k_ref[...].astype(F32)
                xf_ref[2] = v_ref[...].astype(F32)
            for r in range(dil):
                qp_ref[pl.ds(r * ls, ls), :] = (
                    xf_ref[0, pl.ds(r, ls, stride=dil), :].astype(BF16))
                kp_ref[pl.ds(r * seg + BAND, ls), :] = (
                    xf_ref[1, pl.ds(r, ls, stride=dil), :].astype(BF16))
                vp_ref[pl.ds(r * seg + BAND, ls), :] = (
                    xf_ref[2, pl.ds(r, ls, stride=dil), :].astype(BF16))
        for r in range(dil):
            for ref in (kp_ref, vp_ref):
                ref[pl.ds(r * seg, BAND), :] = zeros_pad
                ref[pl.ds(r * seg + BAND + ls, BAND), :] = zeros_pad

        def block(t, carry, dil=dil, ls=ls, nb=nb, seg=seg, pidx=pidx):
            r = t // nb
            qb = t % nb
            qoff = pl.multiple_of(r * ls + qb * QBLK, QBLK)
            koff = pl.multiple_of(r * seg + qb * QBLK, QBLK)
            if dil == 1:
                q = q_ref[pl.ds(qoff, QBLK), :]
            else:
                q = qp_ref[pl.ds(qoff, QBLK), :]
            k = kp_ref[pl.ds(koff, KBLK), :]
            v = vp_ref[pl.ds(koff, KBLK), :]
            variant = jnp.where(qb == 0, 0, jnp.where(qb == nb - 1, 2, 1))
            s = lax.dot_general(q, k, (((1,), (1,)), ((), ())),
                                preferred_element_type=F32)
            s = s + bias_ref[pidx, variant]
            m_blk = jnp.max(s, axis=-1, keepdims=True)
            p = jnp.exp(s - m_blk)
            l_blk = jnp.sum(p, axis=-1, keepdims=True)
            a_blk = jnp.dot(p.astype(BF16), v, preferred_element_type=F32)
            m_b = jnp.broadcast_to(m_blk, (QBLK, LANES))
            l_b = jnp.broadcast_to(l_blk, (QBLK, LANES))
            if dil == 1:
                rows = pl.ds(qoff, QBLK)
                acc_ref[rows, :] = a_blk
                m_ref[rows, :] = m_b
                l_ref[rows, :] = l_b
            else:
                rows = pl.ds(r + qb * (QBLK * dil), QBLK, stride=dil)
                m_old = m_ref[rows, :]
                m_new = jnp.maximum(m_old, m_b)
                w_old = jnp.exp(m_old - m_new)
                w_blk = jnp.exp(m_b - m_new)
                acc_ref[rows, :] = acc_ref[rows, :] * w_old + a_blk * w_blk
                l_ref[rows, :] = l_ref[rows, :] * w_old + l_b * w_blk
                m_ref[rows, :] = m_new
            return carry

        lax.fori_loop(0, dil * nb, block, 0)

    def finish(c, carry):
        rows = pl.ds(pl.multiple_of(c * 256, 256), 256)
        y = acc_ref[rows, :] / l_ref[rows, :]
        o_ref[rows, :] = _rms(y, gain_ref[...]).astype(BF16)
        return carry

    lax.fori_loop(0, seq // 256, finish, 0)


def _attention(u_t, bias_tab, attn_out_norm, batch, seq, heads):
    max_dil = ATTN_PATTERNS[-1][1]
    pad_rows = seq + 2 * BAND * max_dil
    return pl.pallas_call(
        functools.partial(_attn_kernel, seq=seq),
        out_shape=jax.ShapeDtypeStruct((heads, batch * seq, LANES), BF16),
        grid=(batch, heads),
        in_specs=[
            pl.BlockSpec((None, seq, LANES), lambda b, h: (h, b, 0)),
            pl.BlockSpec((None, seq, LANES), lambda b, h: (heads + h, b, 0)),
            pl.BlockSpec((None, seq, LANES), lambda b, h: (2 * heads + h, b, 0)),
            pl.BlockSpec((None, len(ATTN_PATTERNS), 3, QBLK, KBLK),
                         lambda b, h: (h, 0, 0, 0, 0)),
            pl.BlockSpec((None, 1, LANES), lambda b, h: (h, 0, 0)),
        ],
        out_specs=pl.BlockSpec((None, seq, LANES), lambda b, h: (h, b, 0)),
        scratch_shapes=[
            pltpu.VMEM((3, seq, LANES), F32),
            pltpu.VMEM((seq, LANES), BF16),
            pltpu.VMEM((pad_rows, LANES), BF16),
            pltpu.VMEM((pad_rows, LANES), BF16),
            pltpu.VMEM((seq, LANES), F32),
            pltpu.VMEM((seq, LANES), F32),
            pltpu.VMEM((seq, LANES), F32),
        ],
        compiler_params=_cparams(("parallel", "parallel")),
        name="attention",
    )(u_t, u_t, u_t, bias_tab, attn_out_norm.reshape(heads, 1, LANES))


def _filter_features(seq):
    pos = np.arange(seq, dtype=np.float32)
    t = pos / np.float32(max(seq - 1, 1))
    bands = (HY_EMB - 1) // 2
    fr = np.linspace(1e-4, bands - 1, bands, dtype=np.float32)
    ang = np.float32(2.0 * math.pi / seq) * pos[:, None] * fr[None, :]
    z = np.concatenate([t[:, None], np.cos(ang), -np.sin(ang)], axis=-1).astype(np.float32)
    zp = np.zeros((seq, LANES), np.float32)
    zp[:, :HY_EMB] = z
    offs = (np.abs(pos - (seq // 2)) / np.float32(seq / 2)).astype(np.float32)
    return zp, offs[:, None]


def _filter_kernel(z_ref, offs_ref, w1_ref, b1_ref, wi_ref, bi_ref, wo_ref, fq_ref,
                   decay_ref, o_ref):
    hp = lax.Precision.HIGHEST
    fq = fq_ref[...]
    h = jnp.sin(fq * (jnp.dot(z_ref[...], w1_ref[...], precision=hp,
                              preferred_element_type=F32) + b1_ref[...]))
    for j in range(HY_INNER):
        h = jnp.sin(fq * (jnp.dot(h, wi_ref[j], precision=hp,
                                  preferred_element_type=F32) + bi_ref[j]))
    filt = jnp.dot(h, wo_ref[...], precision=hp, preferred_element_type=F32)
    o_ref[...] = filt * jnp.exp(-offs_ref[...] * jnp.abs(decay_ref[...]))


def _hyena_filter(seq, hy_w, w1, b1, wi, bi, wo, freq, decay):
    zp, offs = _filter_features(seq)
    w1p = jnp.zeros((LANES, HY_FILTER_WIDTH), F32).at[:HY_EMB].set(w1.astype(F32))
    tr = 512
    fw = HY_FILTER_WIDTH
    const = lambda *shape: pl.BlockSpec(shape, lambda i: (0,) * len(shape))
    return pl.pallas_call(
        _filter_kernel,
        out_shape=jax.ShapeDtypeStruct((seq, hy_w), F32),
        grid=(seq // tr,),
        in_specs=[
            pl.BlockSpec((tr, LANES), lambda i: (i, 0)),
            pl.BlockSpec((tr, 1), lambda i: (i, 0)),
            const(LANES, fw), const(1, fw), const(HY_INNER, fw, fw),
            const(HY_INNER, 1, fw), const(fw, hy_w), const(1, fw), const(1, hy_w),
        ],
        out_specs=pl.BlockSpec((tr, hy_w), lambda i: (i, 0)),
        compiler_params=_cparams(("parallel",)),
        name="hyena_filter",
    )(jnp.asarray(zp), jnp.asarray(offs), w1p, b1.reshape(1, fw).astype(F32),
      wi.astype(F32), bi.reshape(HY_INNER, 1, fw).astype(F32), wo.astype(F32),
      freq.reshape(1, fw).astype(F32), decay.reshape(1, hy_w).astype(F32))


def _stack(c):
    return np.block([[c.real, -c.imag], [c.imag, c.real]])


@functools.lru_cache(maxsize=None)
def _dft_constants(seq):
    n = 2 * seq
    r = FFT_R
    a_n = n // r
    ar = np.arange(a_n)
    br = np.arange(r)
    f1 = np.exp(-2j * np.pi * np.outer(ar, ar) / a_n)
    f1_fwd = _stack(f1[:, :a_n // 2])
    f1_fwd_real = np.concatenate([f1[:, :a_n // 2].real, f1[:, :a_n // 2].imag], 0)
    f1_inv = _stack(np.conj(f1).T[a_n // 4:3 * a_n // 4] / n)
    ph = (br[None, None, :] * br[None, :, None] / r
          + br[None, None, :] * ar[:, None, None] / n)
    g = np.exp(-2j * np.pi * ph)
    g_fwd = np.stack([_stack(g[k]) for k in range(a_n)])
    g_inv = np.stack([_stack(np.conj(g[k]).T) for k in range(a_n)])
    return (f1_fwd.astype(np.float32), f1_fwd_real.astype(np.float32),
            f1_inv.astype(np.float32), g_fwd.astype(np.float32), g_inv.astype(np.float32))


def _spectrum_kernel(filt_ref, f1_ref, g_ref, hr_ref, hi_ref, zs_ref, ts_ref, *, seq):
    hp = lax.Precision.HIGHEST
    r = FFT_R
    a_half = seq // r
    a_n = 2 * a_half
    for a in range(a_half):
        zs_ref[pl.ds(a * FFT_PITCH, r), :] = filt_ref[pl.ds(a * r, r), :]

    def stage1(bp, carry):
        rhs = jnp.concatenate(
            [zs_ref[pl.ds(2 * bp + e, a_half, stride=FFT_PITCH), :] for e in range(2)], axis=1)
        t = jnp.dot(f1_ref[...], rhs, precision=hp, preferred_element_type=F32)
        for e in range(2):
            for c in range(2):
                ts_ref[c, pl.ds(2 * bp + e, a_n, stride=FFT_PITCH), :] = (
                    t[c * a_n:(c + 1) * a_n, e * LANES:(e + 1) * LANES])
        return carry

    lax.fori_loop(0, r // 2, stage1, 0)

    def stage2(ka, carry):
        rows = pl.ds(pl.multiple_of(ka * FFT_PITCH, 8), r)
        rhs = jnp.concatenate([ts_ref[0, rows, :], ts_ref[1, rows, :]], axis=0)
        s = jnp.dot(g_ref[ka], rhs, precision=hp, preferred_element_type=F32)
        out = pl.ds(pl.multiple_of(ka * r, r), r)
        hr_ref[out, :] = s[:r]
        hi_ref[out, :] = s[r:]
        return carry

    lax.fori_loop(0, a_n, stage2, 0)


def _filter_spectrum(filt, seq):
    hy_w = filt.shape[1]
    n = 2 * seq
    _, f1_real, _, g_fwd, _ = _dft_constants(seq)
    a_n = n // FFT_R
    out = jax.ShapeDtypeStruct((n, hy_w), F32)
    return pl.pallas_call(
        functools.partial(_spectrum_kernel, seq=seq),
        out_shape=(out, out),
        grid=(hy_w // LANES,),
        in_specs=[
            pl.BlockSpec((seq, LANES), lambda c: (0, c)),
            pl.BlockSpec(f1_real.shape, lambda c: (0, 0)),
            pl.BlockSpec(g_fwd.shape, lambda c: (0, 0, 0), pipeline_mode=pl.Buffered(1)),
        ],
        out_specs=(pl.BlockSpec((n, LANES), lambda c: (0, c)),
                   pl.BlockSpec((n, LANES), lambda c: (0, c))),
        scratch_shapes=[
            pltpu.VMEM((a_n // 2 * FFT_PITCH, LANES), F32),
            pltpu.VMEM((2, a_n * FFT_PITCH, LANES), F32),
        ],
        compiler_params=_cparams(("parallel",)),
        name="filter_spectrum",
    )(filt, jnp.asarray(f1_real), jnp.asarray(g_fwd))


def _hy_front_kernel(x0_ref, x1_ref, hv_ref, w0_ref, w1_ref, wv_ref, b0_ref, b1_ref, bv_ref,
                     z_ref, x0c_ref, pad_ref, *, seq):
    chunk = 512
    zero_row = jnp.zeros((8, LANES), F32)
    for idx, ref in enumerate((x0_ref, x1_ref, hv_ref)):
        pad_ref[idx, pl.ds(0, 8), :] = zero_row
        pad_ref[idx, pl.ds(8 + seq, 8), :] = zero_row
        pad_ref[idx, pl.ds(8, seq), :] = ref[...].astype(F32)

    def conv(idx, w_ref, b_ref, c):
        base = 8 + c * chunk
        w = w_ref[...]
        return (pad_ref[idx, pl.ds(base - 1, chunk), :] * w[0:1]
                + pad_ref[idx, pl.ds(base, chunk), :] * w[1:2]
                + pad_ref[idx, pl.ds(base + 1, chunk), :] * w[2:3]
                + b_ref[...])

    for c in range(seq // chunk):
        rows = pl.ds(c * chunk, chunk)
        x0c_ref[rows, :] = conv(0, w0_ref, b0_ref, c).astype(BF16)
        z_ref[rows, :] = (conv(2, wv_ref, bv_ref, c) * conv(1, w1_ref, b1_ref, c)).astype(BF16)


def _hyena_front(u_t, conv_w, conv_b, batch, seq, heads, groups):
    ng = 3 * groups
    cw = conv_w.astype(F32).reshape(3, ng, LANES).transpose(1, 0, 2)
    cb = conv_b.astype(F32).reshape(ng, 1, LANES)
    base = 3 * heads
    uspec = lambda off: pl.BlockSpec((None, seq, LANES), lambda b, g: (base + off + g, b, 0))
    wspec = lambda off: pl.BlockSpec((None, 3, LANES), lambda b, g: (off + g, 0, 0))
    bspec = lambda off: pl.BlockSpec((None, 1, LANES), lambda b, g: (off + g, 0, 0))
    out = jax.ShapeDtypeStruct((groups, batch * seq, LANES), BF16)
    ospec = pl.BlockSpec((None, seq, LANES), lambda b, g: (g, b, 0))
    return pl.pallas_call(
        functools.partial(_hy_front_kernel, seq=seq),
        out_shape=(out, out),
        grid=(batch, groups),
        in_specs=[uspec(0), uspec(groups), uspec(2 * groups),
                  wspec(0), wspec(groups), wspec(2 * groups),
                  bspec(0), bspec(groups), bspec(2 * groups)],
        out_specs=(ospec, ospec),
        scratch_shapes=[pltpu.VMEM((3, seq + 16, LANES), F32)],
        compiler_params=_cparams(("parallel", "parallel")),
        name="hyena_front",
    )(u_t, u_t, u_t, cw, cw, cw, cb, cb, cb)


def _hy_conv_kernel(z_ref, x_ref, hr_ref, hi_ref, f1_ref, f1i_ref,
                    g_ref, gi_ref, bias_ref, gain_ref, o_ref,
                    zs_ref, ts_ref, ys_ref, *, seq):
    r = FFT_R
    a_half = seq // r
    a_n = 2 * a_half
    for c in range(2):
        for a in range(a_half):
            zs_ref[c, pl.ds(a * FFT_PITCH, r), :] = (
                z_ref[pl.ds(c * seq + a * r, r), :].astype(F32))

    def stage1(bp, carry):
        cols = []
        for e in range(2):
            rows = pl.ds(2 * bp + e, a_half, stride=FFT_PITCH)
            cols.append(jnp.concatenate([zs_ref[0, rows, :], zs_ref[1, rows, :]], axis=0))
        rhs = jnp.concatenate(cols, axis=1).astype(BF16)
        t = jnp.dot(f1_ref[...], rhs, preferred_element_type=F32)
        for e in range(2):
            for c in range(2):
                ts_ref[c, pl.ds(2 * bp + e, a_n, stride=FFT_PITCH), :] = (
                    t[c * a_n:(c + 1) * a_n, e * LANES:(e + 1) * LANES])
        return carry

    lax.fori_loop(0, r // 2, stage1, 0)

    def stage2(ka, carry):
        rows = pl.ds(pl.multiple_of(ka * FFT_PITCH, 8), r)
        rhs = jnp.concatenate([ts_ref[0, rows, :], ts_ref[1, rows, :]], axis=0).astype(BF16)
        s = jnp.dot(g_ref[ka], rhs, preferred_element_type=F32)
        hrows = pl.ds(pl.multiple_of(ka * r, r), r)
        hr = hr_ref[hrows, :]
        hi = hi_ref[hrows, :]
        sr, si = s[:r], s[r:]
        prod = jnp.concatenate([sr * hr - si * hi, sr * hi + si * hr], axis=0).astype(BF16)
        u = jnp.dot(gi_ref[ka], prod, preferred_element_type=F32)
        ts_ref[0, rows, :] = u[:r]
        ts_ref[1, rows, :] = u[r:]
        return carry

    lax.fori_loop(0, a_n, stage2, 0)

    def stage3(bp, carry):
        cols = []
        for e in range(2):
            rows = pl.ds(2 * bp + e, a_n, stride=FFT_PITCH)
            cols.append(jnp.concatenate([ts_ref[0, rows, :], ts_ref[1, rows, :]], axis=0))
        rhs = jnp.concatenate(cols, axis=1).astype(BF16)
        y = jnp.dot(f1i_ref[...], rhs, preferred_element_type=F32)
        for e in range(2):
            for c in range(2):
                ys_ref[c, pl.ds(2 * bp + e, a_half, stride=FFT_PITCH), :] = (
                    y[c * a_half:(c + 1) * a_half, e * LANES:(e + 1) * LANES])
        return carry

    lax.fori_loop(0, r // 2, stage3, 0)

    def finish(a, carry):
        prow = pl.ds(pl.multiple_of(a * FFT_PITCH, 8), r)
        for c in range(2):
            orow = pl.ds(pl.multiple_of(c * seq + a * r, r), r)
            z = ys_ref[c, prow, :] + zs_ref[c, prow, :] * bias_ref[...]
            y = z * x_ref[orow, :].astype(F32)
            o_ref[orow, :] = _rms(y, gain_ref[...]).astype(BF16)
        return carry

    lax.fori_loop(0, a_half, finish, 0)


def _hyena_conv(z_t, x0c_t, h_re, h_im, hy_bias, hy_out_norm, batch, seq, groups):
    n = 2 * seq
    f1_fwd, _, f1_inv, g_fwd, g_inv = _dft_constants(seq)
    a_n = n // FFT_R
    zspec = pl.BlockSpec((None, 2 * seq, LANES), lambda g, p: (g, p, 0))
    hspec = pl.BlockSpec((n, LANES), lambda g, p: (0, g), pipeline_mode=pl.Buffered(1))
    cspec = lambda arr: pl.BlockSpec(arr.shape, lambda g, p: (0,) * arr.ndim,
                                     pipeline_mode=pl.Buffered(1))
    vspec = pl.BlockSpec((None, 1, LANES), lambda g, p: (g, 0, 0))
    out = jax.ShapeDtypeStruct((groups, batch * seq, LANES), BF16)
    return pl.pallas_call(
        functools.partial(_hy_conv_kernel, seq=seq),
        out_shape=out,
        grid=(groups, batch // 2),
        in_specs=[zspec, zspec, hspec, hspec,
                  cspec(f1_fwd), cspec(f1_inv), cspec(g_fwd), cspec(g_inv), vspec, vspec],
        out_specs=zspec,
        scratch_shapes=[
            pltpu.VMEM((2, a_n // 2 * FFT_PITCH, LANES), F32),
            pltpu.VMEM((2, a_n * FFT_PITCH, LANES), F32),
            pltpu.VMEM((2, a_n // 2 * FFT_PITCH, LANES), F32),
        ],
        compiler_params=_cparams(("parallel", "arbitrary")),
        name="hyena_conv",
    )(z_t, x0c_t, h_re, h_im,
      jnp.asarray(f1_fwd, BF16), jnp.asarray(f1_inv, BF16),
      jnp.asarray(g_fwd, BF16), jnp.asarray(g_inv, BF16),
      hy_bias.astype(F32).reshape(groups, 1, LANES),
      hy_out_norm.astype(F32).reshape(groups, 1, LANES))


def _out_proj_kernel(ya_ref, yh_ref, x_ref, w_ref, o_ref, *, heads, groups):
    lhs = jnp.concatenate([ya_ref[g] for g in range(heads)]
                          + [yh_ref[g] for g in range(groups)], axis=-1)
    o_ref[...] = x_ref[...] + jnp.dot(lhs, w_ref[...], preferred_element_type=F32)


def _out_proj(ya_t, yh_t, x2, w_out, tm):
    m, d = x2.shape
    heads, groups = ya_t.shape[0], yh_t.shape[0]
    return pl.pallas_call(
        functools.partial(_out_proj_kernel, heads=heads, groups=groups),
        out_shape=jax.ShapeDtypeStruct((m, d), F32),
        grid=(m // tm,),
        in_specs=[
            pl.BlockSpec((heads, tm, LANES), lambda i: (0, i, 0)),
            pl.BlockSpec((groups, tm, LANES), lambda i: (0, i, 0)),
            pl.BlockSpec((tm, d), lambda i: (i, 0)),
            pl.BlockSpec(w_out.shape, lambda i: (0, 0)),
        ],
        out_specs=pl.BlockSpec((tm, d), lambda i: (i, 0)),
        compiler_params=_cparams(("parallel",)),
        name="out_proj",
    )(ya_t, yh_t, x2, w_out.astype(BF16))


def _ffn_kernel(h_ref, g_ref, wa_ref, wg_ref, wd_ref, o_ref, xn_ref, acc_ref):
    j = pl.program_id(1)

    @pl.when(j == 0)
    def _():
        xn_ref[...] = _rms(h_ref[...], g_ref[...]).astype(BF16)
        acc_ref[...] = jnp.zeros_like(acc_ref)

    xn = xn_ref[...]
    a = jnp.dot(xn, wa_ref[...], preferred_element_type=F32)
    g = jnp.dot(xn, wg_ref[...], preferred_element_type=F32)
    act = (a * jax.nn.sigmoid(a) * g).astype(BF16)
    acc_ref[...] += jnp.dot(act, wd_ref[...], preferred_element_type=F32)

    @pl.when(j == pl.num_programs(1) - 1)
    def _():
        o_ref[...] = h_ref[...] + acc_ref[...]


def _ffn(h1, norm2, w_gu, w_down, tm, th):
    m, d = h1.shape
    hidden = w_down.shape[0]
    nh = hidden // th
    w_gu = w_gu.astype(BF16)
    return pl.pallas_call(
        _ffn_kernel,
        out_shape=jax.ShapeDtypeStruct((m, d), F32),
        grid=(m // tm, nh),
        in_specs=[
            pl.BlockSpec((tm, d), lambda i, j: (i, 0)),
            pl.BlockSpec((1, d), lambda i, j: (0, 0)),
            pl.BlockSpec((d, th), lambda i, j: (0, j)),
            pl.BlockSpec((d, th), lambda i, j: (0, nh + j)),
            pl.BlockSpec((th, d), lambda i, j: (j, 0)),
        ],
        out_specs=pl.BlockSpec((tm, d), lambda i, j: (i, 0)),
        scratch_shapes=[pltpu.VMEM((tm, d), BF16), pltpu.VMEM((tm, d), F32)],
        compiler_params=_cparams(("parallel", "arbitrary")),
        name="ffn",
    )(h1, norm2.reshape(1, d), w_gu, w_gu, w_down.astype(BF16))


def _ple_kernel(h_ref, p_ref, gn_ref, wg_ref, wp_ref, pn_ref, o_ref):
    h = h_ref[...]
    gate = jax.nn.sigmoid(jnp.dot(_rms(h, gn_ref[...]).astype(BF16), wg_ref[...],
                                  preferred_element_type=F32))
    e = _rms(jnp.dot(p_ref[...].astype(BF16), wp_ref[...], preferred_element_type=F32),
             pn_ref[...])
    o_ref[...] = h + gate * e


def _ple(h2, p2, ple_norm, w_gate, w_proj, ple_post_norm, tm):
    m, d = h2.shape
    pd = p2.shape[1]
    return pl.pallas_call(
        _ple_kernel,
        out_shape=jax.ShapeDtypeStruct((m, d), F32),
        grid=(m // tm,),
        in_specs=[
            pl.BlockSpec((tm, d), lambda i: (i, 0)),
            pl.BlockSpec((tm, pd), lambda i: (i, 0)),
            pl.BlockSpec((1, d), lambda i: (0, 0)),
            pl.BlockSpec((d, d), lambda i: (0, 0)),
            pl.BlockSpec((pd, d), lambda i: (0, 0)),
            pl.BlockSpec((1, d), lambda i: (0, 0)),
        ],
        out_specs=pl.BlockSpec((tm, d), lambda i: (i, 0)),
        compiler_params=_cparams(("parallel",)),
        name="ple",
    )(h2, p2, ple_norm.reshape(1, d), w_gate.astype(BF16), w_proj.astype(BF16),
      ple_post_norm.reshape(1, d))


def kernel(x, p, rel_bias, norm1, w_in, q_norm, k_norm, conv_w, conv_b, hy_w1, hy_b1, hy_wi, hy_bi, hy_wo, hy_freq, hy_decay, hy_bias, attn_out_norm, hy_out_norm, w_out, norm2, w_gu, w_down, ple_norm, w_ple_gate, w_ple_proj, ple_post_norm):
    batch, seq, d = x.shape
    attn_w = d // 2
    heads = attn_w // HEAD_DIM
    groups = heads
    hy_w = groups * LANES
    m = batch * seq
    tm = 512
    h = x.reshape(m, d)
    bias_tab = _attn_bias_tables(rel_bias)
    for i in range(norm1.shape[0]):
        u_t = _in_proj(h, norm1[i], w_in[i], q_norm[i], k_norm[i], attn_w, tm)
        ya_t = _attention(u_t, bias_tab, attn_out_norm[i], batch, seq, heads)
        filt = _hyena_filter(seq, hy_w, hy_w1[i], hy_b1[i], hy_wi[i], hy_bi[i], hy_wo[i],
                             hy_freq[i], hy_decay[i])
        h_re, h_im = _filter_spectrum(filt, seq)
        z_t, x0c_t = _hyena_front(u_t, conv_w[i], conv_b[i], batch, seq, heads, groups)
        yh_t = _hyena_conv(z_t, x0c_t, h_re, h_im, hy_bias[i], hy_out_norm[i],
                           batch, seq, groups)
        h = _out_proj(ya_t, yh_t, h, w_out[i], tm)
        h = _ffn(h, norm2[i], w_gu[i], w_down[i], tm, 512)
        h = _ple(h, p[i].reshape(m, PLE_DIM), ple_norm[i], w_ple_gate[i], w_ple_proj[i],
                 ple_post_norm[i], tm)
    return h.reshape(batch, seq, d)
```

```python
import functools
import math

import jax
import jax.numpy as jnp
import numpy as np
from jax import lax
from jax.experimental import pallas as pl
from jax.experimental.pallas import tpu as pltpu

F32 = jnp.float32
BF16 = jnp.bfloat16

LANES = 128
HEAD_DIM = 128
ATTN_PATTERNS = ((128, 1), (512, 4), (2048, 16))
BAND = 64
QBLK = 128
KBLK = QBLK + 2 * BAND
N_BUCKETS = 32
REL_MAX_DIST = 1024
HY_EMB = 33
HY_FILTER_WIDTH = 64
HY_INNER = 2
EPS = 1e-6
NEG = -1e30
PLE_DIM = 256

FFT_R = 64
FFT_PITCH = FFT_R + 8
VMEM_LIMIT = 56 * 1024 * 1024


def _cparams(sem, vmem=VMEM_LIMIT):
    return pltpu.CompilerParams(dimension_semantics=sem, vmem_limit_bytes=vmem)


def _rms(x, gain):
    ms = jnp.mean(x * x, axis=-1, keepdims=True)
    return x * lax.rsqrt(ms + EPS) * gain


def _in_proj_kernel(x_ref, g_ref, w_ref, qg_ref, kg_ref, o_ref, xn_ref, *, nq, hpt):
    j = pl.program_id(1)

    @pl.when(j == 0)
    def _():
        xn_ref[...] = _rms(x_ref[...], g_ref[...]).astype(BF16)

    acc = jnp.dot(xn_ref[...], w_ref[...], preferred_element_type=F32)

    def write(gain):
        for hh in range(hpt):
            a = acc[:, hh * LANES:(hh + 1) * LANES]
            if gain is not None:
                a = _rms(a, gain)
            o_ref[hh] = a.astype(BF16)

    @pl.when(j < nq)
    def _():
        write(qg_ref[...] * (HEAD_DIM ** -0.5))

    @pl.when((j >= nq) & (j < 2 * nq))
    def _():
        write(kg_ref[...])

    @pl.when(j >= 2 * nq)
    def _():
        write(None)


def _in_proj(x2, norm1, w_in, q_norm, k_norm, attn_w, tm):
    m, d = x2.shape
    in_w = w_in.shape[1]
    tn = min(512, attn_w)
    hpt = tn // LANES
    nq = attn_w // tn
    return pl.pallas_call(
        functools.partial(_in_proj_kernel, nq=nq, hpt=hpt),
        out_shape=jax.ShapeDtypeStruct((in_w // LANES, m, LANES), BF16),
        grid=(m // tm, in_w // tn),
        in_specs=[
            pl.BlockSpec((tm, d), lambda i, j: (i, 0)),
            pl.BlockSpec((1, d), lambda i, j: (0, 0)),
            pl.BlockSpec((d, tn), lambda i, j: (0, j)),
            pl.BlockSpec((1, LANES), lambda i, j: (0, 0)),
            pl.BlockSpec((1, LANES), lambda i, j: (0, 0)),
        ],
        out_specs=pl.BlockSpec((hpt, tm, LANES), lambda i, j: (j, i, 0)),
        scratch_shapes=[pltpu.VMEM((tm, d), BF16)],
        compiler_params=_cparams(("parallel", "arbitrary")),
        name="in_proj",
    )(x2, norm1.reshape(1, d), w_in.astype(BF16), q_norm.reshape(1, LANES),
      k_norm.reshape(1, LANES))


def _t5_bucket(rel):
    half = N_BUCKETS // 2
    exact = half // 2
    n = np.abs(rel)
    large = exact + (np.log(np.maximum(n, 1).astype(np.float32) / np.float32(exact))
                     / np.float32(math.log(REL_MAX_DIST / exact))
                     * np.float32(half - exact)).astype(np.int32)
    large = np.minimum(large, half - 1)
    return np.where(rel > 0, half, 0) + np.where(n < exact, n, large)


def _attn_bias_tables(rel_bias):
    qi = np.arange(QBLK)[:, None]
    kc = np.arange(KBLK)[None, :]
    rel = kc - BAND - qi
    band = np.abs(rel) <= BAND
    first_ok = np.broadcast_to(kc >= BAND, rel.shape)
    last_ok = np.broadcast_to(kc < QBLK + BAND, rel.shape)
    tabs = []
    for _, dil in ATTN_PATTERNS:
        onehot = np.eye(N_BUCKETS, dtype=np.float32)[_t5_bucket(rel * dil)]
        bias = jnp.einsum("qkn,nh->hqk", jnp.asarray(onehot), rel_bias.astype(F32),
                          precision=lax.Precision.HIGHEST)
        variants = [jnp.where(jnp.asarray(band & ok), bias, NEG)
                    for ok in (first_ok, np.ones_like(band), last_ok)]
        tabs.append(jnp.stack(variants, axis=1))
    return jnp.stack(tabs, axis=1)


def _attn_kernel(q_ref, k_ref, v_ref, bias_ref, gain_ref, o_ref,
                 xf_ref, qp_ref, kp_ref, vp_ref, acc_ref, m_ref, l_ref, *, seq):
    zeros_pad = jnp.zeros((BAND, LANES), BF16)

    for pidx, (_, dil) in enumerate(ATTN_PATTERNS):
        ls = seq // dil
        nb = ls // QBLK
        seg = ls + 2 * BAND

        if dil == 1:
            kp_ref[pl.ds(BAND, seq), :] = k_ref[...]
            vp_ref[pl.ds(BAND, seq), :] = v_ref[...]
        else:
            if pidx == 1:
                xf_ref[0] = q_ref[...].astype(F32)
                xf_ref[1] = k_ref[...].astype(F32)
                xf_ref[2] = v_ref[...].astype(F32)
            for r in range(dil):
                qp_ref[pl.ds(r * ls, ls), :] = (
                    xf_ref[0, pl.ds(r, ls, stride=dil), :].astype(BF16))
                kp_ref[pl.ds(r * seg + BAND, ls), :] = (
                    xf_ref[1, pl.ds(r, ls, stride=dil), :].astype(BF16))
                vp_ref[pl.ds(r * seg + BAND, ls), :] = (
                    xf_ref[2, pl.ds(r, ls, stride=dil), :].astype(BF16))
        for r in range(dil):
            for ref in (kp_ref, vp_ref):
                ref[pl.ds(r * seg, BAND), :] = zeros_pad
                ref[pl.ds(r * seg + BAND + ls, BAND), :] = zeros_pad

        def block(t, carry, dil=dil, ls=ls, nb=nb, seg=seg, pidx=pidx):
            r = t // nb
            qb = t % nb
            qoff = pl.multiple_of(r * ls + qb * QBLK, QBLK)
            koff = pl.multiple_of(r * seg + qb * QBLK, QBLK)
            if dil == 1:
                q = q_ref[pl.ds(qoff, QBLK), :]
            else:
                q = qp_ref[pl.ds(qoff, QBLK), :]
            k = kp_ref[pl.ds(koff, KBLK), :]
            v = vp_ref[pl.ds(koff, KBLK), :]
            variant = jnp.where(qb == 0, 0, jnp.where(qb == nb - 1, 2, 1))
            s = lax.dot_general(q, k, (((1,), (1,)), ((), ())),
                                preferred_element_type=F32)
            s = s + bias_ref[pidx, variant]
            m_blk = jnp.max(s, axis=-1, keepdims=True)
            p = jnp.exp(s - m_blk)
            l_blk = jnp.sum(p, axis=-1, keepdims=True)
            a_blk = jnp.dot(p.astype(BF16), v, preferred_element_type=F32)
            m_b = jnp.broadcast_to(m_blk, (QBLK, LANES))
            l_b = jnp.broadcast_to(l_blk, (QBLK, LANES))
            if dil == 1:
                rows = pl.ds(qoff, QBLK)
                acc_ref[rows, :] = a_blk
                m_ref[rows, :] = m_b
                l_ref[rows, :] = l_b
            else:
                rows = pl.ds(r + qb * (QBLK * dil), QBLK, stride=dil)
                m_old = m_ref[rows, :]
                m_new = jnp.maximum(m_old, m_b)
                w_old = jnp.exp(m_old - m_new)
                w_blk = jnp.exp(m_b - m_new)
                acc_ref[rows, :] = acc_ref[rows, :] * w_old + a_blk * w_blk
                l_ref[rows, :] = l_ref[rows, :] * w_old + l_b * w_blk
                m_ref[rows, :] = m_new
            return carry

        lax.fori_loop(0, dil * nb, block, 0, unroll=8)

    def finish(c, carry):
        rows = pl.ds(pl.multiple_of(c * 256, 256), 256)
        y = acc_ref[rows, :] / l_ref[rows, :]
        o_ref[rows, :] = _rms(y, gain_ref[...]).astype(BF16)
        return carry

    lax.fori_loop(0, seq // 256, finish, 0)


def _attention(u_t, bias_tab, attn_out_norm, batch, seq, heads):
    max_dil = ATTN_PATTERNS[-1][1]
    pad_rows = seq + 2 * BAND * max_dil
    return pl.pallas_call(
        functools.partial(_attn_kernel, seq=seq),
        out_shape=jax.ShapeDtypeStruct((heads, batch * seq, LANES), BF16),
        grid=(batch, heads),
        in_specs=[
            pl.BlockSpec((None, seq, LANES), lambda b, h: (h, b, 0)),
            pl.BlockSpec((None, seq, LANES), lambda b, h: (heads + h, b, 0)),
            pl.BlockSpec((None, seq, LANES), lambda b, h: (2 * heads + h, b, 0)),
            pl.BlockSpec((None, len(ATTN_PATTERNS), 3, QBLK, KBLK),
                         lambda b, h: (h, 0, 0, 0, 0)),
            pl.BlockSpec((None, 1, LANES), lambda b, h: (h, 0, 0)),
        ],
        out_specs=pl.BlockSpec((None, seq, LANES), lambda b, h: (h, b, 0)),
        scratch_shapes=[
            pltpu.VMEM((3, seq, LANES), F32),
            pltpu.VMEM((seq, LANES), BF16),
            pltpu.VMEM((pad_rows, LANES), BF16),
            pltpu.VMEM((pad_rows, LANES), BF16),
            pltpu.VMEM((seq, LANES), F32),
            pltpu.VMEM((seq, LANES), F32),
            pltpu.VMEM((seq, LANES), F32),
        ],
        compiler_params=_cparams(("parallel", "parallel")),
        name="attention",
    )(u_t, u_t, u_t, bias_tab, attn_out_norm.reshape(heads, 1, LANES))


def _filter_features(seq):
    pos = np.arange(seq, dtype=np.float32)
    t = pos / np.float32(max(seq - 1, 1))
    bands = (HY_EMB - 1) // 2
    fr = np.linspace(1e-4, bands - 1, bands, dtype=np.float32)
    ang = np.float32(2.0 * math.pi / seq) * pos[:, None] * fr[None, :]
    z = np.concatenate([t[:, None], np.cos(ang), -np.sin(ang)], axis=-1).astype(np.float32)
    zp = np.zeros((seq, LANES), np.float32)
    zp[:, :HY_EMB] = z
    offs = (np.abs(pos - (seq // 2)) / np.float32(seq / 2)).astype(np.float32)
    return zp, offs[:, None]


def _filter_kernel(z_ref, offs_ref, w1_ref, b1_ref, wi_ref, bi_ref, wo_ref, fq_ref,
                   decay_ref, o_ref):
    hp = lax.Precision.HIGHEST
    fq = fq_ref[...]
    h = jnp.sin(fq * (jnp.dot(z_ref[...], w1_ref[...], precision=hp,
                              preferred_element_type=F32) + b1_ref[...]))
    for j in range(HY_INNER):
        h = jnp.sin(fq * (jnp.dot(h, wi_ref[j], precision=hp,
                                  preferred_element_type=F32) + bi_ref[j]))
    filt = jnp.dot(h, wo_ref[...], precision=hp, preferred_element_type=F32)
    o_ref[...] = filt * jnp.exp(-offs_ref[...] * jnp.abs(decay_ref[...]))


def _hyena_filter(seq, hy_w, w1, b1, wi, bi, wo, freq, decay):
    zp, offs = _filter_features(seq)
    w1p = jnp.zeros((LANES, HY_FILTER_WIDTH), F32).at[:HY_EMB].set(w1.astype(F32))
    tr = 512
    fw = HY_FILTER_WIDTH
    const = lambda *shape: pl.BlockSpec(shape, lambda i: (0,) * len(shape))
    return pl.pallas_call(
        _filter_kernel,
        out_shape=jax.ShapeDtypeStruct((seq, hy_w), F32),
        grid=(seq // tr,),
        in_specs=[
            pl.BlockSpec((tr, LANES), lambda i: (i, 0)),
            pl.BlockSpec((tr, 1), lambda i: (i, 0)),
            const(LANES, fw), const(1, fw), const(HY_INNER, fw, fw),
            const(HY_INNER, 1, fw), const(fw, hy_w), const(1, fw), const(1, hy_w),
        ],
        out_specs=pl.BlockSpec((tr, hy_w), lambda i: (i, 0)),
        compiler_params=_cparams(("parallel",)),
        name="hyena_filter",
    )(jnp.asarray(zp), jnp.asarray(offs), w1p, b1.reshape(1, fw).astype(F32),
      wi.astype(F32), bi.reshape(HY_INNER, 1, fw).astype(F32), wo.astype(F32),
      freq.reshape(1, fw).astype(F32), decay.reshape(1, hy_w).astype(F32))


def _stack(c):
    return np.block([[c.real, -c.imag], [c.imag, c.real]])


@functools.lru_cache(maxsize=None)
def _dft_constants(seq):
    n = 2 * seq
    r = FFT_R
    a_n = n // r
    ar = np.arange(a_n)
    br = np.arange(r)
    f1 = np.exp(-2j * np.pi * np.outer(ar, ar) / a_n)
    f1_fwd = _stack(f1[:, :a_n // 2])
    f1_fwd_real = np.concatenate([f1[:, :a_n // 2].real, f1[:, :a_n // 2].imag], 0)
    f1_inv = _stack(np.conj(f1).T[a_n // 4:3 * a_n // 4] / n)
    ph = (br[None, None, :] * br[None, :, None] / r
          + br[None, None, :] * ar[:, None, None] / n)
    g = np.exp(-2j * np.pi * ph)
    g_fwd = np.stack([_stack(g[k]) for k in range(a_n)])
    g_inv = np.stack([_stack(np.conj(g[k]).T) for k in range(a_n)])
    return (f1_fwd.astype(np.float32), f1_fwd_real.astype(np.float32),
            f1_inv.astype(np.float32), g_fwd.astype(np.float32), g_inv.astype(np.float32))


def _spectrum_kernel(filt_ref, f1_ref, g_ref, hr_ref, hi_ref, zs_ref, ts_ref, *, seq):
    hp = lax.Precision.HIGHEST
    r = FFT_R
    a_half = seq // r
    a_n = 2 * a_half
    for a in range(a_half):
        zs_ref[pl.ds(a * FFT_PITCH, r), :] = filt_ref[pl.ds(a * r, r), :]

    def stage1(bp, carry):
        rhs = jnp.concatenate(
            [zs_ref[pl.ds(2 * bp + e, a_half, stride=FFT_PITCH), :] for e in range(2)], axis=1)
        t = jnp.dot(f1_ref[...], rhs, precision=hp, preferred_element_type=F32)
        for e in range(2):
            for c in range(2):
                ts_ref[c, pl.ds(2 * bp + e, a_n, stride=FFT_PITCH), :] = (
                    t[c * a_n:(c + 1) * a_n, e * LANES:(e + 1) * LANES])
        return carry

    lax.fori_loop(0, r // 2, stage1, 0)

    def stage2(ka, carry):
        rows = pl.ds(pl.multiple_of(ka * FFT_PITCH, 8), r)
        rhs = jnp.concatenate([ts_ref[0, rows, :], ts_ref[1, rows, :]], axis=0)
        s = jnp.dot(g_ref[ka], rhs, precision=hp, preferred_element_type=F32)
        out = pl.ds(pl.multiple_of(ka * r, r), r)
        hr_ref[out, :] = s[:r]
        hi_ref[out, :] = s[r:]
        return carry

    lax.fori_loop(0, a_n, stage2, 0)


def _filter_spectrum(filt, seq):
    hy_w = filt.shape[1]
    n = 2 * seq
    _, f1_real, _, g_fwd, _ = _dft_constants(seq)
    a_n = n // FFT_R
    out = jax.ShapeDtypeStruct((n, hy_w), F32)
    return pl.pallas_call(
        functools.partial(_spectrum_kernel, seq=seq),
        out_shape=(out, out),
        grid=(hy_w // LANES,),
        in_specs=[
            pl.BlockSpec((seq, LANES), lambda c: (0, c)),
            pl.BlockSpec(f1_real.shape, lambda c: (0, 0)),
            pl.BlockSpec(g_fwd.shape, lambda c: (0, 0, 0), pipeline_mode=pl.Buffered(1)),
        ],
        out_specs=(pl.BlockSpec((n, LANES), lambda c: (0, c)),
                   pl.BlockSpec((n, LANES), lambda c: (0, c))),
        scratch_shapes=[
            pltpu.VMEM((a_n // 2 * FFT_PITCH, LANES), F32),
            pltpu.VMEM((2, a_n * FFT_PITCH, LANES), F32),
        ],
        compiler_params=_cparams(("parallel",)),
        name="filter_spectrum",
    )(filt, jnp.asarray(f1_real), jnp.asarray(g_fwd))


def _hy_front_kernel(x0_ref, x1_ref, hv_ref, w0_ref, w1_ref, wv_ref, b0_ref, b1_ref, bv_ref,
                     z_ref, x0c_ref, pad_ref, *, seq):
    chunk = 512
    zero_row = jnp.zeros((8, LANES), F32)
    for idx, ref in enumerate((x0_ref, x1_ref, hv_ref)):
        pad_ref[idx, pl.ds(0, 8), :] = zero_row
        pad_ref[idx, pl.ds(8 + seq, 8), :] = zero_row
        pad_ref[idx, pl.ds(8, seq), :] = ref[...].astype(F32)

    def conv(idx, w_ref, b_ref, c):
        base = 8 + c * chunk
        w = w_ref[...]
        return (pad_ref[idx, pl.ds(base - 1, chunk), :] * w[0:1]
                + pad_ref[idx, pl.ds(base, chunk), :] * w[1:2]
                + pad_ref[idx, pl.ds(base + 1, chunk), :] * w[2:3]
                + b_ref[...])

    for c in range(seq // chunk):
        rows = pl.ds(c * chunk, chunk)
        x0c_ref[rows, :] = conv(0, w0_ref, b0_ref, c).astype(BF16)
        z_ref[rows, :] = (conv(2, wv_ref, bv_ref, c) * conv(1, w1_ref, b1_ref, c)).astype(BF16)


def _hyena_front(u_t, conv_w, conv_b, batch, seq, heads, groups):
    ng = 3 * groups
    cw = conv_w.astype(F32).reshape(3, ng, LANES).transpose(1, 0, 2)
    cb = conv_b.astype(F32).reshape(ng, 1, LANES)
    base = 3 * heads
    uspec = lambda off: pl.BlockSpec((None, seq, LANES), lambda b, g: (base + off + g, b, 0))
    wspec = lambda off: pl.BlockSpec((None, 3, LANES), lambda b, g: (off + g, 0, 0))
    bspec = lambda off: pl.BlockSpec((None, 1, LANES), lambda b, g: (off + g, 0, 0))
    out = jax.ShapeDtypeStruct((groups, batch * seq, LANES), BF16)
    ospec = pl.BlockSpec((None, seq, LANES), lambda b, g: (g, b, 0))
    return pl.pallas_call(
        functools.partial(_hy_front_kernel, seq=seq),
        out_shape=(out, out),
        grid=(batch, groups),
        in_specs=[uspec(0), uspec(groups), uspec(2 * groups),
                  wspec(0), wspec(groups), wspec(2 * groups),
                  bspec(0), bspec(groups), bspec(2 * groups)],
        out_specs=(ospec, ospec),
        scratch_shapes=[pltpu.VMEM((3, seq + 16, LANES), F32)],
        compiler_params=_cparams(("parallel", "parallel")),
        name="hyena_front",
    )(u_t, u_t, u_t, cw, cw, cw, cb, cb, cb)


def _hy_conv_kernel(z_ref, x_ref, hr_ref, hi_ref, f1_ref, f1i_ref,
                    g_ref, gi_ref, bias_ref, gain_ref, o_ref,
                    zs_ref, ts_ref, ys_ref, *, seq):
    r = FFT_R
    a_half = seq // r
    a_n = 2 * a_half
    for c in range(2):
        for a in range(a_half):
            zs_ref[c, pl.ds(a * FFT_PITCH, r), :] = (
                z_ref[pl.ds(c * seq + a * r, r), :].astype(F32))

    def stage1(bp, carry):
        cols = []
        for e in range(2):
            rows = pl.ds(2 * bp + e, a_half, stride=FFT_PITCH)
            cols.append(jnp.concatenate([zs_ref[0, rows, :], zs_ref[1, rows, :]], axis=0))
        rhs = jnp.concatenate(cols, axis=1).astype(BF16)
        t = jnp.dot(f1_ref[...], rhs, preferred_element_type=F32)
        for e in range(2):
            for c in range(2):
                ts_ref[c, pl.ds(2 * bp + e, a_n, stride=FFT_PITCH), :] = (
                    t[c * a_n:(c + 1) * a_n, e * LANES:(e + 1) * LANES])
        return carry

    lax.fori_loop(0, r // 2, stage1, 0, unroll=4)

    def stage2(ka, carry):
        rows = pl.ds(pl.multiple_of(ka * FFT_PITCH, 8), r)
        rhs = jnp.concatenate([ts_ref[0, rows, :], ts_ref[1, rows, :]], axis=0).astype(BF16)
        s = jnp.dot(g_ref[ka], rhs, preferred_element_type=F32)
        hrows = pl.ds(pl.multiple_of(ka * r, r), r)
        hr = hr_ref[hrows, :]
        hi = hi_ref[hrows, :]
        sr, si = s[:r], s[r:]
        prod = jnp.concatenate([sr * hr - si * hi, sr * hi + si * hr], axis=0).astype(BF16)
        u = jnp.dot(gi_ref[ka], prod, preferred_element_type=F32)
        ts_ref[0, rows, :] = u[:r]
        ts_ref[1, rows, :] = u[r:]
        return carry

    lax.fori_loop(0, a_n, stage2, 0, unroll=8)

    def stage3(bp, carry):
        cols = []
        for e in range(2):
            rows = pl.ds(2 * bp + e, a_n, stride=FFT_PITCH)
            cols.append(jnp.concatenate([ts_ref[0, rows, :], ts_ref[1, rows, :]], axis=0))
        rhs = jnp.concatenate(cols, axis=1).astype(BF16)
        y = jnp.dot(f1i_ref[...], rhs, preferred_element_type=F32)
        for e in range(2):
            for c in range(2):
                ys_ref[c, pl.ds(2 * bp + e, a_half, stride=FFT_PITCH), :] = (
                    y[c * a_half:(c + 1) * a_half, e * LANES:(e + 1) * LANES])
        return carry

    lax.fori_loop(0, r // 2, stage3, 0, unroll=4)

    def finish(a, carry):
        prow = pl.ds(pl.multiple_of(a * FFT_PITCH, 8), r)
        for c in range(2):
            orow = pl.ds(pl.multiple_of(c * seq + a * r, r), r)
            z = ys_ref[c, prow, :] + zs_ref[c, prow, :] * bias_ref[...]
            y = z * x_ref[orow, :].astype(F32)
            o_ref[orow, :] = _rms(y, gain_ref[...]).astype(BF16)
        return carry

    lax.fori_loop(0, a_half, finish, 0, unroll=4)


def _hyena_conv(z_t, x0c_t, h_re, h_im, hy_bias, hy_out_norm, batch, seq, groups):
    n = 2 * seq
    f1_fwd, _, f1_inv, g_fwd, g_inv = _dft_constants(seq)
    a_n = n // FFT_R
    zspec = pl.BlockSpec((None, 2 * seq, LANES), lambda g, p: (g, p, 0))
    hspec = pl.BlockSpec((n, LANES), lambda g, p: (0, g), pipeline_mode=pl.Buffered(1))
    cspec = lambda arr: pl.BlockSpec(arr.shape, lambda g, p: (0,) * arr.ndim,
                                     pipeline_mode=pl.Buffered(1))
    vspec = pl.BlockSpec((None, 1, LANES), lambda g, p: (g, 0, 0))
    out = jax.ShapeDtypeStruct((groups, batch * seq, LANES), BF16)
    return pl.pallas_call(
        functools.partial(_hy_conv_kernel, seq=seq),
        out_shape=out,
        grid=(groups, batch // 2),
        in_specs=[zspec, zspec, hspec, hspec,
                  cspec(f1_fwd), cspec(f1_inv), cspec(g_fwd), cspec(g_inv), vspec, vspec],
        out_specs=zspec,
        scratch_shapes=[
            pltpu.VMEM((2, a_n // 2 * FFT_PITCH, LANES), F32),
            pltpu.VMEM((2, a_n * FFT_PITCH, LANES), F32),
            pltpu.VMEM((2, a_n // 2 * FFT_PITCH, LANES), F32),
        ],
        compiler_params=_cparams(("parallel", "arbitrary")),
        name="hyena_conv",
    )(z_t, x0c_t, h_re, h_im,
      jnp.asarray(f1_fwd, BF16), jnp.asarray(f1_inv, BF16),
      jnp.asarray(g_fwd, BF16), jnp.asarray(g_inv, BF16),
      hy_bias.astype(F32).reshape(groups, 1, LANES),
      hy_out_norm.astype(F32).reshape(groups, 1, LANES))


def _out_proj_kernel(ya_ref, yh_ref, x_ref, w_ref, o_ref, *, heads, groups):
    lhs = jnp.concatenate([ya_ref[g] for g in range(heads)]
                          + [yh_ref[g] for g in range(groups)], axis=-1)
    o_ref[...] = x_ref[...] + jnp.dot(lhs, w_ref[...], preferred_element_type=F32)


def _out_proj(ya_t, yh_t, x2, w_out, tm):
    m, d = x2.shape
    heads, groups = ya_t.shape[0], yh_t.shape[0]
    return pl.pallas_call(
        functools.partial(_out_proj_kernel, heads=heads, groups=groups),
        out_shape=jax.ShapeDtypeStruct((m, d), F32),
        grid=(m // tm,),
        in_specs=[
            pl.BlockSpec((heads, tm, LANES), lambda i: (0, i, 0)),
            pl.BlockSpec((groups, tm, LANES), lambda i: (0, i, 0)),
            pl.BlockSpec((tm, d), lambda i: (i, 0)),
            pl.BlockSpec(w_out.shape, lambda i: (0, 0)),
        ],
        out_specs=pl.BlockSpec((tm, d), lambda i: (i, 0)),
        compiler_params=_cparams(("parallel",)),
        name="out_proj",
    )(ya_t, yh_t, x2, w_out.astype(BF16))


def _ffn_kernel(h_ref, g_ref, wa_ref, wg_ref, wd_ref, o_ref, xn_ref, acc_ref):
    j = pl.program_id(1)

    @pl.when(j == 0)
    def _():
        xn_ref[...] = _rms(h_ref[...], g_ref[...]).astype(BF16)
        acc_ref[...] = jnp.zeros_like(acc_ref)

    xn = xn_ref[...]
    a = jnp.dot(xn, wa_ref[...], preferred_element_type=F32)
    g = jnp.dot(xn, wg_ref[...], preferred_element_type=F32)
    act = (a * jax.nn.sigmoid(a) * g).astype(BF16)
    acc_ref[...] += jnp.dot(act, wd_ref[...], preferred_element_type=F32)

    @pl.when(j == pl.num_programs(1) - 1)
    def _():
        o_ref[...] = h_ref[...] + acc_ref[...]


def _ffn(h1, norm2, w_gu, w_down, tm, th):
    m, d = h1.shape
    hidden = w_down.shape[0]
    nh = hidden // th
    w_gu = w_gu.astype(BF16)
    return pl.pallas_call(
        _ffn_kernel,
        out_shape=jax.ShapeDtypeStruct((m, d), F32),
        grid=(m // tm, nh),
        in_specs=[
            pl.BlockSpec((tm, d), lambda i, j: (i, 0)),
            pl.BlockSpec((1, d), lambda i, j: (0, 0)),
            pl.BlockSpec((d, th), lambda i, j: (0, j)),
            pl.BlockSpec((d, th), lambda i, j: (0, nh + j)),
            pl.BlockSpec((th, d), lambda i, j: (j, 0)),
        ],
        out_specs=pl.BlockSpec((tm, d), lambda i, j: (i, 0)),
        scratch_shapes=[pltpu.VMEM((tm, d), BF16), pltpu.VMEM((tm, d), F32)],
        compiler_params=_cparams(("parallel", "arbitrary")),
        name="ffn",
    )(h1, norm2.reshape(1, d), w_gu, w_gu, w_down.astype(BF16))


def _ple_kernel(h_ref, p_ref, gn_ref, wg_ref, wp_ref, pn_ref, o_ref):
    h = h_ref[...]
    gate = jax.nn.sigmoid(jnp.dot(_rms(h, gn_ref[...]).astype(BF16), wg_ref[...],
                                  preferred_element_type=F32))
    e = _rms(jnp.dot(p_ref[...].astype(BF16), wp_ref[...], preferred_element_type=F32),
             pn_ref[...])
    o_ref[...] = h + gate * e


def _ple(h2, p2, ple_norm, w_gate, w_proj, ple_post_norm, tm):
    m, d = h2.shape
    pd = p2.shape[1]
    return pl.pallas_call(
        _ple_kernel,
        out_shape=jax.ShapeDtypeStruct((m, d), F32),
        grid=(m // tm,),
        in_specs=[
            pl.BlockSpec((tm, d), lambda i: (i, 0)),
            pl.BlockSpec((tm, pd), lambda i: (i, 0)),
            pl.BlockSpec((1, d), lambda i: (0, 0)),
            pl.BlockSpec((d, d), lambda i: (0, 0)),
            pl.BlockSpec((pd, d), lambda i: (0, 0)),
            pl.BlockSpec((1, d), lambda i: (0, 0)),
        ],
        out_specs=pl.BlockSpec((tm, d), lambda i: (i, 0)),
        compiler_params=_cparams(("parallel",)),
        name="ple",
    )(h2, p2, ple_norm.reshape(1, d), w_gate.astype(BF16), w_proj.astype(BF16),
      ple_post_norm.reshape(1, d))


def kernel(x, p, rel_bias, norm1, w_in, q_norm, k_norm, conv_w, conv_b, hy_w1, hy_b1, hy_wi, hy_bi, hy_wo, hy_freq, hy_decay, hy_bias, attn_out_norm, hy_out_norm, w_out, norm2, w_gu, w_down, ple_norm, w_ple_gate, w_ple_proj, ple_post_norm):
    batch, seq, d = x.shape
    attn_w = d // 2
    heads = attn_w // HEAD_DIM
    groups = heads
    hy_w = groups * LANES
    m = batch * seq
    tm = 512
    h = x.reshape(m, d)
    bias_tab = _attn_bias_tables(rel_bias)
    for i in range(norm1.shape[0]):
        u_t = _in_proj(h, norm1[i], w_in[i], q_norm[i], k_norm[i], attn_w, tm)
        ya_t = _attention(u_t, bias_tab, attn_out_norm[i], batch, seq, heads)
        filt = _hyena_filter(seq, hy_w, hy_w1[i], hy_b1[i], hy_wi[i], hy_bi[i], hy_wo[i],
                             hy_freq[i], hy_decay[i])
        h_re, h_im = _filter_spectrum(filt, seq)
        z_t, x0c_t = _hyena_front(u_t, conv_w[i], conv_b[i], batch, seq, heads, groups)
        yh_t = _hyena_conv(z_t, x0c_t, h_re, h_im, hy_bias[i], hy_out_norm[i],
                           batch, seq, groups)
        h = _out_proj(ya_t, yh_t, h, w_out[i], tm)
        h = _ffn(h, norm2[i], w_gu[i], w_down[i], tm, 512)
        h = _ple(h, p[i].reshape(m, PLE_DIM), ple_norm[i], w_ple_gate[i], w_ple_proj[i],
                 ple_post_norm[i], tm)
    return h.reshape(batch, seq, d)
```

```python
import functools
import math

import jax
import jax.numpy as jnp
import numpy as np
from jax import lax
from jax.experimental import pallas as pl
from jax.experimental.pallas import tpu as pltpu

F32 = jnp.float32
BF16 = jnp.bfloat16

LANES = 128
HEAD_DIM = 128
ATTN_PATTERNS = ((128, 1), (512, 4), (2048, 16))
BAND = 64
QBLK = 128
KBLK = QBLK + 2 * BAND
N_BUCKETS = 32
REL_MAX_DIST = 1024
HY_EMB = 33
HY_FILTER_WIDTH = 64
HY_INNER = 2
EPS = 1e-6
NEG = -1e30
PLE_DIM = 256

FFT_R = 64
FFT_PITCH = FFT_R + 8
VMEM_LIMIT = 56 * 1024 * 1024


def _cparams(sem, vmem=VMEM_LIMIT):
    return pltpu.CompilerParams(dimension_semantics=sem, vmem_limit_bytes=vmem)


def _rms(x, gain):
    ms = jnp.mean(x * x, axis=-1, keepdims=True)
    return x * lax.rsqrt(ms + EPS) * gain


def _in_proj_kernel(x_ref, g_ref, w_ref, qg_ref, kg_ref, o_ref, xn_ref, *, nq, hpt):
    j = pl.program_id(1)

    @pl.when(j == 0)
    def _():
        xn_ref[...] = _rms(x_ref[...], g_ref[...]).astype(BF16)

    acc = jnp.dot(xn_ref[...], w_ref[...], preferred_element_type=F32)

    def write(gain):
        for hh in range(hpt):
            a = acc[:, hh * LANES:(hh + 1) * LANES]
            if gain is not None:
                a = _rms(a, gain)
            o_ref[hh] = a.astype(BF16)

    @pl.when(j < nq)
    def _():
        write(qg_ref[...] * (HEAD_DIM ** -0.5))

    @pl.when((j >= nq) & (j < 2 * nq))
    def _():
        write(kg_ref[...])

    @pl.when(j >= 2 * nq)
    def _():
        write(None)


def _in_proj(x2, norm1, w_in, q_norm, k_norm, attn_w, tm):
    m, d = x2.shape
    in_w = w_in.shape[1]
    tn = min(1024, attn_w)
    hpt = tn // LANES
    nq = attn_w // tn
    return pl.pallas_call(
        functools.partial(_in_proj_kernel, nq=nq, hpt=hpt),
        out_shape=jax.ShapeDtypeStruct((in_w // LANES, m, LANES), BF16),
        grid=(m // tm, in_w // tn),
        in_specs=[
            pl.BlockSpec((tm, d), lambda i, j: (i, 0)),
            pl.BlockSpec((1, d), lambda i, j: (0, 0)),
            pl.BlockSpec((d, tn), lambda i, j: (0, j)),
            pl.BlockSpec((1, LANES), lambda i, j: (0, 0)),
            pl.BlockSpec((1, LANES), lambda i, j: (0, 0)),
        ],
        out_specs=pl.BlockSpec((hpt, tm, LANES), lambda i, j: (j, i, 0)),
        scratch_shapes=[pltpu.VMEM((tm, d), BF16)],
        compiler_params=_cparams(("parallel", "arbitrary")),
        name="in_proj",
    )(x2, norm1.reshape(1, d), w_in.astype(BF16), q_norm.reshape(1, LANES),
      k_norm.reshape(1, LANES))


def _t5_bucket(rel):
    half = N_BUCKETS // 2
    exact = half // 2
    n = np.abs(rel)
    large = exact + (np.log(np.maximum(n, 1).astype(np.float32) / np.float32(exact))
                     / np.float32(math.log(REL_MAX_DIST / exact))
                     * np.float32(half - exact)).astype(np.int32)
    large = np.minimum(large, half - 1)
    return np.where(rel > 0, half, 0) + np.where(n < exact, n, large)


def _attn_bias_tables(rel_bias):
    qi = np.arange(QBLK)[:, None]
    kc = np.arange(KBLK)[None, :]
    rel = kc - BAND - qi
    band = np.abs(rel) <= BAND
    first_ok = np.broadcast_to(kc >= BAND, rel.shape)
    last_ok = np.broadcast_to(kc < QBLK + BAND, rel.shape)
    tabs = []
    for _, dil in ATTN_PATTERNS:
        onehot = np.eye(N_BUCKETS, dtype=np.float32)[_t5_bucket(rel * dil)]
        bias = jnp.einsum("qkn,nh->hqk", jnp.asarray(onehot), rel_bias.astype(F32),
                          precision=lax.Precision.HIGHEST)
        variants = [jnp.where(jnp.asarray(band & ok), bias, NEG)
                    for ok in (first_ok, np.ones_like(band), last_ok)]
        tabs.append(jnp.stack(variants, axis=1))
    return jnp.stack(tabs, axis=1)


def _attn_kernel(q_ref, k_ref, v_ref, bias_ref, gain_ref, o_ref,
                 xf_ref, qp_ref, kp_ref, vp_ref, acc_ref, m_ref, l_ref, *, seq):
    zeros_pad = jnp.zeros((BAND, LANES), BF16)

    for pidx, (_, dil) in enumerate(ATTN_PATTERNS):
        ls = seq // dil
        nb = ls // QBLK
        seg = ls + 2 * BAND

        if dil == 1:
            kp_ref[pl.ds(BAND, seq), :] = k_ref[...]
            vp_ref[pl.ds(BAND, seq), :] = v_ref[...]
        else:
            if pidx == 1:
                xf_ref[0] = q_ref[...].astype(F32)
                xf_ref[1] = k_ref[...].astype(F32)
                xf_ref[2] = v_ref[...].astype(F32)
            for r in range(dil):
                qp_ref[pl.ds(r * ls, ls), :] = (
                    xf_ref[0, pl.ds(r, ls, stride=dil), :].astype(BF16))
                kp_ref[pl.ds(r * seg + BAND, ls), :] = (
                    xf_ref[1, pl.ds(r, ls, stride=dil), :].astype(BF16))
                vp_ref[pl.ds(r * seg + BAND, ls), :] = (
                    xf_ref[2, pl.ds(r, ls, stride=dil), :].astype(BF16))
        for r in range(dil):
            for ref in (kp_ref, vp_ref):
                ref[pl.ds(r * seg, BAND), :] = zeros_pad
                ref[pl.ds(r * seg + BAND + ls, BAND), :] = zeros_pad

        def block(t, carry, dil=dil, ls=ls, nb=nb, seg=seg, pidx=pidx):
            r = t // nb
            qb = t % nb
            qoff = pl.multiple_of(r * ls + qb * QBLK, QBLK)
            koff = pl.multiple_of(r * seg + qb * QBLK, QBLK)
            if dil == 1:
                q = q_ref[pl.ds(qoff, QBLK), :]
            else:
                q = qp_ref[pl.ds(qoff, QBLK), :]
            k = kp_ref[pl.ds(koff, KBLK), :]
            v = vp_ref[pl.ds(koff, KBLK), :]
            variant = jnp.where(qb == 0, 0, jnp.where(qb == nb - 1, 2, 1))
            s = lax.dot_general(q, k, (((1,), (1,)), ((), ())),
                                preferred_element_type=F32)
            s = s + bias_ref[pidx, variant]
            m_blk = jnp.max(s, axis=-1, keepdims=True)
            p = jnp.exp(s - m_blk)
            l_blk = jnp.sum(p, axis=-1, keepdims=True)
            a_blk = jnp.dot(p.astype(BF16), v, preferred_element_type=F32)
            m_b = jnp.broadcast_to(m_blk, (QBLK, LANES))
            l_b = jnp.broadcast_to(l_blk, (QBLK, LANES))
            if dil == 1:
                rows = pl.ds(qoff, QBLK)
                acc_ref[rows, :] = a_blk
                m_ref[rows, :] = m_b
                l_ref[rows, :] = l_b
            else:
                rows = pl.ds(r + qb * (QBLK * dil), QBLK, stride=dil)
                m_old = m_ref[rows, :]
                m_new = jnp.maximum(m_old, m_b)
                w_old = jnp.exp(m_old - m_new)
                w_blk = jnp.exp(m_b - m_new)
                acc_ref[rows, :] = acc_ref[rows, :] * w_old + a_blk * w_blk
                l_ref[rows, :] = l_ref[rows, :] * w_old + l_b * w_blk
                m_ref[rows, :] = m_new
            return carry

        lax.fori_loop(0, dil * nb, block, 0, unroll=8)

    def finish(c, carry):
        rows = pl.ds(pl.multiple_of(c * 256, 256), 256)
        y = acc_ref[rows, :] / l_ref[rows, :]
        o_ref[rows, :] = _rms(y, gain_ref[...]).astype(BF16)
        return carry

    lax.fori_loop(0, seq // 256, finish, 0)


def _attention(u_t, bias_tab, attn_out_norm, batch, seq, heads):
    max_dil = ATTN_PATTERNS[-1][1]
    pad_rows = seq + 2 * BAND * max_dil
    return pl.pallas_call(
        functools.partial(_attn_kernel, seq=seq),
        out_shape=jax.ShapeDtypeStruct((heads, batch * seq, LANES), BF16),
        grid=(batch, heads),
        in_specs=[
            pl.BlockSpec((None, seq, LANES), lambda b, h: (h, b, 0)),
            pl.BlockSpec((None, seq, LANES), lambda b, h: (heads + h, b, 0)),
            pl.BlockSpec((None, seq, LANES), lambda b, h: (2 * heads + h, b, 0)),
            pl.BlockSpec((None, len(ATTN_PATTERNS), 3, QBLK, KBLK),
                         lambda b, h: (h, 0, 0, 0, 0)),
            pl.BlockSpec((None, 1, LANES), lambda b, h: (h, 0, 0)),
        ],
        out_specs=pl.BlockSpec((None, seq, LANES), lambda b, h: (h, b, 0)),
        scratch_shapes=[
            pltpu.VMEM((3, seq, LANES), F32),
            pltpu.VMEM((seq, LANES), BF16),
            pltpu.VMEM((pad_rows, LANES), BF16),
            pltpu.VMEM((pad_rows, LANES), BF16),
            pltpu.VMEM((seq, LANES), F32),
            pltpu.VMEM((seq, LANES), F32),
            pltpu.VMEM((seq, LANES), F32),
        ],
        compiler_params=_cparams(("parallel", "parallel")),
        name="attention",
    )(u_t, u_t, u_t, bias_tab, attn_out_norm.reshape(heads, 1, LANES))


def _filter_features(seq):
    pos = np.arange(seq, dtype=np.float32)
    t = pos / np.float32(max(seq - 1, 1))
    bands = (HY_EMB - 1) // 2
    fr = np.linspace(1e-4, bands - 1, bands, dtype=np.float32)
    ang = np.float32(2.0 * math.pi / seq) * pos[:, None] * fr[None, :]
    z = np.concatenate([t[:, None], np.cos(ang), -np.sin(ang)], axis=-1).astype(np.float32)
    zp = np.zeros((seq, LANES), np.float32)
    zp[:, :HY_EMB] = z
    offs = (np.abs(pos - (seq // 2)) / np.float32(seq / 2)).astype(np.float32)
    return zp, offs[:, None]


def _filter_kernel(z_ref, offs_ref, w1_ref, b1_ref, wi_ref, bi_ref, wo_ref, fq_ref,
                   decay_ref, o_ref):
    hp = lax.Precision.HIGHEST
    fq = fq_ref[...]
    h = jnp.sin(fq * (jnp.dot(z_ref[...], w1_ref[...], precision=hp,
                              preferred_element_type=F32) + b1_ref[...]))
    for j in range(HY_INNER):
        h = jnp.sin(fq * (jnp.dot(h, wi_ref[j], precision=hp,
                                  preferred_element_type=F32) + bi_ref[j]))
    filt = jnp.dot(h, wo_ref[...], precision=hp, preferred_element_type=F32)
    o_ref[...] = filt * jnp.exp(-offs_ref[...] * jnp.abs(decay_ref[...]))


def _hyena_filter(seq, hy_w, w1, b1, wi, bi, wo, freq, decay):
    zp, offs = _filter_features(seq)
    w1p = jnp.zeros((LANES, HY_FILTER_WIDTH), F32).at[:HY_EMB].set(w1.astype(F32))
    tr = 512
    fw = HY_FILTER_WIDTH
    const = lambda *shape: pl.BlockSpec(shape, lambda i: (0,) * len(shape))
    return pl.pallas_call(
        _filter_kernel,
        out_shape=jax.ShapeDtypeStruct((seq, hy_w), F32),
        grid=(seq // tr,),
        in_specs=[
            pl.BlockSpec((tr, LANES), lambda i: (i, 0)),
            pl.BlockSpec((tr, 1), lambda i: (i, 0)),
            const(LANES, fw), const(1, fw), const(HY_INNER, fw, fw),
            const(HY_INNER, 1, fw), const(fw, hy_w), const(1, fw), const(1, hy_w),
        ],
        out_specs=pl.BlockSpec((tr, hy_w), lambda i: (i, 0)),
        compiler_params=_cparams(("parallel",)),
        name="hyena_filter",
    )(jnp.asarray(zp), jnp.asarray(offs), w1p, b1.reshape(1, fw).astype(F32),
      wi.astype(F32), bi.reshape(HY_INNER, 1, fw).astype(F32), wo.astype(F32),
      freq.reshape(1, fw).astype(F32), decay.reshape(1, hy_w).astype(F32))


def _stack(c):
    return np.block([[c.real, -c.imag], [c.imag, c.real]])


@functools.lru_cache(maxsize=None)
def _dft_constants(seq):
    n = 2 * seq
    r = FFT_R
    a_n = n // r
    ar = np.arange(a_n)
    br = np.arange(r)
    f1 = np.exp(-2j * np.pi * np.outer(ar, ar) / a_n)
    f1_fwd = _stack(f1[:, :a_n // 2])
    f1_fwd_real = np.concatenate([f1[:, :a_n // 2].real, f1[:, :a_n // 2].imag], 0)
    f1_inv = _stack(np.conj(f1).T[a_n // 4:3 * a_n // 4] / n)
    ph = (br[None, None, :] * br[None, :, None] / r
          + br[None, None, :] * ar[:, None, None] / n)
    g = np.exp(-2j * np.pi * ph)
    g_fwd = np.stack([_stack(g[k]) for k in range(a_n)])
    g_inv = np.stack([_stack(np.conj(g[k]).T) for k in range(a_n)])
    return (f1_fwd.astype(np.float32), f1_fwd_real.astype(np.float32),
            f1_inv.astype(np.float32), g_fwd.astype(np.float32), g_inv.astype(np.float32))


def _split_bf16(x):
    if isinstance(x, np.ndarray):
        hi = x.astype(BF16)
        lo = (x - hi.astype(np.float32)).astype(BF16)
        return jnp.asarray(hi), jnp.asarray(lo)
    hi = x.astype(BF16)
    return hi, (x - hi.astype(F32)).astype(BF16)


def _dot3(m_hi, m_lo, x):
    x_hi, x_lo = _split_bf16(x)
    dot = functools.partial(jnp.dot, preferred_element_type=F32)
    return dot(m_hi, x_hi) + (dot(m_hi, x_lo) + dot(m_lo, x_hi))


def _spectrum_kernel(filt_ref, f1h_ref, f1l_ref, gh_ref, gl_ref, hr_ref, hi_ref,
                     zs_ref, ts_ref, *, seq):
    r = FFT_R
    a_half = seq // r
    a_n = 2 * a_half
    for a in range(a_half):
        zs_ref[pl.ds(a * FFT_PITCH, r), :] = filt_ref[pl.ds(a * r, r), :]

    def stage1(bp, carry):
        rhs = jnp.concatenate(
            [zs_ref[pl.ds(2 * bp + e, a_half, stride=FFT_PITCH), :] for e in range(2)], axis=1)
        t = _dot3(f1h_ref[...], f1l_ref[...], rhs)
        for e in range(2):
            for c in range(2):
                ts_ref[c, pl.ds(2 * bp + e, a_n, stride=FFT_PITCH), :] = (
                    t[c * a_n:(c + 1) * a_n, e * LANES:(e + 1) * LANES])
        return carry

    lax.fori_loop(0, r // 2, stage1, 0, unroll=4)

    def stage2(ka, carry):
        rows = pl.ds(pl.multiple_of(ka * FFT_PITCH, 8), r)
        rhs = jnp.concatenate([ts_ref[0, rows, :], ts_ref[1, rows, :]], axis=0)
        s = _dot3(gh_ref[ka], gl_ref[ka], rhs)
        out = pl.ds(pl.multiple_of(ka * r, r), r)
        hr_ref[out, :] = s[:r]
        hi_ref[out, :] = s[r:]
        return carry

    lax.fori_loop(0, a_n, stage2, 0, unroll=8)


def _filter_spectrum(filt, seq):
    hy_w = filt.shape[1]
    n = 2 * seq
    _, f1_real, _, g_fwd, _ = _dft_constants(seq)
    f1h, f1l = _split_bf16(f1_real)
    gh, gl = _split_bf16(g_fwd)
    a_n = n // FFT_R
    out = jax.ShapeDtypeStruct((n, hy_w), F32)
    gspec = pl.BlockSpec(g_fwd.shape, lambda c: (0, 0, 0), pipeline_mode=pl.Buffered(1))
    return pl.pallas_call(
        functools.partial(_spectrum_kernel, seq=seq),
        out_shape=(out, out),
        grid=(hy_w // LANES,),
        in_specs=[
            pl.BlockSpec((seq, LANES), lambda c: (0, c)),
            pl.BlockSpec(f1_real.shape, lambda c: (0, 0)),
            pl.BlockSpec(f1_real.shape, lambda c: (0, 0)),
            gspec, gspec,
        ],
        out_specs=(pl.BlockSpec((n, LANES), lambda c: (0, c)),
                   pl.BlockSpec((n, LANES), lambda c: (0, c))),
        scratch_shapes=[
            pltpu.VMEM((a_n // 2 * FFT_PITCH, LANES), F32),
            pltpu.VMEM((2, a_n * FFT_PITCH, LANES), F32),
        ],
        compiler_params=_cparams(("parallel",)),
        name="filter_spectrum",
    )(filt, f1h, f1l, gh, gl)


def _hy_front_kernel(x0_ref, x1_ref, hv_ref, w0_ref, w1_ref, wv_ref, b0_ref, b1_ref, bv_ref,
                     z_ref, x0c_ref, pad_ref, *, seq):
    chunk = 512
    zero_row = jnp.zeros((8, LANES), F32)
    for idx, ref in enumerate((x0_ref, x1_ref, hv_ref)):
        pad_ref[idx, pl.ds(0, 8), :] = zero_row
        pad_ref[idx, pl.ds(8 + seq, 8), :] = zero_row
        pad_ref[idx, pl.ds(8, seq), :] = ref[...].astype(F32)

    def conv(idx, w_ref, b_ref, c):
        base = 8 + c * chunk
        w = w_ref[...]
        return (pad_ref[idx, pl.ds(base - 1, chunk), :] * w[0:1]
                + pad_ref[idx, pl.ds(base, chunk), :] * w[1:2]
                + pad_ref[idx, pl.ds(base + 1, chunk), :] * w[2:3]
                + b_ref[...])

    for c in range(seq // chunk):
        rows = pl.ds(c * chunk, chunk)
        x0c_ref[rows, :] = conv(0, w0_ref, b0_ref, c).astype(BF16)
        z_ref[rows, :] = (conv(2, wv_ref, bv_ref, c) * conv(1, w1_ref, b1_ref, c)).astype(BF16)


def _hyena_front(u_t, conv_w, conv_b, batch, seq, heads, groups):
    ng = 3 * groups
    cw = conv_w.astype(F32).reshape(3, ng, LANES).transpose(1, 0, 2)
    cb = conv_b.astype(F32).reshape(ng, 1, LANES)
    base = 3 * heads
    uspec = lambda off: pl.BlockSpec((None, seq, LANES), lambda b, g: (base + off + g, b, 0))
    wspec = lambda off: pl.BlockSpec((None, 3, LANES), lambda b, g: (off + g, 0, 0))
    bspec = lambda off: pl.BlockSpec((None, 1, LANES), lambda b, g: (off + g, 0, 0))
    out = jax.ShapeDtypeStruct((groups, batch * seq, LANES), BF16)
    ospec = pl.BlockSpec((None, seq, LANES), lambda b, g: (g, b, 0))
    return pl.pallas_call(
        functools.partial(_hy_front_kernel, seq=seq),
        out_shape=(out, out),
        grid=(batch, groups),
        in_specs=[uspec(0), uspec(groups), uspec(2 * groups),
                  wspec(0), wspec(groups), wspec(2 * groups),
                  bspec(0), bspec(groups), bspec(2 * groups)],
        out_specs=(ospec, ospec),
        scratch_shapes=[pltpu.VMEM((3, seq + 16, LANES), F32)],
        compiler_params=_cparams(("parallel", "parallel")),
        name="hyena_front",
    )(u_t, u_t, u_t, cw, cw, cw, cb, cb, cb)


def _hy_conv_kernel(z_ref, x_ref, hr_ref, hi_ref, f1_ref, f1i_ref,
                    g_ref, gi_ref, bias_ref, gain_ref, o_ref,
                    zs_ref, ts_ref, ys_ref, *, seq):
    r = FFT_R
    a_half = seq // r
    a_n = 2 * a_half
    for c in range(2):
        for a in range(a_half):
            zs_ref[c, pl.ds(a * FFT_PITCH, r), :] = (
                z_ref[pl.ds(c * seq + a * r, r), :].astype(F32))

    def stage1(bp, carry):
        cols = []
        for e in range(2):
            rows = pl.ds(2 * bp + e, a_half, stride=FFT_PITCH)
            cols.append(jnp.concatenate([zs_ref[0, rows, :], zs_ref[1, rows, :]], axis=0))
        rhs = jnp.concatenate(cols, axis=1).astype(BF16)
        t = jnp.dot(f1_ref[...], rhs, preferred_element_type=F32)
        for e in range(2):
            for c in range(2):
                ts_ref[c, pl.ds(2 * bp + e, a_n, stride=FFT_PITCH), :] = (
                    t[c * a_n:(c + 1) * a_n, e * LANES:(e + 1) * LANES])
        return carry

    lax.fori_loop(0, r // 2, stage1, 0, unroll=4)

    def stage2(ka, carry):
        rows = pl.ds(pl.multiple_of(ka * FFT_PITCH, 8), r)
        rhs = jnp.concatenate([ts_ref[0, rows, :], ts_ref[1, rows, :]], axis=0).astype(BF16)
        s = jnp.dot(g_ref[ka], rhs, preferred_element_type=F32)
        hrows = pl.ds(pl.multiple_of(ka * r, r), r)
        hr = hr_ref[hrows, :]
        hi = hi_ref[hrows, :]
        sr, si = s[:r], s[r:]
        prod = jnp.concatenate([sr * hr - si * hi, sr * hi + si * hr], axis=0).astype(BF16)
        u = jnp.dot(gi_ref[ka], prod, preferred_element_type=F32)
        ts_ref[0, rows, :] = u[:r]
        ts_ref[1, rows, :] = u[r:]
        return carry

    lax.fori_loop(0, a_n, stage2, 0, unroll=16)

    def stage3(bp, carry):
        cols = []
        for e in range(2):
            rows = pl.ds(2 * bp + e, a_n, stride=FFT_PITCH)
            cols.append(jnp.concatenate([ts_ref[0, rows, :], ts_ref[1, rows, :]], axis=0))
        rhs = jnp.concatenate(cols, axis=1).astype(BF16)
        y = jnp.dot(f1i_ref[...], rhs, preferred_element_type=F32)
        for e in range(2):
            for c in range(2):
                ys_ref[c, pl.ds(2 * bp + e, a_half, stride=FFT_PITCH), :] = (
                    y[c * a_half:(c + 1) * a_half, e * LANES:(e + 1) * LANES])
        return carry

    lax.fori_loop(0, r // 2, stage3, 0, unroll=4)

    def finish(a, carry):
        prow = pl.ds(pl.multiple_of(a * FFT_PITCH, 8), r)
        for c in range(2):
            orow = pl.ds(pl.multiple_of(c * seq + a * r, r), r)
            z = ys_ref[c, prow, :] + zs_ref[c, prow, :] * bias_ref[...]
            y = z * x_ref[orow, :].astype(F32)
            o_ref[orow, :] = _rms(y, gain_ref[...]).astype(BF16)
        return carry

    lax.fori_loop(0, a_half, finish, 0, unroll=4)


def _hyena_conv(z_t, x0c_t, h_re, h_im, hy_bias, hy_out_norm, batch, seq, groups):
    n = 2 * seq
    f1_fwd, _, f1_inv, g_fwd, g_inv = _dft_constants(seq)
    a_n = n // FFT_R
    zspec = pl.BlockSpec((None, 2 * seq, LANES), lambda g, p: (g, p, 0))
    hspec = pl.BlockSpec((n, LANES), lambda g, p: (0, g), pipeline_mode=pl.Buffered(1))
    cspec = lambda arr: pl.BlockSpec(arr.shape, lambda g, p: (0,) * arr.ndim,
                                     pipeline_mode=pl.Buffered(1))
    vspec = pl.BlockSpec((None, 1, LANES), lambda g, p: (g, 0, 0))
    out = jax.ShapeDtypeStruct((groups, batch * seq, LANES), BF16)
    return pl.pallas_call(
        functools.partial(_hy_conv_kernel, seq=seq),
        out_shape=out,
        grid=(groups, batch // 2),
        in_specs=[zspec, zspec, hspec, hspec,
                  cspec(f1_fwd), cspec(f1_inv), cspec(g_fwd), cspec(g_inv), vspec, vspec],
        out_specs=zspec,
        scratch_shapes=[
            pltpu.VMEM((2, a_n // 2 * FFT_PITCH, LANES), F32),
            pltpu.VMEM((2, a_n * FFT_PITCH, LANES), F32),
            pltpu.VMEM((2, a_n // 2 * FFT_PITCH, LANES), F32),
        ],
        compiler_params=_cparams(("parallel", "arbitrary")),
        name="hyena_conv",
    )(z_t, x0c_t, h_re, h_im,
      jnp.asarray(f1_fwd, BF16), jnp.asarray(f1_inv, BF16),
      jnp.asarray(g_fwd, BF16), jnp.asarray(g_inv, BF16),
      hy_bias.astype(F32).reshape(groups, 1, LANES),
      hy_out_norm.astype(F32).reshape(groups, 1, LANES))


def _out_proj_kernel(ya_ref, yh_ref, x_ref, w_ref, o_ref, *, heads, groups):
    lhs = jnp.concatenate([ya_ref[g] for g in range(heads)]
                          + [yh_ref[g] for g in range(groups)], axis=-1)
    o_ref[...] = x_ref[...] + jnp.dot(lhs, w_ref[...], preferred_element_type=F32)


def _out_proj(ya_t, yh_t, x2, w_out, tm):
    m, d = x2.shape
    heads, groups = ya_t.shape[0], yh_t.shape[0]
    return pl.pallas_call(
        functools.partial(_out_proj_kernel, heads=heads, groups=groups),
        out_shape=jax.ShapeDtypeStruct((m, d), F32),
        grid=(m // tm,),
        in_specs=[
            pl.BlockSpec((heads, tm, LANES), lambda i: (0, i, 0)),
            pl.BlockSpec((groups, tm, LANES), lambda i: (0, i, 0)),
            pl.BlockSpec((tm, d), lambda i: (i, 0)),
            pl.BlockSpec(w_out.shape, lambda i: (0, 0)),
        ],
        out_specs=pl.BlockSpec((tm, d), lambda i: (i, 0)),
        compiler_params=_cparams(("parallel",)),
        name="out_proj",
    )(ya_t, yh_t, x2, w_out.astype(BF16))


def _ffn_kernel(h_ref, g_ref, wa_ref, wg_ref, wd_ref, o_ref, xn_ref):
    @pl.when(pl.program_id(1) == 0)
    def _():
        h = h_ref[...]
        xn_ref[...] = _rms(h, g_ref[...]).astype(BF16)
        o_ref[...] = h

    xn = xn_ref[...]
    a = jnp.dot(xn, wa_ref[...], preferred_element_type=F32)
    g = jnp.dot(xn, wg_ref[...], preferred_element_type=F32)
    act = (a * jax.nn.sigmoid(a) * g).astype(BF16)
    o_ref[...] += jnp.dot(act, wd_ref[...], preferred_element_type=F32)


def _ffn(h1, norm2, w_gu, w_down, tm, th):
    m, d = h1.shape
    hidden = w_down.shape[0]
    nh = hidden // th
    w_gu = w_gu.astype(BF16)
    return pl.pallas_call(
        _ffn_kernel,
        out_shape=jax.ShapeDtypeStruct((m, d), F32),
        grid=(m // tm, nh),
        in_specs=[
            pl.BlockSpec((tm, d), lambda i, j: (i, 0)),
            pl.BlockSpec((1, d), lambda i, j: (0, 0)),
            pl.BlockSpec((d, th), lambda i, j: (0, j)),
            pl.BlockSpec((d, th), lambda i, j: (0, nh + j)),
            pl.BlockSpec((th, d), lambda i, j: (j, 0)),
        ],
        out_specs=pl.BlockSpec((tm, d), lambda i, j: (i, 0)),
        scratch_shapes=[pltpu.VMEM((tm, d), BF16)],
        compiler_params=_cparams(("parallel", "arbitrary")),
        name="ffn",
    )(h1, norm2.reshape(1, d), w_gu, w_gu, w_down.astype(BF16))


def _ple_kernel(h_ref, p_ref, gn_ref, wg_ref, wp_ref, pn_ref, o_ref):
    h = h_ref[...]
    gate = jax.nn.sigmoid(jnp.dot(_rms(h, gn_ref[...]).astype(BF16), wg_ref[...],
                                  preferred_element_type=F32))
    e = _rms(jnp.dot(p_ref[...].astype(BF16), wp_ref[...], preferred_element_type=F32),
             pn_ref[...])
    o_ref[...] = h + gate * e


def _ple(h2, p2, ple_norm, w_gate, w_proj, ple_post_norm, tm):
    m, d = h2.shape
    pd = p2.shape[1]
    return pl.pallas_call(
        _ple_kernel,
        out_shape=jax.ShapeDtypeStruct((m, d), F32),
        grid=(m // tm,),
        in_specs=[
            pl.BlockSpec((tm, d), lambda i: (i, 0)),
            pl.BlockSpec((tm, pd), lambda i: (i, 0)),
            pl.BlockSpec((1, d), lambda i: (0, 0)),
            pl.BlockSpec((d, d), lambda i: (0, 0)),
            pl.BlockSpec((pd, d), lambda i: (0, 0)),
            pl.BlockSpec((1, d), lambda i: (0, 0)),
        ],
        out_specs=pl.BlockSpec((tm, d), lambda i: (i, 0)),
        compiler_params=_cparams(("parallel",)),
        name="ple",
    )(h2, p2, ple_norm.reshape(1, d), w_gate.astype(BF16), w_proj.astype(BF16),
      ple_post_norm.reshape(1, d))


def kernel(x, p, rel_bias, norm1, w_in, q_norm, k_norm, conv_w, conv_b, hy_w1, hy_b1, hy_wi, hy_bi, hy_wo, hy_freq, hy_decay, hy_bias, attn_out_norm, hy_out_norm, w_out, norm2, w_gu, w_down, ple_norm, w_ple_gate, w_ple_proj, ple_post_norm):
    batch, seq, d = x.shape
    attn_w = d // 2
    heads = attn_w // HEAD_DIM
    groups = heads
    hy_w = groups * LANES
    m = batch * seq
    tm = 512
    tm_big = min(1024, m)
    h = x.reshape(m, d)
    bias_tab = _attn_bias_tables(rel_bias)
    for i in range(norm1.shape[0]):
        u_t = _in_proj(h, norm1[i], w_in[i], q_norm[i], k_norm[i], attn_w, tm_big)
        ya_t = _attention(u_t, bias_tab, attn_out_norm[i], batch, seq, heads)
        filt = _hyena_filter(seq, hy_w, hy_w1[i], hy_b1[i], hy_wi[i], hy_bi[i], hy_wo[i],
                             hy_freq[i], hy_decay[i])
        h_re, h_im = _filter_spectrum(filt, seq)
        z_t, x0c_t = _hyena_front(u_t, conv_w[i], conv_b[i], batch, seq, heads, groups)
        yh_t = _hyena_conv(z_t, x0c_t, h_re, h_im, hy_bias[i], hy_out_norm[i],
                           batch, seq, groups)
        h = _out_proj(ya_t, yh_t, h, w_out[i], tm)
        h = _ffn(h, norm2[i], w_gu[i], w_down[i], tm_big, 512)
        h = _ple(h, p[i].reshape(m, PLE_DIM), ple_norm[i], w_ple_gate[i], w_ple_proj[i],
                 ple_post_norm[i], tm)
    return h.reshape(batch, seq, d)
```

```python
import functools
import math

import jax
import jax.numpy as jnp
import numpy as np
from jax import lax
from jax.experimental import pallas as pl
from jax.experimental.pallas import tpu as pltpu

F32 = jnp.float32
BF16 = jnp.bfloat16

LANES = 128
HEAD_DIM = 128
ATTN_PATTERNS = ((128, 1), (512, 4), (2048, 16))
PERM = 16
BAND = 64
QBLK = 128
KBLK = QBLK + 2 * BAND
N_BUCKETS = 32
REL_MAX_DIST = 1024
HY_EMB = 33
HY_FILTER_WIDTH = 64
HY_INNER = 2
EPS = 1e-6
NEG = -1e30
PLE_DIM = 256

FFT_R = 64
FFT_PITCH = FFT_R + 8
VMEM_LIMIT = 56 * 1024 * 1024


def _cparams(sem, vmem=VMEM_LIMIT):
    return pltpu.CompilerParams(dimension_semantics=sem, vmem_limit_bytes=vmem)


def _rms(x, gain):
    ms = jnp.mean(x * x, axis=-1, keepdims=True)
    return x * lax.rsqrt(ms + EPS) * gain


def _in_proj_kernel(*refs, nx, nq, hpt):
    x_refs = refs[:nx]
    g_ref, w_ref, qg_ref, kg_ref, o_ref, xn_ref = refs[nx:]
    j = pl.program_id(1)
    rows = x_refs[0].shape[0]

    @pl.when(j == 0)
    def _():
        for e, x_ref in enumerate(x_refs):
            xn_ref[pl.ds(e * rows, rows), :] = _rms(x_ref[...], g_ref[...]).astype(BF16)

    acc = jnp.dot(xn_ref[...], w_ref[...], preferred_element_type=F32)

    def write(gain):
        for hh in range(hpt):
            a = acc[:, hh * LANES:(hh + 1) * LANES]
            if gain is not None:
                a = _rms(a, gain)
            o_ref[hh] = a.astype(BF16)

    @pl.when(j < nq)
    def _():
        write(qg_ref[...] * (HEAD_DIM ** -0.5))

    @pl.when((j >= nq) & (j < 2 * nq))
    def _():
        write(kg_ref[...])

    @pl.when(j >= 2 * nq)
    def _():
        write(None)


def _in_proj(x3, norm1, w, q_norm, k_norm, attn_w, tm, *, dilated_order):
    batch, seq, d = x3.shape
    m = batch * seq
    n_cols = w.shape[1]
    tn = min(1024, attn_w)
    hpt = tn // LANES
    if dilated_order:
        nq = attn_w // tn
        run = seq // PERM
        nx = tm // run
        tiles_per_seq = PERM // nx
        xs = [x3.reshape(batch, run, PERM * d)] * nx
        x_specs = [
            pl.BlockSpec((None, run, d),
                         lambda i, j, e=e: (i // tiles_per_seq, 0, nx * (i % tiles_per_seq) + e))
            for e in range(nx)]
    else:
        nq, nx = 0, 1
        xs = [x3.reshape(m, d)]
        x_specs = [pl.BlockSpec((tm, d), lambda i, j: (i, 0))]
    return pl.pallas_call(
        functools.partial(_in_proj_kernel, nx=nx, nq=nq, hpt=hpt),
        out_shape=jax.ShapeDtypeStruct((n_cols // LANES, m, LANES), BF16),
        grid=(m // tm, n_cols // tn),
        in_specs=x_specs + [
            pl.BlockSpec((1, d), lambda i, j: (0, 0)),
            pl.BlockSpec((d, tn), lambda i, j: (0, j)),
            pl.BlockSpec((1, LANES), lambda i, j: (0, 0)),
            pl.BlockSpec((1, LANES), lambda i, j: (0, 0)),
        ],
        out_specs=pl.BlockSpec((hpt, tm, LANES), lambda i, j: (j, i, 0)),
        scratch_shapes=[pltpu.VMEM((tm, d), BF16)],
        compiler_params=_cparams(("parallel", "arbitrary")),
        name="in_proj_qkv" if dilated_order else "in_proj_hyena",
    )(*xs, norm1.reshape(1, d), w.astype(BF16), q_norm.reshape(1, LANES),
      k_norm.reshape(1, LANES))


def _t5_bucket(rel):
    half = N_BUCKETS // 2
    exact = half // 2
    n = np.abs(rel)
    large = exact + (np.log(np.maximum(n, 1).astype(np.float32) / np.float32(exact))
                     / np.float32(math.log(REL_MAX_DIST / exact))
                     * np.float32(half - exact)).astype(np.int32)
    large = np.minimum(large, half - 1)
    return np.where(rel > 0, half, 0) + np.where(n < exact, n, large)


def _block_orders(dil):
    sub = PERM // dil
    e, i = np.divmod(np.arange(QBLK), QBLK // sub)
    q_off = sub * i + e
    if dil == 1:
        k_off = np.arange(KBLK) - BAND
    else:
        e, i = np.divmod(np.arange(KBLK), KBLK // sub)
        k_off = sub * i + e - BAND
    return q_off, k_off


def _to_sequence_order_matrix():
    n = PERM * PERM
    mat = np.zeros((n, n), np.float32)
    r, i = np.divmod(np.arange(n), PERM)
    mat[PERM * i + r, np.arange(n)] = 1.0
    return mat


def _attn_bias_tables(rel_bias):
    tabs = []
    for _, dil in ATTN_PATTERNS:
        q_off, k_off = _block_orders(dil)
        rel = k_off[None, :] - q_off[:, None]
        band = np.abs(rel) <= BAND
        first_ok = np.broadcast_to(k_off[None, :] >= 0, rel.shape)
        last_ok = np.broadcast_to(k_off[None, :] < QBLK, rel.shape)
        onehot = np.eye(N_BUCKETS, dtype=np.float32)[_t5_bucket(rel * dil)]
        bias = jnp.einsum("qkn,nh->hqk", jnp.asarray(onehot), rel_bias.astype(F32),
                          precision=lax.Precision.HIGHEST)
        variants = [jnp.where(jnp.asarray(band & ok), bias, NEG)
                    for ok in (first_ok, np.ones_like(band), last_ok)]
        tabs.append(jnp.stack(variants, axis=1))
    return jnp.stack(tabs, axis=1)


def _attn_kernel(q_ref, k_ref, v_ref, bias_ref, gain_ref, perm_ref, o_ref,
                 qf_ref, kp_ref, vp_ref, kn_ref, vn_ref, acc_ref, m_ref, *, seq):
    run = seq // PERM
    chunk = PERM * PERM
    ones = jnp.ones((KBLK, LANES), BF16)
    zpad = jnp.zeros((BAND, LANES), BF16)

    for ref in (kp_ref, vp_ref, kn_ref, vn_ref):
        ref[pl.ds(0, BAND), :] = zpad
        ref[pl.ds(BAND + seq, BAND), :] = zpad
    kp_ref[pl.ds(BAND, seq), :] = k_ref[...]
    vp_ref[pl.ds(BAND, seq), :] = v_ref[...]
    qf_ref[...] = q_ref[...].astype(F32)

    def runs(ref, base, n):
        return jnp.concatenate(
            [ref[pl.ds(pl.multiple_of(r * run + base, n), n), :] for r in range(PERM)], axis=0)

    def to_sequence_order(c, carry):
        base = c * PERM
        kv = jnp.concatenate([runs(k_ref, base, PERM), runs(v_ref, base, PERM)], axis=1)
        nat = jnp.dot(perm_ref[...], kv, preferred_element_type=F32).astype(BF16)
        rows = pl.ds(pl.multiple_of(BAND + c * chunk, BAND), chunk)
        kn_ref[rows, :] = nat[:, :LANES]
        vn_ref[rows, :] = nat[:, LANES:]
        return carry

    lax.fori_loop(0, seq // chunk, to_sequence_order, 0, unroll=4)

    def softmax_block(q, k, v, bias):
        s = lax.dot_general(q, k, (((1,), (1,)), ((), ())), preferred_element_type=F32) + bias
        m_blk = jnp.max(s, axis=-1, keepdims=True)
        p = jnp.exp(s - m_blk).astype(BF16)
        a_l = jnp.dot(p, jnp.concatenate([v, ones], axis=1), preferred_element_type=F32)
        return a_l, jnp.broadcast_to(m_blk, (QBLK, LANES))

    def variant(qb, nb):
        return jnp.where(qb == 0, 0, jnp.where(qb == nb - 1, 2, 1))

    def merge(row_slices, a_l, m_b):
        m_old = jnp.concatenate([m_ref[rows, :] for rows in row_slices], axis=0)
        a_old = jnp.concatenate([acc_ref[rows, :] for rows in row_slices], axis=0)
        m_new = jnp.maximum(m_old, m_b)
        w_old = jnp.exp(m_old - m_new)
        w_blk = jnp.exp(m_b - m_new)
        a_new = (a_old * jnp.concatenate([w_old, w_old], axis=1)
                 + a_l * jnp.concatenate([w_blk, w_blk], axis=1))
        n = QBLK // len(row_slices)
        for e, rows in enumerate(row_slices):
            acc_ref[rows, :] = a_new[e * n:(e + 1) * n]
            m_ref[rows, :] = m_new[e * n:(e + 1) * n]

    nb16 = run // QBLK

    def block16(t, carry):
        r, qb = t // nb16, t % nb16
        off = pl.multiple_of(r * run + qb * QBLK, QBLK)
        a_l, m_b = softmax_block(q_ref[pl.ds(off, QBLK), :], kp_ref[pl.ds(off, KBLK), :],
                                 vp_ref[pl.ds(off, KBLK), :], bias_ref[2, variant(qb, nb16)])
        acc_ref[pl.ds(off, QBLK), :] = a_l
        m_ref[pl.ds(off, QBLK), :] = m_b
        return carry

    lax.fori_loop(0, PERM * nb16, block16, 0, unroll=8)

    dil = ATTN_PATTERNS[1][1]
    sub = PERM // dil
    nb4 = (seq // dil) // QBLK
    qn, kn = QBLK // sub, KBLK // sub

    def block4(t, carry):
        r, qb = t // nb4, t % nb4
        q_rows = [pl.ds(pl.multiple_of((dil * e + r) * run + qb * qn, qn), qn)
                  for e in range(sub)]
        k_rows = [pl.ds(pl.multiple_of((dil * e + r) * run + qb * qn + BAND - BAND // sub,
                                       BAND // sub), kn) for e in range(sub)]
        q = jnp.concatenate([q_ref[rows, :] for rows in q_rows], axis=0)
        k = jnp.concatenate([kp_ref[rows, :] for rows in k_rows], axis=0)
        v = jnp.concatenate([vp_ref[rows, :] for rows in k_rows], axis=0)
        a_l, m_b = softmax_block(q, k, v, bias_ref[1, variant(qb, nb4)])
        merge(q_rows, a_l, m_b)
        return carry

    lax.fori_loop(0, dil * nb4, block4, 0, unroll=8)

    nb1 = seq // QBLK
    qn1 = QBLK // PERM

    def block1(qb, carry):
        q_rows = [pl.ds(pl.multiple_of(r * run + qb * qn1, qn1), qn1) for r in range(PERM)]
        q = jnp.concatenate([qf_ref[rows, :] for rows in q_rows], axis=0).astype(BF16)
        k_rows = pl.ds(pl.multiple_of(qb * QBLK, QBLK), KBLK)
        a_l, m_b = softmax_block(q, kn_ref[k_rows, :], vn_ref[k_rows, :],
                                 bias_ref[0, variant(qb, nb1)])
        merge(q_rows, a_l, m_b)
        return carry

    lax.fori_loop(0, nb1, block1, 0, unroll=8)

    def finish(c, carry):
        st = runs(acc_ref, c * PERM, PERM)
        y = _rms(st[:, :LANES] / st[:, LANES:], gain_ref[...]).astype(BF16)
        nat = jnp.dot(perm_ref[...], y, preferred_element_type=F32)
        o_ref[pl.ds(pl.multiple_of(c * chunk, chunk), chunk), :] = nat.astype(BF16)
        return carry

    lax.fori_loop(0, seq // chunk, finish, 0, unroll=4)


def _attention(u_p, bias_tab, attn_out_norm, batch, seq, heads):
    assert ATTN_PATTERNS[0][1] == 1 and ATTN_PATTERNS[2][1] == PERM
    assert seq % (PERM * QBLK) == 0
    perm = jnp.asarray(_to_sequence_order_matrix(), BF16)
    pad_rows = seq + 2 * BAND
    return pl.pallas_call(
        functools.partial(_attn_kernel, seq=seq),
        out_shape=jax.ShapeDtypeStruct((heads, batch * seq, LANES), BF16),
        grid=(batch, heads),
        in_specs=[
            pl.BlockSpec((None, seq, LANES), lambda b, h: (h, b, 0)),
            pl.BlockSpec((None, seq, LANES), lambda b, h: (heads + h, b, 0)),
            pl.BlockSpec((None, seq, LANES), lambda b, h: (2 * heads + h, b, 0)),
            pl.BlockSpec((None, len(ATTN_PATTERNS), 3, QBLK, KBLK),
                         lambda b, h: (h, 0, 0, 0, 0)),
            pl.BlockSpec((None, 1, LANES), lambda b, h: (h, 0, 0)),
            pl.BlockSpec(perm.shape, lambda b, h: (0, 0)),
        ],
        out_specs=pl.BlockSpec((None, seq, LANES), lambda b, h: (h, b, 0)),
        scratch_shapes=[
            pltpu.VMEM((seq, LANES), F32),
            pltpu.VMEM((pad_rows, LANES), BF16),
            pltpu.VMEM((pad_rows, LANES), BF16),
            pltpu.VMEM((pad_rows, LANES), BF16),
            pltpu.VMEM((pad_rows, LANES), BF16),
            pltpu.VMEM((seq, 2 * LANES), F32),
            pltpu.VMEM((seq, LANES), F32),
        ],
        compiler_params=_cparams(("parallel", "parallel")),
        name="attention",
    )(u_p, u_p, u_p, bias_tab, attn_out_norm.reshape(heads, 1, LANES), perm)


def _filter_features(seq):
    pos = np.arange(seq, dtype=np.float32)
    t = pos / np.float32(max(seq - 1, 1))
    bands = (HY_EMB - 1) // 2
    fr = np.linspace(1e-4, bands - 1, bands, dtype=np.float32)
    ang = np.float32(2.0 * math.pi / seq) * pos[:, None] * fr[None, :]
    z = np.concatenate([t[:, None], np.cos(ang), -np.sin(ang)], axis=-1).astype(np.float32)
    zp = np.zeros((seq, LANES), np.float32)
    zp[:, :HY_EMB] = z
    offs = (np.abs(pos - (seq // 2)) / np.float32(seq / 2)).astype(np.float32)
    return zp, offs[:, None]


def _filter_kernel(z_ref, offs_ref, w1_ref, b1_ref, wi_ref, bi_ref, wo_ref, fq_ref,
                   decay_ref, o_ref):
    hp = lax.Precision.HIGHEST
    fq = fq_ref[...]
    h = jnp.sin(fq * (jnp.dot(z_ref[...], w1_ref[...], precision=hp,
                              preferred_element_type=F32) + b1_ref[...]))
    for j in range(HY_INNER):
        h = jnp.sin(fq * (jnp.dot(h, wi_ref[j], precision=hp,
                                  preferred_element_type=F32) + bi_ref[j]))
    filt = jnp.dot(h, wo_ref[...], precision=hp, preferred_element_type=F32)
    o_ref[...] = filt * jnp.exp(-offs_ref[...] * jnp.abs(decay_ref[...]))


def _hyena_filter(seq, hy_w, w1, b1, wi, bi, wo, freq, decay):
    zp, offs = _filter_features(seq)
    w1p = jnp.zeros((LANES, HY_FILTER_WIDTH), F32).at[:HY_EMB].set(w1.astype(F32))
    tr = 512
    fw = HY_FILTER_WIDTH
    const = lambda *shape: pl.BlockSpec(shape, lambda i: (0,) * len(shape))
    return pl.pallas_call(
        _filter_kernel,
        out_shape=jax.ShapeDtypeStruct((seq, hy_w), F32),
        grid=(seq // tr,),
        in_specs=[
            pl.BlockSpec((tr, LANES), lambda i: (i, 0)),
            pl.BlockSpec((tr, 1), lambda i: (i, 0)),
            const(LANES, fw), const(1, fw), const(HY_INNER, fw, fw),
            const(HY_INNER, 1, fw), const(fw, hy_w), const(1, fw), const(1, hy_w),
        ],
        out_specs=pl.BlockSpec((tr, hy_w), lambda i: (i, 0)),
        compiler_params=_cparams(("parallel",)),
        name="hyena_filter",
    )(jnp.asarray(zp), jnp.asarray(offs), w1p, b1.reshape(1, fw).astype(F32),
      wi.astype(F32), bi.reshape(HY_INNER, 1, fw).astype(F32), wo.astype(F32),
      freq.reshape(1, fw).astype(F32), decay.reshape(1, hy_w).astype(F32))


def _stack(c):
    return np.block([[c.real, -c.imag], [c.imag, c.real]])


@functools.lru_cache(maxsize=None)
def _dft_constants(seq):
    n = 2 * seq
    r = FFT_R
    a_n = n // r
    ar = np.arange(a_n)
    br = np.arange(r)
    f1 = np.exp(-2j * np.pi * np.outer(ar, ar) / a_n)
    f1_fwd = _stack(f1[:, :a_n // 2])
    f1_fwd_real = np.concatenate([f1[:, :a_n // 2].real, f1[:, :a_n // 2].imag], 0)
    f1_inv = _stack(np.conj(f1).T[a_n // 4:3 * a_n // 4] / n)
    ph = (br[None, None, :] * br[None, :, None] / r
          + br[None, None, :] * ar[:, None, None] / n)
    g = np.exp(-2j * np.pi * ph)
    g_fwd = np.stack([_stack(g[k]) for k in range(a_n)])
    g_inv = np.stack([_stack(np.conj(g[k]).T) for k in range(a_n)])
    return (f1_fwd.astype(np.float32), f1_fwd_real.astype(np.float32),
            f1_inv.astype(np.float32), g_fwd.astype(np.float32), g_inv.astype(np.float32))


def _split_bf16(x):
    if isinstance(x, np.ndarray):
        hi = x.astype(BF16)
        lo = (x - hi.astype(np.float32)).astype(BF16)
        return jnp.asarray(hi), jnp.asarray(lo)
    hi = x.astype(BF16)
    return hi, (x - hi.astype(F32)).astype(BF16)


def _dot3(m_hi, m_lo, x):
    x_hi, x_lo = _split_bf16(x)
    dot = functools.partial(jnp.dot, preferred_element_type=F32)
    return dot(m_hi, x_hi) + (dot(m_hi, x_lo) + dot(m_lo, x_hi))


def _spectrum_kernel(filt_ref, f1h_ref, f1l_ref, gh_ref, gl_ref, hr_ref, hi_ref,
                     zs_ref, ts_ref, *, seq):
    r = FFT_R
    a_half = seq // r
    a_n = 2 * a_half
    for a in range(a_half):
        zs_ref[pl.ds(a * FFT_PITCH, r), :] = filt_ref[pl.ds(a * r, r), :]

    def stage1(bp, carry):
        rhs = jnp.concatenate(
            [zs_ref[pl.ds(2 * bp + e, a_half, stride=FFT_PITCH), :] for e in range(2)], axis=1)
        t = _dot3(f1h_ref[...], f1l_ref[...], rhs)
        for e in range(2):
            for c in range(2):
                ts_ref[c, pl.ds(2 * bp + e, a_n, stride=FFT_PITCH), :] = (
                    t[c * a_n:(c + 1) * a_n, e * LANES:(e + 1) * LANES])
        return carry

    lax.fori_loop(0, r // 2, stage1, 0, unroll=4)

    def stage2(ka, carry):
        rows = pl.ds(pl.multiple_of(ka * FFT_PITCH, 8), r)
        rhs = jnp.concatenate([ts_ref[0, rows, :], ts_ref[1, rows, :]], axis=0)
        s = _dot3(gh_ref[ka], gl_ref[ka], rhs)
        out = pl.ds(pl.multiple_of(ka * r, r), r)
        hr_ref[out, :] = s[:r]
        hi_ref[out, :] = s[r:]
        return carry

    lax.fori_loop(0, a_n, stage2, 0, unroll=8)


def _filter_spectrum(filt, seq):
    hy_w = filt.shape[1]
    n = 2 * seq
    _, f1_real, _, g_fwd, _ = _dft_constants(seq)
    f1h, f1l = _split_bf16(f1_real)
    gh, gl = _split_bf16(g_fwd)
    a_n = n // FFT_R
    out = jax.ShapeDtypeStruct((n, hy_w), F32)
    gspec = pl.BlockSpec(g_fwd.shape, lambda c: (0, 0, 0), pipeline_mode=pl.Buffered(1))
    return pl.pallas_call(
        functools.partial(_spectrum_kernel, seq=seq),
        out_shape=(out, out),
        grid=(hy_w // LANES,),
        in_specs=[
            pl.BlockSpec((seq, LANES), lambda c: (0, c)),
            pl.BlockSpec(f1_real.shape, lambda c: (0, 0)),
            pl.BlockSpec(f1_real.shape, lambda c: (0, 0)),
            gspec, gspec,
        ],
        out_specs=(pl.BlockSpec((n, LANES), lambda c: (0, c)),
                   pl.BlockSpec((n, LANES), lambda c: (0, c))),
        scratch_shapes=[
            pltpu.VMEM((a_n // 2 * FFT_PITCH, LANES), F32),
            pltpu.VMEM((2, a_n * FFT_PITCH, LANES), F32),
        ],
        compiler_params=_cparams(("parallel",)),
        name="filter_spectrum",
    )(filt, f1h, f1l, gh, gl)


def _hy_front_kernel(x0_ref, x1_ref, hv_ref, w0_ref, w1_ref, wv_ref, b0_ref, b1_ref, bv_ref,
                     z_ref, x0c_ref, pad_ref, *, seq):
    chunk = 512
    zero_row = jnp.zeros((8, LANES), F32)
    for idx, ref in enumerate((x0_ref, x1_ref, hv_ref)):
        pad_ref[idx, pl.ds(0, 8), :] = zero_row
        pad_ref[idx, pl.ds(8 + seq, 8), :] = zero_row
        pad_ref[idx, pl.ds(8, seq), :] = ref[...].astype(F32)

    def conv(idx, w_ref, b_ref, c):
        base = 8 + c * chunk
        w = w_ref[...]
        return (pad_ref[idx, pl.ds(base - 1, chunk), :] * w[0:1]
                + pad_ref[idx, pl.ds(base, chunk), :] * w[1:2]
                + pad_ref[idx, pl.ds(base + 1, chunk), :] * w[2:3]
                + b_ref[...])

    for c in range(seq // chunk):
        rows = pl.ds(c * chunk, chunk)
        x0c_ref[rows, :] = conv(0, w0_ref, b0_ref, c).astype(BF16)
        z_ref[rows, :] = (conv(2, wv_ref, bv_ref, c) * conv(1, w1_ref, b1_ref, c)).astype(BF16)


def _hyena_front(u_t, conv_w, conv_b, batch, seq, groups):
    ng = 3 * groups
    cw = conv_w.astype(F32).reshape(3, ng, LANES).transpose(1, 0, 2)
    cb = conv_b.astype(F32).reshape(ng, 1, LANES)
    uspec = lambda off: pl.BlockSpec((None, seq, LANES), lambda b, g: (off + g, b, 0))
    wspec = lambda off: pl.BlockSpec((None, 3, LANES), lambda b, g: (off + g, 0, 0))
    bspec = lambda off: pl.BlockSpec((None, 1, LANES), lambda b, g: (off + g, 0, 0))
    out = jax.ShapeDtypeStruct((groups, batch * seq, LANES), BF16)
    ospec = pl.BlockSpec((None, seq, LANES), lambda b, g: (g, b, 0))
    return pl.pallas_call(
        functools.partial(_hy_front_kernel, seq=seq),
        out_shape=(out, out),
        grid=(batch, groups),
        in_specs=[uspec(0), uspec(groups), uspec(2 * groups),
                  wspec(0), wspec(groups), wspec(2 * groups),
                  bspec(0), bspec(groups), bspec(2 * groups)],
        out_specs=(ospec, ospec),
        scratch_shapes=[pltpu.VMEM((3, seq + 16, LANES), F32)],
        compiler_params=_cparams(("parallel", "parallel")),
        name="hyena_front",
    )(u_t, u_t, u_t, cw, cw, cw, cb, cb, cb)


def _hy_conv_kernel(z_ref, x_ref, hr_ref, hi_ref, f1_ref, f1i_ref,
                    g_ref, gi_ref, bias_ref, gain_ref, o_ref,
                    zs_ref, ts_ref, ys_ref, *, seq):
    r = FFT_R
    a_half = seq // r
    a_n = 2 * a_half
    for c in range(2):
        for a in range(a_half):
            zs_ref[c, pl.ds(a * FFT_PITCH, r), :] = (
                z_ref[pl.ds(c * seq + a * r, r), :].astype(F32))

    def stage1(bp, carry):
        cols = []
        for e in range(2):
            rows = pl.ds(2 * bp + e, a_half, stride=FFT_PITCH)
            cols.append(jnp.concatenate([zs_ref[0, rows, :], zs_ref[1, rows, :]], axis=0))
        rhs = jnp.concatenate(cols, axis=1).astype(BF16)
        t = jnp.dot(f1_ref[...], rhs, preferred_element_type=F32)
        for e in range(2):
            for c in range(2):
                ts_ref[c, pl.ds(2 * bp + e, a_n, stride=FFT_PITCH), :] = (
                    t[c * a_n:(c + 1) * a_n, e * LANES:(e + 1) * LANES])
        return carry

    lax.fori_loop(0, r // 2, stage1, 0, unroll=4)

    def stage2(ka, carry):
        rows = pl.ds(pl.multiple_of(ka * FFT_PITCH, 8), r)
        rhs = jnp.concatenate([ts_ref[0, rows, :], ts_ref[1, rows, :]], axis=0).astype(BF16)
        s = jnp.dot(g_ref[ka], rhs, preferred_element_type=F32)
        hrows = pl.ds(pl.multiple_of(ka * r, r), r)
        hr = hr_ref[hrows, :]
        hi = hi_ref[hrows, :]
        sr, si = s[:r], s[r:]
        prod = jnp.concatenate([sr * hr - si * hi, sr * hi + si * hr], axis=0).astype(BF16)
        u = jnp.dot(gi_ref[ka], prod, preferred_element_type=F32)
        ts_ref[0, rows, :] = u[:r]
        ts_ref[1, rows, :] = u[r:]
        return carry

    lax.fori_loop(0, a_n, stage2, 0, unroll=16)

    def stage3(bp, carry):
        cols = []
        for e in range(2):
            rows = pl.ds(2 * bp + e, a_n, stride=FFT_PITCH)
            cols.append(jnp.concatenate([ts_ref[0, rows, :], ts_ref[1, rows, :]], axis=0))
        rhs = jnp.concatenate(cols, axis=1).astype(BF16)
        y = jnp.dot(f1i_ref[...], rhs, preferred_element_type=F32)
        for e in range(2):
            for c in range(2):
                ys_ref[c, pl.ds(2 * bp + e, a_half, stride=FFT_PITCH), :] = (
                    y[c * a_half:(c + 1) * a_half, e * LANES:(e + 1) * LANES])
        return carry

    lax.fori_loop(0, r // 2, stage3, 0, unroll=4)

    def finish(a, carry):
        prow = pl.ds(pl.multiple_of(a * FFT_PITCH, 8), r)
        for c in range(2):
            orow = pl.ds(pl.multiple_of(c * seq + a * r, r), r)
            z = ys_ref[c, prow, :] + zs_ref[c, prow, :] * bias_ref[...]
            y = z * x_ref[orow, :].astype(F32)
            o_ref[orow, :] = _rms(y, gain_ref[...]).astype(BF16)
        return carry

    lax.fori_loop(0, a_half, finish, 0, unroll=4)


def _hyena_conv(z_t, x0c_t, h_re, h_im, hy_bias, hy_out_norm, batch, seq, groups):
    n = 2 * seq
    f1_fwd, _, f1_inv, g_fwd, g_inv = _dft_constants(seq)
    a_n = n // FFT_R
    zspec = pl.BlockSpec((None, 2 * seq, LANES), lambda g, p: (g, p, 0))
    hspec = pl.BlockSpec((n, LANES), lambda g, p: (0, g), pipeline_mode=pl.Buffered(1))
    cspec = lambda arr: pl.BlockSpec(arr.shape, lambda g, p: (0,) * arr.ndim,
                                     pipeline_mode=pl.Buffered(1))
    vspec = pl.BlockSpec((None, 1, LANES), lambda g, p: (g, 0, 0))
    out = jax.ShapeDtypeStruct((groups, batch * seq, LANES), BF16)
    return pl.pallas_call(
        functools.partial(_hy_conv_kernel, seq=seq),
        out_shape=out,
        grid=(groups, batch // 2),
        in_specs=[zspec, zspec, hspec, hspec,
                  cspec(f1_fwd), cspec(f1_inv), cspec(g_fwd), cspec(g_inv), vspec, vspec],
        out_specs=zspec,
        scratch_shapes=[
            pltpu.VMEM((2, a_n // 2 * FFT_PITCH, LANES), F32),
            pltpu.VMEM((2, a_n * FFT_PITCH, LANES), F32),
            pltpu.VMEM((2, a_n // 2 * FFT_PITCH, LANES), F32),
        ],
        compiler_params=_cparams(("parallel", "arbitrary")),
        name="hyena_conv",
    )(z_t, x0c_t, h_re, h_im,
      jnp.asarray(f1_fwd, BF16), jnp.asarray(f1_inv, BF16),
      jnp.asarray(g_fwd, BF16), jnp.asarray(g_inv, BF16),
      hy_bias.astype(F32).reshape(groups, 1, LANES),
      hy_out_norm.astype(F32).reshape(groups, 1, LANES))


def _out_proj_kernel(ya_ref, yh_ref, x_ref, w_ref, o_ref, *, heads, groups):
    lhs = jnp.concatenate([ya_ref[g] for g in range(heads)]
                          + [yh_ref[g] for g in range(groups)], axis=-1)
    o_ref[...] = x_ref[...] + jnp.dot(lhs, w_ref[...], preferred_element_type=F32)


def _out_proj(ya_t, yh_t, x2, w_out, tm):
    m, d = x2.shape
    heads, groups = ya_t.shape[0], yh_t.shape[0]
    return pl.pallas_call(
        functools.partial(_out_proj_kernel, heads=heads, groups=groups),
        out_shape=jax.ShapeDtypeStruct((m, d), F32),
        grid=(m // tm,),
        in_specs=[
            pl.BlockSpec((heads, tm, LANES), lambda i: (0, i, 0)),
            pl.BlockSpec((groups, tm, LANES), lambda i: (0, i, 0)),
            pl.BlockSpec((tm, d), lambda i: (i, 0)),
            pl.BlockSpec(w_out.shape, lambda i: (0, 0)),
        ],
        out_specs=pl.BlockSpec((tm, d), lambda i: (i, 0)),
        compiler_params=_cparams(("parallel",)),
        name="out_proj",
    )(ya_t, yh_t, x2, w_out.astype(BF16))


def _ffn_kernel(h_ref, g_ref, wa_ref, wg_ref, wd_ref, o_ref, xn_ref):
    @pl.when(pl.program_id(1) == 0)
    def _():
        h = h_ref[...]
        xn_ref[...] = _rms(h, g_ref[...]).astype(BF16)
        o_ref[...] = h

    xn = xn_ref[...]
    a = jnp.dot(xn, wa_ref[...], preferred_element_type=F32)
    g = jnp.dot(xn, wg_ref[...], preferred_element_type=F32)
    act = (a * jax.nn.sigmoid(a) * g).astype(BF16)
    o_ref[...] += jnp.dot(act, wd_ref[...], preferred_element_type=F32)


def _ffn(h1, norm2, w_gu, w_down, tm, th):
    m, d = h1.shape
    hidden = w_down.shape[0]
    nh = hidden // th
    w_gu = w_gu.astype(BF16)
    return pl.pallas_call(
        _ffn_kernel,
        out_shape=jax.ShapeDtypeStruct((m, d), F32),
        grid=(m // tm, nh),
        in_specs=[
            pl.BlockSpec((tm, d), lambda i, j: (i, 0)),
            pl.BlockSpec((1, d), lambda i, j: (0, 0)),
            pl.BlockSpec((d, th), lambda i, j: (0, j)),
            pl.BlockSpec((d, th), lambda i, j: (0, nh + j)),
            pl.BlockSpec((th, d), lambda i, j: (j, 0)),
        ],
        out_specs=pl.BlockSpec((tm, d), lambda i, j: (i, 0)),
        scratch_shapes=[pltpu.VMEM((tm, d), BF16)],
        compiler_params=_cparams(("parallel", "arbitrary")),
        name="ffn",
    )(h1, norm2.reshape(1, d), w_gu, w_gu, w_down.astype(BF16))


def _ple_kernel(h_ref, p_ref, gn_ref, wg_ref, wp_ref, pn_ref, o_ref):
    h = h_ref[...]
    gate = jax.nn.sigmoid(jnp.dot(_rms(h, gn_ref[...]).astype(BF16), wg_ref[...],
                                  preferred_element_type=F32))
    e = _rms(jnp.dot(p_ref[...].astype(BF16), wp_ref[...], preferred_element_type=F32),
             pn_ref[...])
    o_ref[...] = h + gate * e


def _ple(h2, p2, ple_norm, w_gate, w_proj, ple_post_norm, tm):
    m, d = h2.shape
    pd = p2.shape[1]
    return pl.pallas_call(
        _ple_kernel,
        out_shape=jax.ShapeDtypeStruct((m, d), F32),
        grid=(m // tm,),
        in_specs=[
            pl.BlockSpec((tm, d), lambda i: (i, 0)),
            pl.BlockSpec((tm, pd), lambda i: (i, 0)),
            pl.BlockSpec((1, d), lambda i: (0, 0)),
            pl.BlockSpec((d, d), lambda i: (0, 0)),
            pl.BlockSpec((pd, d), lambda i: (0, 0)),
            pl.BlockSpec((1, d), lambda i: (0, 0)),
        ],
        out_specs=pl.BlockSpec((tm, d), lambda i: (i, 0)),
        compiler_params=_cparams(("parallel",)),
        name="ple",
    )(h2, p2, ple_norm.reshape(1, d), w_gate.astype(BF16), w_proj.astype(BF16),
      ple_post_norm.reshape(1, d))


def kernel(x, p, rel_bias, norm1, w_in, q_norm, k_norm, conv_w, conv_b, hy_w1, hy_b1, hy_wi, hy_bi, hy_wo, hy_freq, hy_decay, hy_bias, attn_out_norm, hy_out_norm, w_out, norm2, w_gu, w_down, ple_norm, w_ple_gate, w_ple_proj, ple_post_norm):
    batch, seq, d = x.shape
    attn_w = d // 2
    heads = attn_w // HEAD_DIM
    groups = heads
    hy_w = groups * LANES
    m = batch * seq
    tm = 512
    tm_big = min(1024, m)
    h = x.reshape(m, d)
    bias_tab = _attn_bias_tables(rel_bias)
    for i in range(norm1.shape[0]):
        h3 = h.reshape(batch, seq, d)
        u_p = _in_proj(h3, norm1[i], w_in[i][:, :3 * attn_w], q_norm[i], k_norm[i], attn_w,
                       tm_big, dilated_order=True)
        u_t = _in_proj(h3, norm1[i], w_in[i][:, 3 * attn_w:], q_norm[i], k_norm[i], attn_w,
                       tm_big, dilated_order=False)
        ya_t = _attention(u_p, bias_tab, attn_out_norm[i], batch, seq, heads)
        filt = _hyena_filter(seq, hy_w, hy_w1[i], hy_b1[i], hy_wi[i], hy_bi[i], hy_wo[i],
                             hy_freq[i], hy_decay[i])
        h_re, h_im = _filter_spectrum(filt, seq)
        z_t, x0c_t = _hyena_front(u_t, conv_w[i], conv_b[i], batch, seq, groups)
        yh_t = _hyena_conv(z_t, x0c_t, h_re, h_im, hy_bias[i], hy_out_norm[i],
                           batch, seq, groups)
        h = _out_proj(ya_t, yh_t, h, w_out[i], tm)
        h = _ffn(h, norm2[i], w_gu[i], w_down[i], tm_big, 512)
        h = _ple(h, p[i].reshape(m, PLE_DIM), ple_norm[i], w_ple_gate[i], w_ple_proj[i],
                 ple_post_norm[i], tm)
    return h.reshape(batch, seq, d)
```

```python
import functools
import math

import jax
import jax.numpy as jnp
import numpy as np
from jax import lax
from jax.experimental import pallas as pl
from jax.experimental.pallas import tpu as pltpu

F32 = jnp.float32
BF16 = jnp.bfloat16

LANES = 128
HEAD_DIM = 128
ATTN_PATTERNS = ((128, 1), (512, 4), (2048, 16))
PERM = 16
BAND = 64
QBLK = 128
KBLK = QBLK + 2 * BAND
N_BUCKETS = 32
REL_MAX_DIST = 1024
HY_EMB = 33
HY_FILTER_WIDTH = 64
HY_INNER = 2
EPS = 1e-6
NEG = -1e30
PLE_DIM = 256

FFT_R = 64
FFT_PITCH = FFT_R + 8
VMEM_LIMIT = 56 * 1024 * 1024


def _cparams(sem, vmem=VMEM_LIMIT):
    return pltpu.CompilerParams(dimension_semantics=sem, vmem_limit_bytes=vmem)


def _rms(x, gain):
    ms = jnp.mean(x * x, axis=-1, keepdims=True)
    return x * lax.rsqrt(ms + EPS) * gain


def _in_proj_kernel(x_ref, g_ref, w_ref, qg_ref, kg_ref, perm_ref, oq_ref, oh_ref,
                    xn_ref, xp_ref, *, nq, hpt):
    j = pl.program_id(1)
    tm = x_ref.shape[0]
    chunk = PERM * PERM

    @pl.when(j == 0)
    def _():
        xn_ref[...] = _rms(x_ref[...], g_ref[...]).astype(BF16)
        for c in range(tm // chunk):
            rows = pl.ds(c * chunk, chunk)
            xp_ref[rows, :] = jnp.dot(perm_ref[...], xn_ref[rows, :],
                                      preferred_element_type=F32).astype(BF16)

    @pl.when(j < 3 * nq)
    def _():
        acc = jnp.dot(xp_ref[...], w_ref[...], preferred_element_type=F32)
        is_q = j < nq
        is_k = (j >= nq) & (j < 2 * nq)
        gain = jnp.where(is_q, qg_ref[...] * (HEAD_DIM ** -0.5),
                         jnp.where(is_k, kg_ref[...], 1.0))
        for hh in range(hpt):
            a = acc[:, hh * LANES:(hh + 1) * LANES]
            inv = lax.rsqrt(jnp.mean(a * a, axis=-1, keepdims=True) + EPS)
            a = (a * jnp.where(is_q | is_k, inv, 1.0) * gain).astype(BF16)
            for c in range(tm // chunk):
                for r in range(PERM):
                    oq_ref[hh, r, pl.ds(c * PERM, PERM), :] = (
                        a[c * chunk + r * PERM:c * chunk + (r + 1) * PERM])

    @pl.when(j >= 3 * nq)
    def _():
        acc = jnp.dot(xn_ref[...], w_ref[...], preferred_element_type=F32)
        for hh in range(hpt):
            oh_ref[hh] = acc[:, hh * LANES:(hh + 1) * LANES].astype(BF16)


def _in_proj(x3, norm1, w_in, q_norm, k_norm, attn_w, tm):
    batch, seq, d = x3.shape
    m = batch * seq
    in_w = w_in.shape[1]
    tn = min(1024, attn_w)
    hpt = tn // LANES
    nq = attn_w // tn
    n_qkv = 3 * nq
    n_tiles = in_w // tn
    tiles_per_seq = seq // tm
    assert tm % (PERM * PERM) == 0 and seq % tm == 0
    perm = jnp.asarray(_to_sequence_order_matrix().T, BF16)
    out_q = jax.ShapeDtypeStruct((n_qkv * hpt, batch, PERM, seq // PERM, LANES), BF16)
    out_h = jax.ShapeDtypeStruct(((n_tiles - n_qkv) * hpt, m, LANES), BF16)
    q_spec = pl.BlockSpec(
        (hpt, None, PERM, tm // PERM, LANES),
        lambda i, j: (jnp.minimum(j, n_qkv - 1), i // tiles_per_seq, 0, i % tiles_per_seq, 0))
    h_spec = pl.BlockSpec((hpt, tm, LANES), lambda i, j: (jnp.maximum(j - n_qkv, 0), i, 0))
    u_q, u_h = pl.pallas_call(
        functools.partial(_in_proj_kernel, nq=nq, hpt=hpt),
        out_shape=(out_q, out_h),
        grid=(m // tm, n_tiles),
        in_specs=[
            pl.BlockSpec((tm, d), lambda i, j: (i, 0)),
            pl.BlockSpec((1, d), lambda i, j: (0, 0)),
            pl.BlockSpec((d, tn), lambda i, j: (0, j)),
            pl.BlockSpec((1, LANES), lambda i, j: (0, 0)),
            pl.BlockSpec((1, LANES), lambda i, j: (0, 0)),
            pl.BlockSpec(perm.shape, lambda i, j: (0, 0)),
        ],
        out_specs=(q_spec, h_spec),
        scratch_shapes=[pltpu.VMEM((tm, d), BF16), pltpu.VMEM((tm, d), BF16)],
        compiler_params=_cparams(("parallel", "arbitrary")),
        name="in_proj",
    )(x3.reshape(m, d), norm1.reshape(1, d), w_in.astype(BF16), q_norm.reshape(1, LANES),
      k_norm.reshape(1, LANES), perm)
    return u_q.reshape(n_qkv * hpt, m, LANES), u_h


def _t5_bucket(rel):
    half = N_BUCKETS // 2
    exact = half // 2
    n = np.abs(rel)
    large = exact + (np.log(np.maximum(n, 1).astype(np.float32) / np.float32(exact))
                     / np.float32(math.log(REL_MAX_DIST / exact))
                     * np.float32(half - exact)).astype(np.int32)
    large = np.minimum(large, half - 1)
    return np.where(rel > 0, half, 0) + np.where(n < exact, n, large)


def _block_orders(dil):
    sub = PERM // dil
    e, i = np.divmod(np.arange(QBLK), QBLK // sub)
    q_off = sub * i + e
    if dil == 1:
        k_off = np.arange(KBLK) - BAND
    else:
        e, i = np.divmod(np.arange(KBLK), KBLK // sub)
        k_off = sub * i + e - BAND
    return q_off, k_off


def _to_sequence_order_matrix():
    n = PERM * PERM
    mat = np.zeros((n, n), np.float32)
    r, i = np.divmod(np.arange(n), PERM)
    mat[PERM * i + r, np.arange(n)] = 1.0
    return mat


def _attn_bias_tables(rel_bias):
    tabs = []
    for _, dil in ATTN_PATTERNS:
        q_off, k_off = _block_orders(dil)
        rel = k_off[None, :] - q_off[:, None]
        band = np.abs(rel) <= BAND
        first_ok = np.broadcast_to(k_off[None, :] >= 0, rel.shape)
        last_ok = np.broadcast_to(k_off[None, :] < QBLK, rel.shape)
        onehot = np.eye(N_BUCKETS, dtype=np.float32)[_t5_bucket(rel * dil)]
        bias = jnp.einsum("qkn,nh->hqk", jnp.asarray(onehot), rel_bias.astype(F32),
                          precision=lax.Precision.HIGHEST)
        variants = [jnp.where(jnp.asarray(band & ok), bias, NEG)
                    for ok in (first_ok, np.ones_like(band), last_ok)]
        tabs.append(jnp.stack(variants, axis=1))
    return jnp.stack(tabs, axis=1)


def _attn_kernel(q_ref, k_ref, v_ref, bias_ref, gain_ref, perm_ref, o_ref,
                 qf_ref, kp_ref, vp_ref, kn_ref, vn_ref, acc_ref, m_ref, *, seq):
    run = seq // PERM
    chunk = PERM * PERM
    ones = jnp.ones((KBLK, LANES), BF16)
    zpad = jnp.zeros((BAND, LANES), BF16)

    for ref in (kp_ref, vp_ref, kn_ref, vn_ref):
        ref[pl.ds(0, BAND), :] = zpad
        ref[pl.ds(BAND + seq, BAND), :] = zpad
    kp_ref[pl.ds(BAND, seq), :] = k_ref[...]
    vp_ref[pl.ds(BAND, seq), :] = v_ref[...]
    qf_ref[...] = q_ref[...].astype(F32)

    def runs(ref, base, n):
        return jnp.concatenate(
            [ref[pl.ds(pl.multiple_of(r * run + base, n), n), :] for r in range(PERM)], axis=0)

    def to_sequence_order(c, carry):
        base = c * PERM
        kv = jnp.concatenate([runs(k_ref, base, PERM), runs(v_ref, base, PERM)], axis=1)
        nat = jnp.dot(perm_ref[...], kv, preferred_element_type=F32).astype(BF16)
        rows = pl.ds(pl.multiple_of(BAND + c * chunk, BAND), chunk)
        kn_ref[rows, :] = nat[:, :LANES]
        vn_ref[rows, :] = nat[:, LANES:]
        return carry

    lax.fori_loop(0, seq // chunk, to_sequence_order, 0, unroll=4)

    def softmax_block(q, k, v, bias):
        s = lax.dot_general(q, k, (((1,), (1,)), ((), ())), preferred_element_type=F32) + bias
        m_blk = jnp.max(s, axis=-1, keepdims=True)
        p = jnp.exp(s - m_blk).astype(BF16)
        a_l = jnp.dot(p, jnp.concatenate([v, ones], axis=1), preferred_element_type=F32)
        return a_l, jnp.broadcast_to(m_blk, (QBLK, LANES))

    def variant(qb, nb):
        return jnp.where(qb == 0, 0, jnp.where(qb == nb - 1, 2, 1))

    def merge(row_slices, a_l, m_b):
        m_old = jnp.concatenate([m_ref[rows, :] for rows in row_slices], axis=0)
        a_old = jnp.concatenate([acc_ref[rows, :] for rows in row_slices], axis=0)
        m_new = jnp.maximum(m_old, m_b)
        w_old = jnp.exp(m_old - m_new)
        w_blk = jnp.exp(m_b - m_new)
        a_new = (a_old * jnp.concatenate([w_old, w_old], axis=1)
                 + a_l * jnp.concatenate([w_blk, w_blk], axis=1))
        n = QBLK // len(row_slices)
        for e, rows in enumerate(row_slices):
            acc_ref[rows, :] = a_new[e * n:(e + 1) * n]
            m_ref[rows, :] = m_new[e * n:(e + 1) * n]

    nb16 = run // QBLK

    def block16(t, carry):
        r, qb = t // nb16, t % nb16
        off = pl.multiple_of(r * run + qb * QBLK, QBLK)
        a_l, m_b = softmax_block(q_ref[pl.ds(off, QBLK), :], kp_ref[pl.ds(off, KBLK), :],
                                 vp_ref[pl.ds(off, KBLK), :], bias_ref[2, variant(qb, nb16)])
        acc_ref[pl.ds(off, QBLK), :] = a_l
        m_ref[pl.ds(off, QBLK), :] = m_b
        return carry

    lax.fori_loop(0, PERM * nb16, block16, 0, unroll=8)

    dil = ATTN_PATTERNS[1][1]
    sub = PERM // dil
    nb4 = (seq // dil) // QBLK
    qn, kn = QBLK // sub, KBLK // sub

    def block4(t, carry):
        r, qb = t // nb4, t % nb4
        q_rows = [pl.ds(pl.multiple_of((dil * e + r) * run + qb * qn, qn), qn)
                  for e in range(sub)]
        k_rows = [pl.ds(pl.multiple_of((dil * e + r) * run + qb * qn + BAND - BAND // sub,
                                       BAND // sub), kn) for e in range(sub)]
        q = jnp.concatenate([q_ref[rows, :] for rows in q_rows], axis=0)
        k = jnp.concatenate([kp_ref[rows, :] for rows in k_rows], axis=0)
        v = jnp.concatenate([vp_ref[rows, :] for rows in k_rows], axis=0)
        a_l, m_b = softmax_block(q, k, v, bias_ref[1, variant(qb, nb4)])
        merge(q_rows, a_l, m_b)
        return carry

    lax.fori_loop(0, dil * nb4, block4, 0, unroll=8)

    nb1 = seq // QBLK
    qn1 = QBLK // PERM

    def block1(qb, carry):
        q_rows = [pl.ds(pl.multiple_of(r * run + qb * qn1, qn1), qn1) for r in range(PERM)]
        q = jnp.concatenate([qf_ref[rows, :] for rows in q_rows], axis=0).astype(BF16)
        k_rows = pl.ds(pl.multiple_of(qb * QBLK, QBLK), KBLK)
        a_l, m_b = softmax_block(q, kn_ref[k_rows, :], vn_ref[k_rows, :],
                                 bias_ref[0, variant(qb, nb1)])
        merge(q_rows, a_l, m_b)
        return carry

    lax.fori_loop(0, nb1, block1, 0, unroll=8)

    def finish(c, carry):
        st = runs(acc_ref, c * PERM, PERM)
        y = _rms(st[:, :LANES] / st[:, LANES:], gain_ref[...]).astype(BF16)
        nat = jnp.dot(perm_ref[...], y, preferred_element_type=F32)
        o_ref[pl.ds(pl.multiple_of(c * chunk, chunk), chunk), :] = nat.astype(BF16)
        return carry

    lax.fori_loop(0, seq // chunk, finish, 0, unroll=4)


def _attention(u_p, bias_tab, attn_out_norm, batch, seq, heads):
    assert ATTN_PATTERNS[0][1] == 1 and ATTN_PATTERNS[2][1] == PERM
    assert seq % (PERM * QBLK) == 0
    perm = jnp.asarray(_to_sequence_order_matrix(), BF16)
    pad_rows = seq + 2 * BAND
    return pl.pallas_call(
        functools.partial(_attn_kernel, seq=seq),
        out_shape=jax.ShapeDtypeStruct((heads, batch * seq, LANES), BF16),
        grid=(batch, heads),
        in_specs=[
            pl.BlockSpec((None, seq, LANES), lambda b, h: (h, b, 0)),
            pl.BlockSpec((None, seq, LANES), lambda b, h: (heads + h, b, 0)),
            pl.BlockSpec((None, seq, LANES), lambda b, h: (2 * heads + h, b, 0)),
            pl.BlockSpec((None, len(ATTN_PATTERNS), 3, QBLK, KBLK),
                         lambda b, h: (h, 0, 0, 0, 0)),
            pl.BlockSpec((None, 1, LANES), lambda b, h: (h, 0, 0)),
            pl.BlockSpec(perm.shape, lambda b, h: (0, 0)),
        ],
        out_specs=pl.BlockSpec((None, seq, LANES), lambda b, h: (h, b, 0)),
        scratch_shapes=[
            pltpu.VMEM((seq, LANES), F32),
            pltpu.VMEM((pad_rows, LANES), BF16),
            pltpu.VMEM((pad_rows, LANES), BF16),
            pltpu.VMEM((pad_rows, LANES), BF16),
            pltpu.VMEM((pad_rows, LANES), BF16),
            pltpu.VMEM((seq, 2 * LANES), F32),
            pltpu.VMEM((seq, LANES), F32),
        ],
        compiler_params=_cparams(("parallel", "parallel")),
        name="attention",
    )(u_p, u_p, u_p, bias_tab, attn_out_norm.reshape(heads, 1, LANES), perm)


def _filter_features(seq):
    pos = np.arange(seq, dtype=np.float32)
    t = pos / np.float32(max(seq - 1, 1))
    bands = (HY_EMB - 1) // 2
    fr = np.linspace(1e-4, bands - 1, bands, dtype=np.float32)
    ang = np.float32(2.0 * math.pi / seq) * pos[:, None] * fr[None, :]
    z = np.concatenate([t[:, None], np.cos(ang), -np.sin(ang)], axis=-1).astype(np.float32)
    zp = np.zeros((seq, LANES), np.float32)
    zp[:, :HY_EMB] = z
    offs = (np.abs(pos - (seq // 2)) / np.float32(seq / 2)).astype(np.float32)
    return zp, offs[:, None]


def _filter_kernel(z_ref, offs_ref, w1_ref, b1_ref, wi_ref, bi_ref, wo_ref, fq_ref,
                   decay_ref, o_ref):
    hp = lax.Precision.HIGHEST
    fq = fq_ref[...]
    h = jnp.sin(fq * (jnp.dot(z_ref[...], w1_ref[...], precision=hp,
                              preferred_element_type=F32) + b1_ref[...]))
    for j in range(HY_INNER):
        h = jnp.sin(fq * (jnp.dot(h, wi_ref[j], precision=hp,
                                  preferred_element_type=F32) + bi_ref[j]))
    filt = jnp.dot(h, wo_ref[...], precision=hp, preferred_element_type=F32)
    o_ref[...] = filt * jnp.exp(-offs_ref[...] * jnp.abs(decay_ref[...]))


def _hyena_filter(seq, hy_w, w1, b1, wi, bi, wo, freq, decay):
    zp, offs = _filter_features(seq)
    w1p = jnp.zeros((LANES, HY_FILTER_WIDTH), F32).at[:HY_EMB].set(w1.astype(F32))
    tr = 512
    fw = HY_FILTER_WIDTH
    const = lambda *shape: pl.BlockSpec(shape, lambda i: (0,) * len(shape))
    return pl.pallas_call(
        _filter_kernel,
        out_shape=jax.ShapeDtypeStruct((seq, hy_w), F32),
        grid=(seq // tr,),
        in_specs=[
            pl.BlockSpec((tr, LANES), lambda i: (i, 0)),
            pl.BlockSpec((tr, 1), lambda i: (i, 0)),
            const(LANES, fw), const(1, fw), const(HY_INNER, fw, fw),
            const(HY_INNER, 1, fw), const(fw, hy_w), const(1, fw), const(1, hy_w),
        ],
        out_specs=pl.BlockSpec((tr, hy_w), lambda i: (i, 0)),
        compiler_params=_cparams(("parallel",)),
        name="hyena_filter",
    )(jnp.asarray(zp), jnp.asarray(offs), w1p, b1.reshape(1, fw).astype(F32),
      wi.astype(F32), bi.reshape(HY_INNER, 1, fw).astype(F32), wo.astype(F32),
      freq.reshape(1, fw).astype(F32), decay.reshape(1, hy_w).astype(F32))


def _stack(c):
    return np.block([[c.real, -c.imag], [c.imag, c.real]])


@functools.lru_cache(maxsize=None)
def _dft_constants(seq):
    n = 2 * seq
    r = FFT_R
    a_n = n // r
    ar = np.arange(a_n)
    br = np.arange(r)
    f1 = np.exp(-2j * np.pi * np.outer(ar, ar) / a_n)
    f1_fwd = _stack(f1[:, :a_n // 2])
    f1_fwd_real = np.concatenate([f1[:, :a_n // 2].real, f1[:, :a_n // 2].imag], 0)
    f1_inv = _stack(np.conj(f1).T[a_n // 4:3 * a_n // 4] / n)
    ph = (br[None, None, :] * br[None, :, None] / r
          + br[None, None, :] * ar[:, None, None] / n)
    g = np.exp(-2j * np.pi * ph)
    g_fwd = np.stack([_stack(g[k]) for k in range(a_n)])
    g_inv = np.stack([_stack(np.conj(g[k]).T) for k in range(a_n)])
    return (f1_fwd.astype(np.float32), f1_fwd_real.astype(np.float32),
            f1_inv.astype(np.float32), g_fwd.astype(np.float32), g_inv.astype(np.float32))


def _split_bf16(x):
    if isinstance(x, np.ndarray):
        hi = x.astype(BF16)
        lo = (x - hi.astype(np.float32)).astype(BF16)
        return jnp.asarray(hi), jnp.asarray(lo)
    hi = x.astype(BF16)
    return hi, (x - hi.astype(F32)).astype(BF16)


def _dot3(m_hi, m_lo, x):
    x_hi, x_lo = _split_bf16(x)
    dot = functools.partial(jnp.dot, preferred_element_type=F32)
    return dot(m_hi, x_hi) + (dot(m_hi, x_lo) + dot(m_lo, x_hi))


def _spectrum_kernel(filt_ref, f1h_ref, f1l_ref, gh_ref, gl_ref, hr_ref, hi_ref,
                     zs_ref, ts_ref, *, seq):
    r = FFT_R
    a_half = seq // r
    a_n = 2 * a_half
    for a in range(a_half):
        zs_ref[pl.ds(a * FFT_PITCH, r), :] = filt_ref[pl.ds(a * r, r), :]

    def stage1(bp, carry):
        rhs = jnp.concatenate(
            [zs_ref[pl.ds(2 * bp + e, a_half, stride=FFT_PITCH), :] for e in range(2)], axis=1)
        t = _dot3(f1h_ref[...], f1l_ref[...], rhs)
        for e in range(2):
            for c in range(2):
                ts_ref[c, pl.ds(2 * bp + e, a_n, stride=FFT_PITCH), :] = (
                    t[c * a_n:(c + 1) * a_n, e * LANES:(e + 1) * LANES])
        return carry

    lax.fori_loop(0, r // 2, stage1, 0, unroll=4)

    def stage2(ka, carry):
        rows = pl.ds(pl.multiple_of(ka * FFT_PITCH, 8), r)
        rhs = jnp.concatenate([ts_ref[0, rows, :], ts_ref[1, rows, :]], axis=0)
        s = _dot3(gh_ref[ka], gl_ref[ka], rhs)
        out = pl.ds(pl.multiple_of(ka * r, r), r)
        hr_ref[out, :] = s[:r]
        hi_ref[out, :] = s[r:]
        return carry

    lax.fori_loop(0, a_n, stage2, 0, unroll=8)


def _filter_spectrum(filt, seq):
    hy_w = filt.shape[1]
    n = 2 * seq
    _, f1_real, _, g_fwd, _ = _dft_constants(seq)
    f1h, f1l = _split_bf16(f1_real)
    gh, gl = _split_bf16(g_fwd)
    a_n = n // FFT_R
    out = jax.ShapeDtypeStruct((n, hy_w), F32)
    gspec = pl.BlockSpec(g_fwd.shape, lambda c: (0, 0, 0), pipeline_mode=pl.Buffered(1))
    return pl.pallas_call(
        functools.partial(_spectrum_kernel, seq=seq),
        out_shape=(out, out),
        grid=(hy_w // LANES,),
        in_specs=[
            pl.BlockSpec((seq, LANES), lambda c: (0, c)),
            pl.BlockSpec(f1_real.shape, lambda c: (0, 0)),
            pl.BlockSpec(f1_real.shape, lambda c: (0, 0)),
            gspec, gspec,
        ],
        out_specs=(pl.BlockSpec((n, LANES), lambda c: (0, c)),
                   pl.BlockSpec((n, LANES), lambda c: (0, c))),
        scratch_shapes=[
            pltpu.VMEM((a_n // 2 * FFT_PITCH, LANES), F32),
            pltpu.VMEM((2, a_n * FFT_PITCH, LANES), F32),
        ],
        compiler_params=_cparams(("parallel",)),
        name="filter_spectrum",
    )(filt, f1h, f1l, gh, gl)


def _hy_front_kernel(x0_ref, x1_ref, hv_ref, w0_ref, w1_ref, wv_ref, b0_ref, b1_ref, bv_ref,
                     z_ref, x0c_ref, pad_ref, *, seq):
    chunk = 512
    zero_row = jnp.zeros((8, LANES), F32)
    for idx, ref in enumerate((x0_ref, x1_ref, hv_ref)):
        pad_ref[idx, pl.ds(0, 8), :] = zero_row
        pad_ref[idx, pl.ds(8 + seq, 8), :] = zero_row
        pad_ref[idx, pl.ds(8, seq), :] = ref[...].astype(F32)

    def conv(idx, w_ref, b_ref, c):
        base = 8 + c * chunk
        w = w_ref[...]
        return (pad_ref[idx, pl.ds(base - 1, chunk), :] * w[0:1]
                + pad_ref[idx, pl.ds(base, chunk), :] * w[1:2]
                + pad_ref[idx, pl.ds(base + 1, chunk), :] * w[2:3]
                + b_ref[...])

    for c in range(seq // chunk):
        rows = pl.ds(c * chunk, chunk)
        x0c_ref[rows, :] = conv(0, w0_ref, b0_ref, c).astype(BF16)
        z_ref[rows, :] = (conv(2, wv_ref, bv_ref, c) * conv(1, w1_ref, b1_ref, c)).astype(BF16)


def _hyena_front(u_t, conv_w, conv_b, batch, seq, groups):
    ng = 3 * groups
    cw = conv_w.astype(F32).reshape(3, ng, LANES).transpose(1, 0, 2)
    cb = conv_b.astype(F32).reshape(ng, 1, LANES)
    uspec = lambda off: pl.BlockSpec((None, seq, LANES), lambda b, g: (off + g, b, 0))
    wspec = lambda off: pl.BlockSpec((None, 3, LANES), lambda b, g: (off + g, 0, 0))
    bspec = lambda off: pl.BlockSpec((None, 1, LANES), lambda b, g: (off + g, 0, 0))
    out = jax.ShapeDtypeStruct((groups, batch * seq, LANES), BF16)
    ospec = pl.BlockSpec((None, seq, LANES), lambda b, g: (g, b, 0))
    return pl.pallas_call(
        functools.partial(_hy_front_kernel, seq=seq),
        out_shape=(out, out),
        grid=(batch, groups),
        in_specs=[uspec(0), uspec(groups), uspec(2 * groups),
                  wspec(0), wspec(groups), wspec(2 * groups),
                  bspec(0), bspec(groups), bspec(2 * groups)],
        out_specs=(ospec, ospec),
        scratch_shapes=[pltpu.VMEM((3, seq + 16, LANES), F32)],
        compiler_params=_cparams(("parallel", "parallel")),
        name="hyena_front",
    )(u_t, u_t, u_t, cw, cw, cw, cb, cb, cb)


def _hy_conv_kernel(z_ref, x_ref, hr_ref, hi_ref, f1_ref, f1i_ref,
                    g_ref, gi_ref, bias_ref, gain_ref, o_ref,
                    zs_ref, ts_ref, ys_ref, *, seq):
    r = FFT_R
    a_half = seq // r
    a_n = 2 * a_half
    for c in range(2):
        for a in range(a_half):
            zs_ref[c, pl.ds(a * FFT_PITCH, r), :] = (
                z_ref[pl.ds(c * seq + a * r, r), :].astype(F32))

    def stage1(bp, carry):
        cols = []
        for e in range(2):
            rows = pl.ds(2 * bp + e, a_half, stride=FFT_PITCH)
            cols.append(jnp.concatenate([zs_ref[0, rows, :], zs_ref[1, rows, :]], axis=0))
        rhs = jnp.concatenate(cols, axis=1).astype(BF16)
        t = jnp.dot(f1_ref[...], rhs, preferred_element_type=F32)
        for e in range(2):
            for c in range(2):
                ts_ref[c, pl.ds(2 * bp + e, a_n, stride=FFT_PITCH), :] = (
                    t[c * a_n:(c + 1) * a_n, e * LANES:(e + 1) * LANES])
        return carry

    lax.fori_loop(0, r // 2, stage1, 0, unroll=4)

    def stage2(ka, carry):
        rows = pl.ds(pl.multiple_of(ka * FFT_PITCH, 8), r)
        rhs = jnp.concatenate([ts_ref[0, rows, :], ts_ref[1, rows, :]], axis=0).astype(BF16)
        s = jnp.dot(g_ref[ka], rhs, preferred_element_type=F32)
        hrows = pl.ds(pl.multiple_of(ka * r, r), r)
        hr = hr_ref[hrows, :]
        hi = hi_ref[hrows, :]
        sr, si = s[:r], s[r:]
        prod = jnp.concatenate([sr * hr - si * hi, sr * hi + si * hr], axis=0).astype(BF16)
        u = jnp.dot(gi_ref[ka], prod, preferred_element_type=F32)
        ts_ref[0, rows, :] = u[:r]
        ts_ref[1, rows, :] = u[r:]
        return carry

    lax.fori_loop(0, a_n, stage2, 0, unroll=16)

    def stage3(bp, carry):
        cols = []
        for e in range(2):
            rows = pl.ds(2 * bp + e, a_n, stride=FFT_PITCH)
            cols.append(jnp.concatenate([ts_ref[0, rows, :], ts_ref[1, rows, :]], axis=0))
        rhs = jnp.concatenate(cols, axis=1).astype(BF16)
        y = jnp.dot(f1i_ref[...], rhs, preferred_element_type=F32)
        for e in range(2):
            for c in range(2):
                ys_ref[c, pl.ds(2 * bp + e, a_half, stride=FFT_PITCH), :] = (
                    y[c * a_half:(c + 1) * a_half, e * LANES:(e + 1) * LANES])
        return carry

    lax.fori_loop(0, r // 2, stage3, 0, unroll=4)

    def finish(a, carry):
        prow = pl.ds(pl.multiple_of(a * FFT_PITCH, 8), r)
        for c in range(2):
            orow = pl.ds(pl.multiple_of(c * seq + a * r, r), r)
            z = ys_ref[c, prow, :] + zs_ref[c, prow, :] * bias_ref[...]
            y = z * x_ref[orow, :].astype(F32)
            o_ref[orow, :] = _rms(y, gain_ref[...]).astype(BF16)
        return carry

    lax.fori_loop(0, a_half, finish, 0, unroll=4)


def _hyena_conv(z_t, x0c_t, h_re, h_im, hy_bias, hy_out_norm, batch, seq, groups):
    n = 2 * seq
    f1_fwd, _, f1_inv, g_fwd, g_inv = _dft_constants(seq)
    a_n = n // FFT_R
    zspec = pl.BlockSpec((None, 2 * seq, LANES), lambda g, p: (g, p, 0))
    hspec = pl.BlockSpec((n, LANES), lambda g, p: (0, g), pipeline_mode=pl.Buffered(1))
    cspec = lambda arr: pl.BlockSpec(arr.shape, lambda g, p: (0,) * arr.ndim,
                                     pipeline_mode=pl.Buffered(1))
    vspec = pl.BlockSpec((None, 1, LANES), lambda g, p: (g, 0, 0))
    out = jax.ShapeDtypeStruct((groups, batch * seq, LANES), BF16)
    return pl.pallas_call(
        functools.partial(_hy_conv_kernel, seq=seq),
        out_shape=out,
        grid=(groups, batch // 2),
        in_specs=[zspec, zspec, hspec, hspec,
                  cspec(f1_fwd), cspec(f1_inv), cspec(g_fwd), cspec(g_inv), vspec, vspec],
        out_specs=zspec,
        scratch_shapes=[
            pltpu.VMEM((2, a_n // 2 * FFT_PITCH, LANES), F32),
            pltpu.VMEM((2, a_n * FFT_PITCH, LANES), F32),
            pltpu.VMEM((2, a_n // 2 * FFT_PITCH, LANES), F32),
        ],
        compiler_params=_cparams(("parallel", "arbitrary")),
        name="hyena_conv",
    )(z_t, x0c_t, h_re, h_im,
      jnp.asarray(f1_fwd, BF16), jnp.asarray(f1_inv, BF16),
      jnp.asarray(g_fwd, BF16), jnp.asarray(g_inv, BF16),
      hy_bias.astype(F32).reshape(groups, 1, LANES),
      hy_out_norm.astype(F32).reshape(groups, 1, LANES))


def _out_proj_kernel(ya_ref, yh_ref, x_ref, w_ref, o_ref, *, heads, groups):
    lhs = jnp.concatenate([ya_ref[g] for g in range(heads)]
                          + [yh_ref[g] for g in range(groups)], axis=-1)
    o_ref[...] = x_ref[...] + jnp.dot(lhs, w_ref[...], preferred_element_type=F32)


def _out_proj(ya_t, yh_t, x2, w_out, tm):
    m, d = x2.shape
    heads, groups = ya_t.shape[0], yh_t.shape[0]
    return pl.pallas_call(
        functools.partial(_out_proj_kernel, heads=heads, groups=groups),
        out_shape=jax.ShapeDtypeStruct((m, d), F32),
        grid=(m // tm,),
        in_specs=[
            pl.BlockSpec((heads, tm, LANES), lambda i: (0, i, 0)),
            pl.BlockSpec((groups, tm, LANES), lambda i: (0, i, 0)),
            pl.BlockSpec((tm, d), lambda i: (i, 0)),
            pl.BlockSpec(w_out.shape, lambda i: (0, 0)),
        ],
        out_specs=pl.BlockSpec((tm, d), lambda i: (i, 0)),
        compiler_params=_cparams(("parallel",)),
        name="out_proj",
    )(ya_t, yh_t, x2, w_out.astype(BF16))


def _ffn_kernel(h_ref, g_ref, wa_ref, wg_ref, wd_ref, o_ref, xn_ref):
    @pl.when(pl.program_id(1) == 0)
    def _():
        h = h_ref[...]
        xn_ref[...] = _rms(h, g_ref[...]).astype(BF16)
        o_ref[...] = h

    xn = xn_ref[...]
    a = jnp.dot(xn, wa_ref[...], preferred_element_type=F32)
    g = jnp.dot(xn, wg_ref[...], preferred_element_type=F32)
    act = (a * jax.nn.sigmoid(a) * g).astype(BF16)
    o_ref[...] += jnp.dot(act, wd_ref[...], preferred_element_type=F32)


def _ffn(h1, norm2, w_gu, w_down, tm, th):
    m, d = h1.shape
    hidden = w_down.shape[0]
    nh = hidden // th
    w_gu = w_gu.astype(BF16)
    return pl.pallas_call(
        _ffn_kernel,
        out_shape=jax.ShapeDtypeStruct((m, d), F32),
        grid=(m // tm, nh),
        in_specs=[
            pl.BlockSpec((tm, d), lambda i, j: (i, 0)),
            pl.BlockSpec((1, d), lambda i, j: (0, 0)),
            pl.BlockSpec((d, th), lambda i, j: (0, j)),
            pl.BlockSpec((d, th), lambda i, j: (0, nh + j)),
            pl.BlockSpec((th, d), lambda i, j: (j, 0)),
        ],
        out_specs=pl.BlockSpec((tm, d), lambda i, j: (i, 0)),
        scratch_shapes=[pltpu.VMEM((tm, d), BF16)],
        compiler_params=_cparams(("parallel", "arbitrary")),
        name="ffn",
    )(h1, norm2.reshape(1, d), w_gu, w_gu, w_down.astype(BF16))


def _ple_kernel(h_ref, p_ref, gn_ref, wg_ref, wp_ref, pn_ref, o_ref):
    h = h_ref[...]
    gate = jax.nn.sigmoid(jnp.dot(_rms(h, gn_ref[...]).astype(BF16), wg_ref[...],
                                  preferred_element_type=F32))
    e = _rms(jnp.dot(p_ref[...].astype(BF16), wp_ref[...], preferred_element_type=F32),
             pn_ref[...])
    o_ref[...] = h + gate * e


def _ple(h2, p2, ple_norm, w_gate, w_proj, ple_post_norm, tm):
    m, d = h2.shape
    pd = p2.shape[1]
    return pl.pallas_call(
        _ple_kernel,
        out_shape=jax.ShapeDtypeStruct((m, d), F32),
        grid=(m // tm,),
        in_specs=[
            pl.BlockSpec((tm, d), lambda i: (i, 0)),
            pl.BlockSpec((tm, pd), lambda i: (i, 0)),
            pl.BlockSpec((1, d), lambda i: (0, 0)),
            pl.BlockSpec((d, d), lambda i: (0, 0)),
            pl.BlockSpec((pd, d), lambda i: (0, 0)),
            pl.BlockSpec((1, d), lambda i: (0, 0)),
        ],
        out_specs=pl.BlockSpec((tm, d), lambda i: (i, 0)),
        compiler_params=_cparams(("parallel",)),
        name="ple",
    )(h2, p2, ple_norm.reshape(1, d), w_gate.astype(BF16), w_proj.astype(BF16),
      ple_post_norm.reshape(1, d))


def kernel(x, p, rel_bias, norm1, w_in, q_norm, k_norm, conv_w, conv_b, hy_w1, hy_b1, hy_wi, hy_bi, hy_wo, hy_freq, hy_decay, hy_bias, attn_out_norm, hy_out_norm, w_out, norm2, w_gu, w_down, ple_norm, w_ple_gate, w_ple_proj, ple_post_norm):
    batch, seq, d = x.shape
    attn_w = d // 2
    heads = attn_w // HEAD_DIM
    groups = heads
    hy_w = groups * LANES
    m = batch * seq
    tm = 512
    tm_big = min(1024, m)
    h = x.reshape(m, d)
    bias_tab = _attn_bias_tables(rel_bias)
    for i in range(norm1.shape[0]):
        h3 = h.reshape(batch, seq, d)
        u_p, u_t = _in_proj(h3, norm1[i], w_in[i], q_norm[i], k_norm[i], attn_w, tm_big)
        ya_t = _attention(u_p, bias_tab, attn_out_norm[i], batch, seq, heads)
        filt = _hyena_filter(seq, hy_w, hy_w1[i], hy_b1[i], hy_wi[i], hy_bi[i], hy_wo[i],
                             hy_freq[i], hy_decay[i])
        h_re, h_im = _filter_spectrum(filt, seq)
        z_t, x0c_t = _hyena_front(u_t, conv_w[i], conv_b[i], batch, seq, groups)
        yh_t = _hyena_conv(z_t, x0c_t, h_re, h_im, hy_bias[i], hy_out_norm[i],
                           batch, seq, groups)
        h = _out_proj(ya_t, yh_t, h, w_out[i], tm)
        h = _ffn(h, norm2[i], w_gu[i], w_down[i], tm_big, 512)
        h = _ple(h, p[i].reshape(m, PLE_DIM), ple_norm[i], w_ple_gate[i], w_ple_proj[i],
                 ple_post_norm[i], tm)
    return h.reshape(batch, seq, d)
```

```python
import functools
import math

import jax
import jax.numpy as jnp
import numpy as np
from jax import lax
from jax.experimental import pallas as pl
from jax.experimental.pallas import tpu as pltpu

F32 = jnp.float32
BF16 = jnp.bfloat16

LANES = 128
HEAD_DIM = 128
ATTN_PATTERNS = ((128, 1), (512, 4), (2048, 16))
PERM = 16
BAND = 64
QBLK = 128
KBLK = QBLK + 2 * BAND
N_BUCKETS = 32
REL_MAX_DIST = 1024
HY_EMB = 33
HY_FILTER_WIDTH = 64
HY_INNER = 2
EPS = 1e-6
NEG = -1e30
PLE_DIM = 256

FFT_R = 64
FFT_PITCH = FFT_R + 8
VMEM_LIMIT = 56 * 1024 * 1024


def _cparams(sem, vmem=VMEM_LIMIT):
    return pltpu.CompilerParams(dimension_semantics=sem, vmem_limit_bytes=vmem)


def _rms(x, gain):
    ms = jnp.mean(x * x, axis=-1, keepdims=True)
    return x * lax.rsqrt(ms + EPS) * gain


def _in_proj_kernel(x_ref, g_ref, w_ref, qg_ref, kg_ref, perm_ref, oq_ref, oh_ref,
                    xn_ref, xp_ref, *, nq, hpt):
    j = pl.program_id(1)
    tm = x_ref.shape[0]
    chunk = PERM * PERM

    @pl.when(j == 0)
    def _():
        xn_ref[...] = _rms(x_ref[...], g_ref[...]).astype(BF16)
        for c in range(tm // chunk):
            rows = pl.ds(c * chunk, chunk)
            xp_ref[rows, :] = jnp.dot(perm_ref[...], xn_ref[rows, :],
                                      preferred_element_type=F32).astype(BF16)

    @pl.when(j < 3 * nq)
    def _():
        acc = jnp.dot(xp_ref[...], w_ref[...], preferred_element_type=F32)
        is_q = j < nq
        is_k = (j >= nq) & (j < 2 * nq)
        gain = jnp.where(is_q, qg_ref[...] * (HEAD_DIM ** -0.5),
                         jnp.where(is_k, kg_ref[...], 1.0))
        for hh in range(hpt):
            a = acc[:, hh * LANES:(hh + 1) * LANES]
            inv = lax.rsqrt(jnp.mean(a * a, axis=-1, keepdims=True) + EPS)
            a = (a * jnp.where(is_q | is_k, inv, 1.0) * gain).astype(BF16)
            for c in range(tm // chunk):
                for r in range(PERM):
                    oq_ref[hh, r, pl.ds(c * PERM, PERM), :] = (
                        a[c * chunk + r * PERM:c * chunk + (r + 1) * PERM])

    @pl.when(j >= 3 * nq)
    def _():
        acc = jnp.dot(xn_ref[...], w_ref[...], preferred_element_type=F32)
        for hh in range(hpt):
            oh_ref[hh] = acc[:, hh * LANES:(hh + 1) * LANES].astype(BF16)


def _in_proj(x3, norm1, w_in, q_norm, k_norm, attn_w, tm):
    batch, seq, d = x3.shape
    m = batch * seq
    in_w = w_in.shape[1]
    tn = min(1024, attn_w)
    hpt = tn // LANES
    nq = attn_w // tn
    n_qkv = 3 * nq
    n_tiles = in_w // tn
    tiles_per_seq = seq // tm
    assert tm % (PERM * PERM) == 0 and seq % tm == 0
    perm = jnp.asarray(_to_sequence_order_matrix().T, BF16)
    out_q = jax.ShapeDtypeStruct((n_qkv * hpt, batch, PERM, seq // PERM, LANES), BF16)
    out_h = jax.ShapeDtypeStruct(((n_tiles - n_qkv) * hpt, m, LANES), BF16)
    q_spec = pl.BlockSpec(
        (hpt, None, PERM, tm // PERM, LANES),
        lambda i, j: (jnp.minimum(j, n_qkv - 1), i // tiles_per_seq, 0, i % tiles_per_seq, 0))
    h_spec = pl.BlockSpec((hpt, tm, LANES), lambda i, j: (jnp.maximum(j - n_qkv, 0), i, 0))
    u_q, u_h = pl.pallas_call(
        functools.partial(_in_proj_kernel, nq=nq, hpt=hpt),
        out_shape=(out_q, out_h),
        grid=(m // tm, n_tiles),
        in_specs=[
            pl.BlockSpec((tm, d), lambda i, j: (i, 0)),
            pl.BlockSpec((1, d), lambda i, j: (0, 0)),
            pl.BlockSpec((d, tn), lambda i, j: (0, j)),
            pl.BlockSpec((1, LANES), lambda i, j: (0, 0)),
            pl.BlockSpec((1, LANES), lambda i, j: (0, 0)),
            pl.BlockSpec(perm.shape, lambda i, j: (0, 0)),
        ],
        out_specs=(q_spec, h_spec),
        scratch_shapes=[pltpu.VMEM((tm, d), BF16), pltpu.VMEM((tm, d), BF16)],
        compiler_params=_cparams(("parallel", "arbitrary")),
        name="in_proj",
    )(x3.reshape(m, d), norm1.reshape(1, d), w_in.astype(BF16), q_norm.reshape(1, LANES),
      k_norm.reshape(1, LANES), perm)
    return u_q.reshape(n_qkv * hpt, m, LANES), u_h


def _t5_bucket(rel):
    half = N_BUCKETS // 2
    exact = half // 2
    n = np.abs(rel)
    large = exact + (np.log(np.maximum(n, 1).astype(np.float32) / np.float32(exact))
                     / np.float32(math.log(REL_MAX_DIST / exact))
                     * np.float32(half - exact)).astype(np.int32)
    large = np.minimum(large, half - 1)
    return np.where(rel > 0, half, 0) + np.where(n < exact, n, large)


def _block_orders(dil):
    sub = PERM // dil
    e, i = np.divmod(np.arange(QBLK), QBLK // sub)
    q_off = sub * i + e
    if dil == 1:
        k_off = np.arange(KBLK) - BAND
    else:
        e, i = np.divmod(np.arange(KBLK), KBLK // sub)
        k_off = sub * i + e - BAND
    return q_off, k_off


def _to_sequence_order_matrix():
    n = PERM * PERM
    mat = np.zeros((n, n), np.float32)
    r, i = np.divmod(np.arange(n), PERM)
    mat[PERM * i + r, np.arange(n)] = 1.0
    return mat


def _attn_bias_tables(rel_bias):
    tabs = []
    for _, dil in ATTN_PATTERNS:
        q_off, k_off = _block_orders(dil)
        rel = k_off[None, :] - q_off[:, None]
        band = np.abs(rel) <= BAND
        first_ok = np.broadcast_to(k_off[None, :] >= 0, rel.shape)
        last_ok = np.broadcast_to(k_off[None, :] < QBLK, rel.shape)
        onehot = np.eye(N_BUCKETS, dtype=np.float32)[_t5_bucket(rel * dil)]
        bias = jnp.einsum("qkn,nh->hqk", jnp.asarray(onehot), rel_bias.astype(F32),
                          precision=lax.Precision.HIGHEST)
        variants = [jnp.where(jnp.asarray(band & ok), bias, NEG)
                    for ok in (first_ok, np.ones_like(band), last_ok)]
        tabs.append(jnp.stack(variants, axis=1))
    return jnp.stack(tabs, axis=1)


def _attn_kernel(q_ref, k_ref, v_ref, bias_ref, gain_ref, perm_ref, o_ref,
                 qf_ref, kp_ref, vp_ref, kn_ref, vn_ref, acc_ref, m_ref, *, seq):
    run = seq // PERM
    chunk = PERM * PERM
    ones = jnp.ones((KBLK, LANES), BF16)
    zpad = jnp.zeros((BAND, LANES), BF16)

    for ref in (kp_ref, vp_ref, kn_ref, vn_ref):
        ref[pl.ds(0, BAND), :] = zpad
        ref[pl.ds(BAND + seq, BAND), :] = zpad
    kp_ref[pl.ds(BAND, seq), :] = k_ref[...]
    vp_ref[pl.ds(BAND, seq), :] = v_ref[...]
    qf_ref[...] = q_ref[...].astype(F32)

    def runs(ref, base, n):
        return jnp.concatenate(
            [ref[pl.ds(pl.multiple_of(r * run + base, n), n), :] for r in range(PERM)], axis=0)

    def to_sequence_order(c, carry):
        base = c * PERM
        kv = jnp.concatenate([runs(k_ref, base, PERM), runs(v_ref, base, PERM)], axis=1)
        nat = jnp.dot(perm_ref[...], kv, preferred_element_type=F32).astype(BF16)
        rows = pl.ds(pl.multiple_of(BAND + c * chunk, BAND), chunk)
        kn_ref[rows, :] = nat[:, :LANES]
        vn_ref[rows, :] = nat[:, LANES:]
        return carry

    lax.fori_loop(0, seq // chunk, to_sequence_order, 0, unroll=4)

    def softmax_block(q, k, v, bias):
        s = lax.dot_general(q, k, (((1,), (1,)), ((), ())), preferred_element_type=F32) + bias
        m_blk = jnp.max(s, axis=-1, keepdims=True)
        p = jnp.exp(s - m_blk).astype(BF16)
        a_l = jnp.dot(p, jnp.concatenate([v, ones], axis=1), preferred_element_type=F32)
        return a_l, jnp.broadcast_to(m_blk, (QBLK, LANES))

    def variant(qb, nb):
        return jnp.where(qb == 0, 0, jnp.where(qb == nb - 1, 2, 1))

    def merge(row_slices, a_l, m_b):
        m_old = jnp.concatenate([m_ref[rows, :] for rows in row_slices], axis=0)
        a_old = jnp.concatenate([acc_ref[rows, :] for rows in row_slices], axis=0)
        m_new = jnp.maximum(m_old, m_b)
        w_old = jnp.exp(m_old - m_new)
        w_blk = jnp.exp(m_b - m_new)
        a_new = (a_old * jnp.concatenate([w_old, w_old], axis=1)
                 + a_l * jnp.concatenate([w_blk, w_blk], axis=1))
        n = QBLK // len(row_slices)
        for e, rows in enumerate(row_slices):
            acc_ref[rows, :] = a_new[e * n:(e + 1) * n]
            m_ref[rows, :] = m_new[e * n:(e + 1) * n]

    nb16 = run // QBLK

    def block16(t, carry):
        r, qb = t // nb16, t % nb16
        off = pl.multiple_of(r * run + qb * QBLK, QBLK)
        a_l, m_b = softmax_block(q_ref[pl.ds(off, QBLK), :], kp_ref[pl.ds(off, KBLK), :],
                                 vp_ref[pl.ds(off, KBLK), :], bias_ref[2, variant(qb, nb16)])
        acc_ref[pl.ds(off, QBLK), :] = a_l
        m_ref[pl.ds(off, QBLK), :] = m_b
        return carry

    lax.fori_loop(0, PERM * nb16, block16, 0, unroll=8)

    dil = ATTN_PATTERNS[1][1]
    sub = PERM // dil
    nb4 = (seq // dil) // QBLK
    qn, kn = QBLK // sub, KBLK // sub

    def block4(t, carry):
        r, qb = t // nb4, t % nb4
        q_rows = [pl.ds(pl.multiple_of((dil * e + r) * run + qb * qn, qn), qn)
                  for e in range(sub)]
        k_rows = [pl.ds(pl.multiple_of((dil * e + r) * run + qb * qn + BAND - BAND // sub,
                                       BAND // sub), kn) for e in range(sub)]
        q = jnp.concatenate([q_ref[rows, :] for rows in q_rows], axis=0)
        k = jnp.concatenate([kp_ref[rows, :] for rows in k_rows], axis=0)
        v = jnp.concatenate([vp_ref[rows, :] for rows in k_rows], axis=0)
        a_l, m_b = softmax_block(q, k, v, bias_ref[1, variant(qb, nb4)])
        merge(q_rows, a_l, m_b)
        return carry

    lax.fori_loop(0, dil * nb4, block4, 0, unroll=8)

    nb1 = seq // QBLK
    qn1 = QBLK // PERM

    def block1(qb, carry):
        q_rows = [pl.ds(pl.multiple_of(r * run + qb * qn1, qn1), qn1) for r in range(PERM)]
        q = jnp.concatenate([qf_ref[rows, :] for rows in q_rows], axis=0).astype(BF16)
        k_rows = pl.ds(pl.multiple_of(qb * QBLK, QBLK), KBLK)
        a_l, m_b = softmax_block(q, kn_ref[k_rows, :], vn_ref[k_rows, :],
                                 bias_ref[0, variant(qb, nb1)])
        merge(q_rows, a_l, m_b)
        return carry

    lax.fori_loop(0, nb1, block1, 0, unroll=8)

    def finish(c, carry):
        st = runs(acc_ref, c * PERM, PERM)
        y = _rms(st[:, :LANES] / st[:, LANES:], gain_ref[...]).astype(BF16)
        nat = jnp.dot(perm_ref[...], y, preferred_element_type=F32)
        o_ref[pl.ds(pl.multiple_of(c * chunk, chunk), chunk), :] = nat.astype(BF16)
        return carry

    lax.fori_loop(0, seq // chunk, finish, 0, unroll=4)


def _attention(u_p, bias_tab, attn_out_norm, batch, seq, heads):
    assert ATTN_PATTERNS[0][1] == 1 and ATTN_PATTERNS[2][1] == PERM
    assert seq % (PERM * QBLK) == 0
    perm = jnp.asarray(_to_sequence_order_matrix(), BF16)
    pad_rows = seq + 2 * BAND
    return pl.pallas_call(
        functools.partial(_attn_kernel, seq=seq),
        out_shape=jax.ShapeDtypeStruct((heads, batch * seq, LANES), BF16),
        grid=(batch, heads),
        in_specs=[
            pl.BlockSpec((None, seq, LANES), lambda b, h: (h, b, 0)),
            pl.BlockSpec((None, seq, LANES), lambda b, h: (heads + h, b, 0)),
            pl.BlockSpec((None, seq, LANES), lambda b, h: (2 * heads + h, b, 0)),
            pl.BlockSpec((None, len(ATTN_PATTERNS), 3, QBLK, KBLK),
                         lambda b, h: (h, 0, 0, 0, 0)),
            pl.BlockSpec((None, 1, LANES), lambda b, h: (h, 0, 0)),
            pl.BlockSpec(perm.shape, lambda b, h: (0, 0)),
        ],
        out_specs=pl.BlockSpec((None, seq, LANES), lambda b, h: (h, b, 0)),
        scratch_shapes=[
            pltpu.VMEM((seq, LANES), F32),
            pltpu.VMEM((pad_rows, LANES), BF16),
            pltpu.VMEM((pad_rows, LANES), BF16),
            pltpu.VMEM((pad_rows, LANES), BF16),
            pltpu.VMEM((pad_rows, LANES), BF16),
            pltpu.VMEM((seq, 2 * LANES), F32),
            pltpu.VMEM((seq, LANES), F32),
        ],
        compiler_params=_cparams(("parallel", "parallel")),
        name="attention",
    )(u_p, u_p, u_p, bias_tab, attn_out_norm.reshape(heads, 1, LANES), perm)


def _filter_features(seq):
    pos = np.arange(seq, dtype=np.float32)
    t = pos / np.float32(max(seq - 1, 1))
    bands = (HY_EMB - 1) // 2
    fr = np.linspace(1e-4, bands - 1, bands, dtype=np.float32)
    ang = np.float32(2.0 * math.pi / seq) * pos[:, None] * fr[None, :]
    z = np.concatenate([t[:, None], np.cos(ang), -np.sin(ang)], axis=-1).astype(np.float32)
    zp = np.zeros((seq, LANES), np.float32)
    zp[:, :HY_EMB] = z
    offs = (np.abs(pos - (seq // 2)) / np.float32(seq / 2)).astype(np.float32)
    return zp, offs[:, None]


def _filter_kernel(z_ref, offs_ref, w1_ref, b1_ref, wi_ref, bi_ref, wo_ref, fq_ref,
                   decay_ref, o_ref):
    hp = lax.Precision.HIGHEST
    fq = fq_ref[...]
    h = jnp.sin(fq * (jnp.dot(z_ref[...], w1_ref[...], precision=hp,
                              preferred_element_type=F32) + b1_ref[...]))
    for j in range(HY_INNER):
        h = jnp.sin(fq * (jnp.dot(h, wi_ref[j], precision=hp,
                                  preferred_element_type=F32) + bi_ref[j]))
    filt = jnp.dot(h, wo_ref[...], precision=hp, preferred_element_type=F32)
    o_ref[...] = filt * jnp.exp(-offs_ref[...] * jnp.abs(decay_ref[...]))


def _hyena_filter(seq, hy_w, w1, b1, wi, bi, wo, freq, decay):
    zp, offs = _filter_features(seq)
    w1p = jnp.zeros((LANES, HY_FILTER_WIDTH), F32).at[:HY_EMB].set(w1.astype(F32))
    tr = 512
    fw = HY_FILTER_WIDTH
    const = lambda *shape: pl.BlockSpec(shape, lambda i: (0,) * len(shape))
    return pl.pallas_call(
        _filter_kernel,
        out_shape=jax.ShapeDtypeStruct((seq, hy_w), F32),
        grid=(seq // tr,),
        in_specs=[
            pl.BlockSpec((tr, LANES), lambda i: (i, 0)),
            pl.BlockSpec((tr, 1), lambda i: (i, 0)),
            const(LANES, fw), const(1, fw), const(HY_INNER, fw, fw),
            const(HY_INNER, 1, fw), const(fw, hy_w), const(1, fw), const(1, hy_w),
        ],
        out_specs=pl.BlockSpec((tr, hy_w), lambda i: (i, 0)),
        compiler_params=_cparams(("parallel",)),
        name="hyena_filter",
    )(jnp.asarray(zp), jnp.asarray(offs), w1p, b1.reshape(1, fw).astype(F32),
      wi.astype(F32), bi.reshape(HY_INNER, 1, fw).astype(F32), wo.astype(F32),
      freq.reshape(1, fw).astype(F32), decay.reshape(1, hy_w).astype(F32))


def _stack(c):
    return np.block([[c.real, -c.imag], [c.imag, c.real]])


def _fft_len(seq):
    return 3 * seq // 2


@functools.lru_cache(maxsize=None)
def _dft_constants(seq):
    n = _fft_len(seq)
    r = FFT_R
    a_n = n // r
    a_in = seq // r
    a_out = (seq // 2) // r
    ar = np.arange(a_n)
    br = np.arange(r)
    f1 = np.exp(-2j * np.pi * np.outer(ar, ar) / a_n)
    f1_fwd = _stack(f1[:, :a_in])
    f1_fwd_real = np.concatenate([f1[:, :a_in].real, f1[:, :a_in].imag], 0)
    f1_inv = _stack(np.conj(f1).T[a_out:a_out + a_in] / n)
    ph = (br[None, None, :] * br[None, :, None] / r
          + br[None, None, :] * ar[:, None, None] / n)
    g = np.exp(-2j * np.pi * ph)
    g_fwd = np.stack([_stack(g[k]) for k in range(a_n)])
    g_inv = np.stack([_stack(np.conj(g[k]).T) for k in range(a_n)])
    return (f1_fwd.astype(np.float32), f1_fwd_real.astype(np.float32),
            f1_inv.astype(np.float32), g_fwd.astype(np.float32), g_inv.astype(np.float32))


def _split_bf16(x):
    if isinstance(x, np.ndarray):
        hi = x.astype(BF16)
        lo = (x - hi.astype(np.float32)).astype(BF16)
        return jnp.asarray(hi), jnp.asarray(lo)
    hi = x.astype(BF16)
    return hi, (x - hi.astype(F32)).astype(BF16)


def _dot3(m_hi, m_lo, x):
    x_hi, x_lo = _split_bf16(x)
    dot = functools.partial(jnp.dot, preferred_element_type=F32)
    return dot(m_hi, x_hi) + (dot(m_hi, x_lo) + dot(m_lo, x_hi))


def _spectrum_kernel(filt_ref, f1h_ref, f1l_ref, gh_ref, gl_ref, hr_ref, hi_ref,
                     zs_ref, ts_ref, *, seq):
    r = FFT_R
    a_half = seq // r
    a_n = _fft_len(seq) // r
    for a in range(a_half):
        zs_ref[pl.ds(a * FFT_PITCH, r), :] = filt_ref[pl.ds(a * r, r), :]

    def stage1(bp, carry):
        rhs = jnp.concatenate(
            [zs_ref[pl.ds(2 * bp + e, a_half, stride=FFT_PITCH), :] for e in range(2)], axis=1)
        t = _dot3(f1h_ref[...], f1l_ref[...], rhs)
        for e in range(2):
            for c in range(2):
                ts_ref[c, pl.ds(2 * bp + e, a_n, stride=FFT_PITCH), :] = (
                    t[c * a_n:(c + 1) * a_n, e * LANES:(e + 1) * LANES])
        return carry

    lax.fori_loop(0, r // 2, stage1, 0, unroll=4)

    def stage2(ka, carry):
        rows = pl.ds(pl.multiple_of(ka * FFT_PITCH, 8), r)
        rhs = jnp.concatenate([ts_ref[0, rows, :], ts_ref[1, rows, :]], axis=0)
        s = _dot3(gh_ref[ka], gl_ref[ka], rhs)
        out = pl.ds(pl.multiple_of(ka * r, r), r)
        hr_ref[out, :] = s[:r]
        hi_ref[out, :] = s[r:]
        return carry

    lax.fori_loop(0, a_n, stage2, 0, unroll=8)


def _filter_spectrum(filt, seq):
    hy_w = filt.shape[1]
    n = _fft_len(seq)
    _, f1_real, _, g_fwd, _ = _dft_constants(seq)
    f1h, f1l = _split_bf16(f1_real)
    gh, gl = _split_bf16(g_fwd)
    a_n = n // FFT_R
    out = jax.ShapeDtypeStruct((n, hy_w), F32)
    gspec = pl.BlockSpec(g_fwd.shape, lambda c: (0, 0, 0), pipeline_mode=pl.Buffered(1))
    return pl.pallas_call(
        functools.partial(_spectrum_kernel, seq=seq),
        out_shape=(out, out),
        grid=(hy_w // LANES,),
        in_specs=[
            pl.BlockSpec((seq, LANES), lambda c: (0, c)),
            pl.BlockSpec(f1_real.shape, lambda c: (0, 0)),
            pl.BlockSpec(f1_real.shape, lambda c: (0, 0)),
            gspec, gspec,
        ],
        out_specs=(pl.BlockSpec((n, LANES), lambda c: (0, c)),
                   pl.BlockSpec((n, LANES), lambda c: (0, c))),
        scratch_shapes=[
            pltpu.VMEM((seq // FFT_R * FFT_PITCH, LANES), F32),
            pltpu.VMEM((2, a_n * FFT_PITCH, LANES), F32),
        ],
        compiler_params=_cparams(("parallel",)),
        name="filter_spectrum",
    )(filt, f1h, f1l, gh, gl)


def _hy_front_kernel(x0_ref, x1_ref, hv_ref, w0_ref, w1_ref, wv_ref, b0_ref, b1_ref, bv_ref,
                     z_ref, x0c_ref, pad_ref, *, seq):
    chunk = 512
    zero_row = jnp.zeros((8, LANES), F32)
    for idx, ref in enumerate((x0_ref, x1_ref, hv_ref)):
        pad_ref[idx, pl.ds(0, 8), :] = zero_row
        pad_ref[idx, pl.ds(8 + seq, 8), :] = zero_row
        pad_ref[idx, pl.ds(8, seq), :] = ref[...].astype(F32)

    def conv(idx, w_ref, b_ref, c):
        base = 8 + c * chunk
        w = w_ref[...]
        return (pad_ref[idx, pl.ds(base - 1, chunk), :] * w[0:1]
                + pad_ref[idx, pl.ds(base, chunk), :] * w[1:2]
                + pad_ref[idx, pl.ds(base + 1, chunk), :] * w[2:3]
                + b_ref[...])

    for c in range(seq // chunk):
        rows = pl.ds(c * chunk, chunk)
        x0c_ref[rows, :] = conv(0, w0_ref, b0_ref, c).astype(BF16)
        z_ref[rows, :] = (conv(2, wv_ref, bv_ref, c) * conv(1, w1_ref, b1_ref, c)).astype(BF16)


def _hyena_front(u_t, conv_w, conv_b, batch, seq, groups):
    ng = 3 * groups
    cw = conv_w.astype(F32).reshape(3, ng, LANES).transpose(1, 0, 2)
    cb = conv_b.astype(F32).reshape(ng, 1, LANES)
    uspec = lambda off: pl.BlockSpec((None, seq, LANES), lambda b, g: (off + g, b, 0))
    wspec = lambda off: pl.BlockSpec((None, 3, LANES), lambda b, g: (off + g, 0, 0))
    bspec = lambda off: pl.BlockSpec((None, 1, LANES), lambda b, g: (off + g, 0, 0))
    out = jax.ShapeDtypeStruct((groups, batch * seq, LANES), BF16)
    ospec = pl.BlockSpec((None, seq, LANES), lambda b, g: (g, b, 0))
    return pl.pallas_call(
        functools.partial(_hy_front_kernel, seq=seq),
        out_shape=(out, out),
        grid=(batch, groups),
        in_specs=[uspec(0), uspec(groups), uspec(2 * groups),
                  wspec(0), wspec(groups), wspec(2 * groups),
                  bspec(0), bspec(groups), bspec(2 * groups)],
        out_specs=(ospec, ospec),
        scratch_shapes=[pltpu.VMEM((3, seq + 16, LANES), F32)],
        compiler_params=_cparams(("parallel", "parallel")),
        name="hyena_front",
    )(u_t, u_t, u_t, cw, cw, cw, cb, cb, cb)


def _hy_conv_kernel(z_ref, x_ref, hr_ref, hi_ref, f1_ref, f1i_ref,
                    g_ref, gi_ref, bias_ref, gain_ref, o_ref,
                    zs_ref, ts_ref, ys_ref, *, seq):
    r = FFT_R
    a_half = seq // r
    a_n = _fft_len(seq) // r
    for c in range(2):
        for a in range(a_half):
            zs_ref[c, pl.ds(a * FFT_PITCH, r), :] = (
                z_ref[pl.ds(c * seq + a * r, r), :].astype(F32))

    def stage1(bp, carry):
        cols = []
        for e in range(2):
            rows = pl.ds(2 * bp + e, a_half, stride=FFT_PITCH)
            cols.append(jnp.concatenate([zs_ref[0, rows, :], zs_ref[1, rows, :]], axis=0))
        rhs = jnp.concatenate(cols, axis=1).astype(BF16)
        t = jnp.dot(f1_ref[...], rhs, preferred_element_type=F32)
        for e in range(2):
            for c in range(2):
                ts_ref[c, pl.ds(2 * bp + e, a_n, stride=FFT_PITCH), :] = (
                    t[c * a_n:(c + 1) * a_n, e * LANES:(e + 1) * LANES])
        return carry

    lax.fori_loop(0, r // 2, stage1, 0, unroll=8)

    def stage2(ka, carry):
        rows = pl.ds(pl.multiple_of(ka * FFT_PITCH, 8), r)
        rhs = jnp.concatenate([ts_ref[0, rows, :], ts_ref[1, rows, :]], axis=0).astype(BF16)
        s = jnp.dot(g_ref[ka], rhs, preferred_element_type=F32)
        hrows = pl.ds(pl.multiple_of(ka * r, r), r)
        hr = hr_ref[hrows, :]
        hi = hi_ref[hrows, :]
        sr, si = s[:r], s[r:]
        prod = jnp.concatenate([sr * hr - si * hi, sr * hi + si * hr], axis=0).astype(BF16)
        u = jnp.dot(gi_ref[ka], prod, preferred_element_type=F32)
        ts_ref[0, rows, :] = u[:r]
        ts_ref[1, rows, :] = u[r:]
        return carry

    lax.fori_loop(0, a_n, stage2, 0, unroll=32)

    def stage3(bp, carry):
        cols = []
        for e in range(2):
            rows = pl.ds(2 * bp + e, a_n, stride=FFT_PITCH)
            cols.append(jnp.concatenate([ts_ref[0, rows, :], ts_ref[1, rows, :]], axis=0))
        rhs = jnp.concatenate(cols, axis=1).astype(BF16)
        y = jnp.dot(f1i_ref[...], rhs, preferred_element_type=F32)
        for e in range(2):
            for c in range(2):
                ys_ref[c, pl.ds(2 * bp + e, a_half, stride=FFT_PITCH), :] = (
                    y[c * a_half:(c + 1) * a_half, e * LANES:(e + 1) * LANES])
        return carry

    lax.fori_loop(0, r // 2, stage3, 0, unroll=8)

    def finish(a, carry):
        prow = pl.ds(pl.multiple_of(a * FFT_PITCH, 8), r)
        for c in range(2):
            orow = pl.ds(pl.multiple_of(c * seq + a * r, r), r)
            z = ys_ref[c, prow, :] + zs_ref[c, prow, :] * bias_ref[...]
            y = z * x_ref[orow, :].astype(F32)
            o_ref[orow, :] = _rms(y, gain_ref[...]).astype(BF16)
        return carry

    lax.fori_loop(0, a_half, finish, 0, unroll=8)


def _hyena_conv(z_t, x0c_t, h_re, h_im, hy_bias, hy_out_norm, batch, seq, groups):
    n = _fft_len(seq)
    f1_fwd, _, f1_inv, g_fwd, g_inv = _dft_constants(seq)
    a_n = n // FFT_R
    zspec = pl.BlockSpec((None, 2 * seq, LANES), lambda g, p: (g, p, 0))
    hspec = pl.BlockSpec((n, LANES), lambda g, p: (0, g), pipeline_mode=pl.Buffered(1))
    cspec = lambda arr: pl.BlockSpec(arr.shape, lambda g, p: (0,) * arr.ndim,
                                     pipeline_mode=pl.Buffered(1))
    vspec = pl.BlockSpec((None, 1, LANES), lambda g, p: (g, 0, 0))
    out = jax.ShapeDtypeStruct((groups, batch * seq, LANES), BF16)
    return pl.pallas_call(
        functools.partial(_hy_conv_kernel, seq=seq),
        out_shape=out,
        grid=(groups, batch // 2),
        in_specs=[zspec, zspec, hspec, hspec,
                  cspec(f1_fwd), cspec(f1_inv), cspec(g_fwd), cspec(g_inv), vspec, vspec],
        out_specs=zspec,
        scratch_shapes=[
            pltpu.VMEM((2, seq // FFT_R * FFT_PITCH, LANES), F32),
            pltpu.VMEM((2, a_n * FFT_PITCH, LANES), F32),
            pltpu.VMEM((2, seq // FFT_R * FFT_PITCH, LANES), F32),
        ],
        compiler_params=_cparams(("parallel", "arbitrary")),
        name="hyena_conv",
    )(z_t, x0c_t, h_re, h_im,
      jnp.asarray(f1_fwd, BF16), jnp.asarray(f1_inv, BF16),
      jnp.asarray(g_fwd, BF16), jnp.asarray(g_inv, BF16),
      hy_bias.astype(F32).reshape(groups, 1, LANES),
      hy_out_norm.astype(F32).reshape(groups, 1, LANES))


def _out_proj_kernel(ya_ref, yh_ref, x_ref, w_ref, o_ref, *, heads, groups):
    lhs = jnp.concatenate([ya_ref[g] for g in range(heads)]
                          + [yh_ref[g] for g in range(groups)], axis=-1)
    o_ref[...] = x_ref[...] + jnp.dot(lhs, w_ref[...], preferred_element_type=F32)


def _out_proj(ya_t, yh_t, x2, w_out, tm):
    m, d = x2.shape
    heads, groups = ya_t.shape[0], yh_t.shape[0]
    return pl.pallas_call(
        functools.partial(_out_proj_kernel, heads=heads, groups=groups),
        out_shape=jax.ShapeDtypeStruct((m, d), F32),
        grid=(m // tm,),
        in_specs=[
            pl.BlockSpec((heads, tm, LANES), lambda i: (0, i, 0)),
            pl.BlockSpec((groups, tm, LANES), lambda i: (0, i, 0)),
            pl.BlockSpec((tm, d), lambda i: (i, 0)),
            pl.BlockSpec(w_out.shape, lambda i: (0, 0)),
        ],
        out_specs=pl.BlockSpec((tm, d), lambda i: (i, 0)),
        compiler_params=_cparams(("parallel",)),
        name="out_proj",
    )(ya_t, yh_t, x2, w_out.astype(BF16))


def _ffn_kernel(h_ref, g_ref, wa_ref, wg_ref, wd_ref, o_ref, xn_ref):
    @pl.when(pl.program_id(1) == 0)
    def _():
        h = h_ref[...]
        xn_ref[...] = _rms(h, g_ref[...]).astype(BF16)
        o_ref[...] = h

    xn = xn_ref[...]
    a = jnp.dot(xn, wa_ref[...], preferred_element_type=F32)
    g = jnp.dot(xn, wg_ref[...], preferred_element_type=F32)
    act = (a * jax.nn.sigmoid(a) * g).astype(BF16)
    o_ref[...] += jnp.dot(act, wd_ref[...], preferred_element_type=F32)


def _ffn(h1, norm2, w_gu, w_down, tm, th):
    m, d = h1.shape
    hidden = w_down.shape[0]
    nh = hidden // th
    w_gu = w_gu.astype(BF16)
    return pl.pallas_call(
        _ffn_kernel,
        out_shape=jax.ShapeDtypeStruct((m, d), F32),
        grid=(m // tm, nh),
        in_specs=[
            pl.BlockSpec((tm, d), lambda i, j: (i, 0)),
            pl.BlockSpec((1, d), lambda i, j: (0, 0)),
            pl.BlockSpec((d, th), lambda i, j: (0, j)),
            pl.BlockSpec((d, th), lambda i, j: (0, nh + j)),
            pl.BlockSpec((th, d), lambda i, j: (j, 0)),
        ],
        out_specs=pl.BlockSpec((tm, d), lambda i, j: (i, 0)),
        scratch_shapes=[pltpu.VMEM((tm, d), BF16)],
        compiler_params=_cparams(("parallel", "arbitrary")),
        name="ffn",
    )(h1, norm2.reshape(1, d), w_gu, w_gu, w_down.astype(BF16))


def _ple_kernel(h_ref, p_ref, gn_ref, wg_ref, wp_ref, pn_ref, o_ref):
    h = h_ref[...]
    e = _rms(jnp.dot(p_ref[...].astype(BF16), wp_ref[...], preferred_element_type=F32),
             pn_ref[...])
    inv = lax.rsqrt(jnp.mean(h * h, axis=-1, keepdims=True) + EPS)
    logits = jnp.dot((h * gn_ref[...]).astype(BF16), wg_ref[...], preferred_element_type=F32)
    o_ref[...] = h + jax.nn.sigmoid(logits * inv) * e


def _ple(h2, p2, ple_norm, w_gate, w_proj, ple_post_norm, tm):
    m, d = h2.shape
    pd = p2.shape[1]
    return pl.pallas_call(
        _ple_kernel,
        out_shape=jax.ShapeDtypeStruct((m, d), F32),
        grid=(m // tm,),
        in_specs=[
            pl.BlockSpec((tm, d), lambda i: (i, 0)),
            pl.BlockSpec((tm, pd), lambda i: (i, 0)),
            pl.BlockSpec((1, d), lambda i: (0, 0)),
            pl.BlockSpec((d, d), lambda i: (0, 0)),
            pl.BlockSpec((pd, d), lambda i: (0, 0)),
            pl.BlockSpec((1, d), lambda i: (0, 0)),
        ],
        out_specs=pl.BlockSpec((tm, d), lambda i: (i, 0)),
        compiler_params=_cparams(("parallel",)),
        name="ple",
    )(h2, p2, ple_norm.reshape(1, d), w_gate.astype(BF16), w_proj.astype(BF16),
      ple_post_norm.reshape(1, d))


def kernel(x, p, rel_bias, norm1, w_in, q_norm, k_norm, conv_w, conv_b, hy_w1, hy_b1, hy_wi, hy_bi, hy_wo, hy_freq, hy_decay, hy_bias, attn_out_norm, hy_out_norm, w_out, norm2, w_gu, w_down, ple_norm, w_ple_gate, w_ple_proj, ple_post_norm):
    batch, seq, d = x.shape
    attn_w = d // 2
    heads = attn_w // HEAD_DIM
    groups = heads
    hy_w = groups * LANES
    m = batch * seq
    tm = 512
    tm_big = min(1024, m)
    h = x.reshape(m, d)
    bias_tab = _attn_bias_tables(rel_bias)
    for i in range(norm1.shape[0]):
        h3 = h.reshape(batch, seq, d)
        u_p, u_t = _in_proj(h3, norm1[i], w_in[i], q_norm[i], k_norm[i], attn_w, tm_big)
        ya_t = _attention(u_p, bias_tab, attn_out_norm[i], batch, seq, heads)
        filt = _hyena_filter(seq, hy_w, hy_w1[i], hy_b1[i], hy_wi[i], hy_bi[i], hy_wo[i],
                             hy_freq[i], hy_decay[i])
        h_re, h_im = _filter_spectrum(filt, seq)
        z_t, x0c_t = _hyena_front(u_t, conv_w[i], conv_b[i], batch, seq, groups)
        yh_t = _hyena_conv(z_t, x0c_t, h_re, h_im, hy_bias[i], hy_out_norm[i],
                           batch, seq, groups)
        h = _out_proj(ya_t, yh_t, h, w_out[i], tm)
        h = _ffn(h, norm2[i], w_gu[i], w_down[i], tm_big, 512)
        h = _ple(h, p[i].reshape(m, PLE_DIM), ple_norm[i], w_ple_gate[i], w_ple_proj[i],
                 ple_post_norm[i], tm)
    return h.reshape(batch, seq, d)
```

```python
import functools
import math

import jax
import jax.numpy as jnp
import numpy as np
from jax import lax
from jax.experimental import pallas as pl
from jax.experimental.pallas import tpu as pltpu

F32 = jnp.float32
BF16 = jnp.bfloat16

LANES = 128
HEAD_DIM = 128
ATTN_PATTERNS = ((128, 1), (512, 4), (2048, 16))
PERM = 16
BAND = 64
QBLK = 128
KBLK = QBLK + 2 * BAND
N_BUCKETS = 32
REL_MAX_DIST = 1024
HY_EMB = 33
HY_FILTER_WIDTH = 64
HY_INNER = 2
EPS = 1e-6
NEG = -1e30
PLE_DIM = 256

FFT_R = 64
FFT_PITCH = FFT_R + 8
VMEM_LIMIT = 56 * 1024 * 1024


def _cparams(sem, vmem=VMEM_LIMIT):
    return pltpu.CompilerParams(dimension_semantics=sem, vmem_limit_bytes=vmem)


def _rms(x, gain):
    ms = jnp.mean(x * x, axis=-1, keepdims=True)
    return x * lax.rsqrt(ms + EPS) * gain


def _in_proj_kernel(x_ref, g_ref, w_ref, qg_ref, kg_ref, perm_ref, oq_ref, oh_ref,
                    xn_ref, xp_ref, *, nq, hpt):
    j = pl.program_id(1)
    tm = x_ref.shape[0]
    chunk = PERM * PERM

    is_q = j < nq
    is_k = (j >= nq) & (j < 2 * nq)

    def qkv_rows(c, xp):
        acc = jnp.dot(xp, w_ref[...], preferred_element_type=F32)
        gain = jnp.where(is_q, qg_ref[...] * (HEAD_DIM ** -0.5),
                         jnp.where(is_k, kg_ref[...], 1.0))
        for hh in range(hpt):
            a = acc[:, hh * LANES:(hh + 1) * LANES]
            inv = lax.rsqrt(jnp.mean(a * a, axis=-1, keepdims=True) + EPS)
            a = (a * jnp.where(is_q | is_k, inv, 1.0) * gain).astype(BF16)
            for r in range(PERM):
                oq_ref[hh, r, pl.ds(c * PERM, PERM), :] = a[r * PERM:(r + 1) * PERM]

    @pl.when(j == 0)
    def _():
        for c in range(tm // chunk):
            rows = pl.ds(c * chunk, chunk)
            xn = _rms(x_ref[rows, :], g_ref[...]).astype(BF16)
            xp = jnp.dot(perm_ref[...], xn, preferred_element_type=F32).astype(BF16)
            xn_ref[rows, :] = xn
            xp_ref[rows, :] = xp
            qkv_rows(c, xp)

    @pl.when((j > 0) & (j < 3 * nq))
    def _():
        for c in range(tm // chunk):
            qkv_rows(c, xp_ref[pl.ds(c * chunk, chunk), :])

    @pl.when(j >= 3 * nq)
    def _():
        acc = jnp.dot(xn_ref[...], w_ref[...], preferred_element_type=F32)
        for hh in range(hpt):
            oh_ref[hh] = acc[:, hh * LANES:(hh + 1) * LANES].astype(BF16)


def _in_proj(x3, norm1, w_in, q_norm, k_norm, attn_w, tm):
    batch, seq, d = x3.shape
    m = batch * seq
    in_w = w_in.shape[1]
    tn = min(1024, attn_w)
    hpt = tn // LANES
    nq = attn_w // tn
    n_qkv = 3 * nq
    n_tiles = in_w // tn
    tiles_per_seq = seq // tm
    assert tm % (PERM * PERM) == 0 and seq % tm == 0
    perm = jnp.asarray(_to_sequence_order_matrix().T, BF16)
    out_q = jax.ShapeDtypeStruct((n_qkv * hpt, batch, PERM, seq // PERM, LANES), BF16)
    out_h = jax.ShapeDtypeStruct(((n_tiles - n_qkv) * hpt, m, LANES), BF16)
    q_spec = pl.BlockSpec(
        (hpt, None, PERM, tm // PERM, LANES),
        lambda i, j: (jnp.minimum(j, n_qkv - 1), i // tiles_per_seq, 0, i % tiles_per_seq, 0))
    h_spec = pl.BlockSpec((hpt, tm, LANES), lambda i, j: (jnp.maximum(j - n_qkv, 0), i, 0))
    u_q, u_h = pl.pallas_call(
        functools.partial(_in_proj_kernel, nq=nq, hpt=hpt),
        out_shape=(out_q, out_h),
        grid=(m // tm, n_tiles),
        in_specs=[
            pl.BlockSpec((tm, d), lambda i, j: (i, 0)),
            pl.BlockSpec((1, d), lambda i, j: (0, 0)),
            pl.BlockSpec((d, tn), lambda i, j: (0, j)),
            pl.BlockSpec((1, LANES), lambda i, j: (0, 0)),
            pl.BlockSpec((1, LANES), lambda i, j: (0, 0)),
            pl.BlockSpec(perm.shape, lambda i, j: (0, 0)),
        ],
        out_specs=(q_spec, h_spec),
        scratch_shapes=[pltpu.VMEM((tm, d), BF16), pltpu.VMEM((tm, d), BF16)],
        compiler_params=_cparams(("parallel", "arbitrary")),
        name="in_proj",
    )(x3.reshape(m, d), norm1.reshape(1, d), w_in.astype(BF16), q_norm.reshape(1, LANES),
      k_norm.reshape(1, LANES), perm)
    return u_q.reshape(n_qkv * hpt, m, LANES), u_h


def _t5_bucket(rel):
    half = N_BUCKETS // 2
    exact = half // 2
    n = np.abs(rel)
    large = exact + (np.log(np.maximum(n, 1).astype(np.float32) / np.float32(exact))
                     / np.float32(math.log(REL_MAX_DIST / exact))
                     * np.float32(half - exact)).astype(np.int32)
    large = np.minimum(large, half - 1)
    return np.where(rel > 0, half, 0) + np.where(n < exact, n, large)


def _block_orders(dil):
    sub = PERM // dil
    e, i = np.divmod(np.arange(QBLK), QBLK // sub)
    q_off = sub * i + e
    if dil == 1:
        k_off = np.arange(KBLK) - BAND
    else:
        e, i = np.divmod(np.arange(KBLK), KBLK // sub)
        k_off = sub * i + e - BAND
    return q_off, k_off


def _to_sequence_order_matrix():
    n = PERM * PERM
    mat = np.zeros((n, n), np.float32)
    r, i = np.divmod(np.arange(n), PERM)
    mat[PERM * i + r, np.arange(n)] = 1.0
    return mat


def _attn_bias_tables(rel_bias):
    tabs = []
    for _, dil in ATTN_PATTERNS:
        q_off, k_off = _block_orders(dil)
        rel = k_off[None, :] - q_off[:, None]
        band = np.abs(rel) <= BAND
        first_ok = np.broadcast_to(k_off[None, :] >= 0, rel.shape)
        last_ok = np.broadcast_to(k_off[None, :] < QBLK, rel.shape)
        onehot = np.eye(N_BUCKETS, dtype=np.float32)[_t5_bucket(rel * dil)]
        bias = jnp.einsum("qkn,nh->hqk", jnp.asarray(onehot), rel_bias.astype(F32),
                          precision=lax.Precision.HIGHEST)
        variants = [jnp.where(jnp.asarray(band & ok), bias, NEG)
                    for ok in (first_ok, np.ones_like(band), last_ok)]
        tabs.append(jnp.stack(variants, axis=1))
    return jnp.stack(tabs, axis=1)


def _attn_kernel(q_ref, k_ref, v_ref, bias_ref, gain_ref, perm_ref, o_ref,
                 qf_ref, kp_ref, vp_ref, kn_ref, vn_ref, acc_ref, m_ref, *, seq):
    run = seq // PERM
    chunk = PERM * PERM
    ones = jnp.ones((KBLK, LANES), BF16)
    zpad = jnp.zeros((BAND, LANES), BF16)

    for ref in (kp_ref, vp_ref, kn_ref, vn_ref):
        ref[pl.ds(0, BAND), :] = zpad
        ref[pl.ds(BAND + seq, BAND), :] = zpad
    kp_ref[pl.ds(BAND, seq), :] = k_ref[...]
    vp_ref[pl.ds(BAND, seq), :] = v_ref[...]
    qf_ref[...] = q_ref[...].astype(F32)

    def runs(ref, base, n):
        return jnp.concatenate(
            [ref[pl.ds(pl.multiple_of(r * run + base, n), n), :] for r in range(PERM)], axis=0)

    def to_sequence_order(c, carry):
        base = c * PERM
        kv = jnp.concatenate([runs(k_ref, base, PERM), runs(v_ref, base, PERM)], axis=1)
        nat = jnp.dot(perm_ref[...], kv, preferred_element_type=F32).astype(BF16)
        rows = pl.ds(pl.multiple_of(BAND + c * chunk, BAND), chunk)
        kn_ref[rows, :] = nat[:, :LANES]
        vn_ref[rows, :] = nat[:, LANES:]
        return carry

    lax.fori_loop(0, seq // chunk, to_sequence_order, 0, unroll=4)

    def softmax_block(q, k, v, bias):
        s = lax.dot_general(q, k, (((1,), (1,)), ((), ())), preferred_element_type=F32) + bias
        m_blk = jnp.max(s, axis=-1, keepdims=True)
        p = jnp.exp(s - m_blk).astype(BF16)
        a_l = jnp.dot(p, jnp.concatenate([v, ones], axis=1), preferred_element_type=F32)
        return a_l, jnp.broadcast_to(m_blk, (QBLK, LANES))

    def variant(qb, nb):
        return jnp.where(qb == 0, 0, jnp.where(qb == nb - 1, 2, 1))

    def merge(row_slices, a_l, m_b):
        m_old = jnp.concatenate([m_ref[rows, :] for rows in row_slices], axis=0)
        a_old = jnp.concatenate([acc_ref[rows, :] for rows in row_slices], axis=0)
        m_new = jnp.maximum(m_old, m_b)
        w_old = jnp.exp(m_old - m_new)
        w_blk = jnp.exp(m_b - m_new)
        a_new = (a_old * jnp.concatenate([w_old, w_old], axis=1)
                 + a_l * jnp.concatenate([w_blk, w_blk], axis=1))
        n = QBLK // len(row_slices)
        for e, rows in enumerate(row_slices):
            acc_ref[rows, :] = a_new[e * n:(e + 1) * n]
            m_ref[rows, :] = m_new[e * n:(e + 1) * n]

    nb16 = run // QBLK

    def block16(t, carry):
        r, qb = t // nb16, t % nb16
        off = pl.multiple_of(r * run + qb * QBLK, QBLK)
        a_l, m_b = softmax_block(q_ref[pl.ds(off, QBLK), :], kp_ref[pl.ds(off, KBLK), :],
                                 vp_ref[pl.ds(off, KBLK), :], bias_ref[2, variant(qb, nb16)])
        acc_ref[pl.ds(off, QBLK), :] = a_l
        m_ref[pl.ds(off, QBLK), :] = m_b
        return carry

    lax.fori_loop(0, PERM * nb16, block16, 0, unroll=16)

    dil = ATTN_PATTERNS[1][1]
    sub = PERM // dil
    nb4 = (seq // dil) // QBLK
    qn, kn = QBLK // sub, KBLK // sub

    def block4(t, carry):
        r, qb = t // nb4, t % nb4
        q_rows = [pl.ds(pl.multiple_of((dil * e + r) * run + qb * qn, qn), qn)
                  for e in range(sub)]
        k_rows = [pl.ds(pl.multiple_of((dil * e + r) * run + qb * qn + BAND - BAND // sub,
                                       BAND // sub), kn) for e in range(sub)]
        q = jnp.concatenate([q_ref[rows, :] for rows in q_rows], axis=0)
        k = jnp.concatenate([kp_ref[rows, :] for rows in k_rows], axis=0)
        v = jnp.concatenate([vp_ref[rows, :] for rows in k_rows], axis=0)
        a_l, m_b = softmax_block(q, k, v, bias_ref[1, variant(qb, nb4)])
        merge(q_rows, a_l, m_b)
        return carry

    lax.fori_loop(0, dil * nb4, block4, 0, unroll=16)

    nb1 = seq // QBLK
    qn1 = QBLK // PERM

    def block1(qb, carry):
        q_rows = [pl.ds(pl.multiple_of(r * run + qb * qn1, qn1), qn1) for r in range(PERM)]
        q = jnp.concatenate([qf_ref[rows, :] for rows in q_rows], axis=0).astype(BF16)
        k_rows = pl.ds(pl.multiple_of(qb * QBLK, QBLK), KBLK)
        a_l, m_b = softmax_block(q, kn_ref[k_rows, :], vn_ref[k_rows, :],
                                 bias_ref[0, variant(qb, nb1)])
        merge(q_rows, a_l, m_b)
        return carry

    lax.fori_loop(0, nb1, block1, 0, unroll=16)

    def finish(c, carry):
        st = runs(acc_ref, c * PERM, PERM)
        y = _rms(st[:, :LANES] / st[:, LANES:], gain_ref[...]).astype(BF16)
        nat = jnp.dot(perm_ref[...], y, preferred_element_type=F32)
        o_ref[pl.ds(pl.multiple_of(c * chunk, chunk), chunk), :] = nat.astype(BF16)
        return carry

    lax.fori_loop(0, seq // chunk, finish, 0, unroll=4)


def _attention(u_p, bias_tab, attn_out_norm, batch, seq, heads):
    assert ATTN_PATTERNS[0][1] == 1 and ATTN_PATTERNS[2][1] == PERM
    assert seq % (PERM * QBLK) == 0
    perm = jnp.asarray(_to_sequence_order_matrix(), BF16)
    pad_rows = seq + 2 * BAND
    return pl.pallas_call(
        functools.partial(_attn_kernel, seq=seq),
        out_shape=jax.ShapeDtypeStruct((heads, batch * seq, LANES), BF16),
        grid=(batch, heads),
        in_specs=[
            pl.BlockSpec((None, seq, LANES), lambda b, h: (h, b, 0)),
            pl.BlockSpec((None, seq, LANES), lambda b, h: (heads + h, b, 0)),
            pl.BlockSpec((None, seq, LANES), lambda b, h: (2 * heads + h, b, 0)),
            pl.BlockSpec((None, len(ATTN_PATTERNS), 3, QBLK, KBLK),
                         lambda b, h: (h, 0, 0, 0, 0)),
            pl.BlockSpec((None, 1, LANES), lambda b, h: (h, 0, 0)),
            pl.BlockSpec(perm.shape, lambda b, h: (0, 0)),
        ],
        out_specs=pl.BlockSpec((None, seq, LANES), lambda b, h: (h, b, 0)),
        scratch_shapes=[
            pltpu.VMEM((seq, LANES), F32),
            pltpu.VMEM((pad_rows, LANES), BF16),
            pltpu.VMEM((pad_rows, LANES), BF16),
            pltpu.VMEM((pad_rows, LANES), BF16),
            pltpu.VMEM((pad_rows, LANES), BF16),
            pltpu.VMEM((seq, 2 * LANES), F32),
            pltpu.VMEM((seq, LANES), F32),
        ],
        compiler_params=_cparams(("parallel", "parallel")),
        name="attention",
    )(u_p, u_p, u_p, bias_tab, attn_out_norm.reshape(heads, 1, LANES), perm)


def _filter_features(seq):
    pos = np.arange(seq, dtype=np.float32)
    t = pos / np.float32(max(seq - 1, 1))
    bands = (HY_EMB - 1) // 2
    fr = np.linspace(1e-4, bands - 1, bands, dtype=np.float32)
    ang = np.float32(2.0 * math.pi / seq) * pos[:, None] * fr[None, :]
    z = np.concatenate([t[:, None], np.cos(ang), -np.sin(ang)], axis=-1).astype(np.float32)
    zp = np.zeros((seq, LANES), np.float32)
    zp[:, :HY_EMB] = z
    offs = (np.abs(pos - (seq // 2)) / np.float32(seq / 2)).astype(np.float32)
    return zp, offs[:, None]


def _filter_kernel(z_ref, offs_ref, w1_ref, b1_ref, wi_ref, bi_ref, wo_ref, fq_ref,
                   decay_ref, o_ref):
    hp = lax.Precision.HIGHEST
    fq = fq_ref[...]
    h = jnp.sin(fq * (jnp.dot(z_ref[...], w1_ref[...], precision=hp,
                              preferred_element_type=F32) + b1_ref[...]))
    for j in range(HY_INNER):
        h = jnp.sin(fq * (jnp.dot(h, wi_ref[j], precision=hp,
                                  preferred_element_type=F32) + bi_ref[j]))
    h_hi, h_lo = _split_bf16(h)
    w_hi, w_lo = _split_bf16(wo_ref[...])
    dot = functools.partial(jnp.dot, preferred_element_type=F32)
    filt = dot(h_hi, w_hi) + (dot(h_hi, w_lo) + dot(h_lo, w_hi))
    o_ref[...] = filt * jnp.exp(-offs_ref[...] * jnp.abs(decay_ref[...]))


def _hyena_filter(seq, hy_w, w1, b1, wi, bi, wo, freq, decay):
    zp, offs = _filter_features(seq)
    w1p = jnp.zeros((LANES, HY_FILTER_WIDTH), F32).at[:HY_EMB].set(w1.astype(F32))
    tr = 512
    fw = HY_FILTER_WIDTH
    const = lambda *shape: pl.BlockSpec(shape, lambda i: (0,) * len(shape))
    return pl.pallas_call(
        _filter_kernel,
        out_shape=jax.ShapeDtypeStruct((seq, hy_w), F32),
        grid=(seq // tr,),
        in_specs=[
            pl.BlockSpec((tr, LANES), lambda i: (i, 0)),
            pl.BlockSpec((tr, 1), lambda i: (i, 0)),
            const(LANES, fw), const(1, fw), const(HY_INNER, fw, fw),
            const(HY_INNER, 1, fw), const(fw, hy_w), const(1, fw), const(1, hy_w),
        ],
        out_specs=pl.BlockSpec((tr, hy_w), lambda i: (i, 0)),
        compiler_params=_cparams(("parallel",)),
        name="hyena_filter",
    )(jnp.asarray(zp), jnp.asarray(offs), w1p, b1.reshape(1, fw).astype(F32),
      wi.astype(F32), bi.reshape(HY_INNER, 1, fw).astype(F32), wo.astype(F32),
      freq.reshape(1, fw).astype(F32), decay.reshape(1, hy_w).astype(F32))


def _stack(c):
    return np.block([[c.real, -c.imag], [c.imag, c.real]])


def _fft_len(seq):
    return 3 * seq // 2


@functools.lru_cache(maxsize=None)
def _dft_constants(seq):
    n = _fft_len(seq)
    r = FFT_R
    a_n = n // r
    a_in = seq // r
    a_out = (seq // 2) // r
    ar = np.arange(a_n)
    br = np.arange(r)
    f1 = np.exp(-2j * np.pi * np.outer(ar, ar) / a_n)
    f1_fwd = _stack(f1[:, :a_in])
    f1_fwd_real = np.concatenate([f1[:, :a_in].real, f1[:, :a_in].imag], 0)
    f1_inv = _stack(np.conj(f1).T[a_out:a_out + a_in] / n)
    ph = (br[None, None, :] * br[None, :, None] / r
          + br[None, None, :] * ar[:, None, None] / n)
    g = np.exp(-2j * np.pi * ph)
    g_fwd = np.stack([_stack(g[k]) for k in range(a_n)])
    g_inv = np.stack([_stack(np.conj(g[k]).T) for k in range(a_n)])
    return (f1_fwd.astype(np.float32), f1_fwd_real.astype(np.float32),
            f1_inv.astype(np.float32), g_fwd.astype(np.float32), g_inv.astype(np.float32))


def _split_bf16(x):
    if isinstance(x, np.ndarray):
        hi = x.astype(BF16)
        lo = (x - hi.astype(np.float32)).astype(BF16)
        return jnp.asarray(hi), jnp.asarray(lo)
    hi = x.astype(BF16)
    return hi, (x - hi.astype(F32)).astype(BF16)


def _dot3(m_hi, m_lo, x):
    x_hi, x_lo = _split_bf16(x)
    dot = functools.partial(jnp.dot, preferred_element_type=F32)
    return dot(m_hi, x_hi) + (dot(m_hi, x_lo) + dot(m_lo, x_hi))


def _spectrum_kernel(filt_ref, f1h_ref, f1l_ref, gh_ref, gl_ref, hr_ref, hi_ref,
                     zs_ref, ts_ref, *, seq):
    r = FFT_R
    a_half = seq // r
    a_n = _fft_len(seq) // r
    for a in range(a_half):
        zs_ref[pl.ds(a * FFT_PITCH, r), :] = filt_ref[pl.ds(a * r, r), :]

    def stage1(bp, carry):
        rhs = jnp.concatenate(
            [zs_ref[pl.ds(2 * bp + e, a_half, stride=FFT_PITCH), :] for e in range(2)], axis=1)
        t = _dot3(f1h_ref[...], f1l_ref[...], rhs)
        for e in range(2):
            for c in range(2):
                ts_ref[c, pl.ds(2 * bp + e, a_n, stride=FFT_PITCH), :] = (
                    t[c * a_n:(c + 1) * a_n, e * LANES:(e + 1) * LANES])
        return carry

    lax.fori_loop(0, r // 2, stage1, 0, unroll=4)

    def stage2(ka, carry):
        rows = pl.ds(pl.multiple_of(ka * FFT_PITCH, 8), r)
        rhs = jnp.concatenate([ts_ref[0, rows, :], ts_ref[1, rows, :]], axis=0)
        s = _dot3(gh_ref[ka], gl_ref[ka], rhs)
        out = pl.ds(pl.multiple_of(ka * r, r), r)
        hr_ref[out, :] = s[:r]
        hi_ref[out, :] = s[r:]
        return carry

    lax.fori_loop(0, a_n, stage2, 0, unroll=8)


def _filter_spectrum(filt, seq):
    hy_w = filt.shape[1]
    n = _fft_len(seq)
    _, f1_real, _, g_fwd, _ = _dft_constants(seq)
    f1h, f1l = _split_bf16(f1_real)
    gh, gl = _split_bf16(g_fwd)
    a_n = n // FFT_R
    out = jax.ShapeDtypeStruct((n, hy_w), F32)
    gspec = pl.BlockSpec(g_fwd.shape, lambda c: (0, 0, 0), pipeline_mode=pl.Buffered(1))
    return pl.pallas_call(
        functools.partial(_spectrum_kernel, seq=seq),
        out_shape=(out, out),
        grid=(hy_w // LANES,),
        in_specs=[
            pl.BlockSpec((seq, LANES), lambda c: (0, c)),
            pl.BlockSpec(f1_real.shape, lambda c: (0, 0)),
            pl.BlockSpec(f1_real.shape, lambda c: (0, 0)),
            gspec, gspec,
        ],
        out_specs=(pl.BlockSpec((n, LANES), lambda c: (0, c)),
                   pl.BlockSpec((n, LANES), lambda c: (0, c))),
        scratch_shapes=[
            pltpu.VMEM((seq // FFT_R * FFT_PITCH, LANES), F32),
            pltpu.VMEM((2, a_n * FFT_PITCH, LANES), F32),
        ],
        compiler_params=_cparams(("parallel",)),
        name="filter_spectrum",
    )(filt, f1h, f1l, gh, gl)


def _hy_front_kernel(x0_ref, x1_ref, hv_ref, w0_ref, w1_ref, wv_ref, b0_ref, b1_ref, bv_ref,
                     z_ref, x0c_ref, pad_ref, *, seq):
    chunk = 512
    zero_row = jnp.zeros((8, LANES), F32)
    for idx, ref in enumerate((x0_ref, x1_ref, hv_ref)):
        pad_ref[idx, pl.ds(0, 8), :] = zero_row
        pad_ref[idx, pl.ds(8 + seq, 8), :] = zero_row
        pad_ref[idx, pl.ds(8, seq), :] = ref[...].astype(F32)

    def conv(idx, w_ref, b_ref, c):
        base = 8 + c * chunk
        w = w_ref[...]
        return (pad_ref[idx, pl.ds(base - 1, chunk), :] * w[0:1]
                + pad_ref[idx, pl.ds(base, chunk), :] * w[1:2]
                + pad_ref[idx, pl.ds(base + 1, chunk), :] * w[2:3]
                + b_ref[...])

    for c in range(seq // chunk):
        rows = pl.ds(c * chunk, chunk)
        x0c_ref[rows, :] = conv(0, w0_ref, b0_ref, c).astype(BF16)
        z_ref[rows, :] = (conv(2, wv_ref, bv_ref, c) * conv(1, w1_ref, b1_ref, c)).astype(BF16)


def _hyena_front(u_t, conv_w, conv_b, batch, seq, groups):
    ng = 3 * groups
    cw = conv_w.astype(F32).reshape(3, ng, LANES).transpose(1, 0, 2)
    cb = conv_b.astype(F32).reshape(ng, 1, LANES)
    uspec = lambda off: pl.BlockSpec((None, seq, LANES), lambda b, g: (off + g, b, 0))
    wspec = lambda off: pl.BlockSpec((None, 3, LANES), lambda b, g: (off + g, 0, 0))
    bspec = lambda off: pl.BlockSpec((None, 1, LANES), lambda b, g: (off + g, 0, 0))
    out = jax.ShapeDtypeStruct((groups, batch * seq, LANES), BF16)
    ospec = pl.BlockSpec((None, seq, LANES), lambda b, g: (g, b, 0))
    return pl.pallas_call(
        functools.partial(_hy_front_kernel, seq=seq),
        out_shape=(out, out),
        grid=(batch, groups),
        in_specs=[uspec(0), uspec(groups), uspec(2 * groups),
                  wspec(0), wspec(groups), wspec(2 * groups),
                  bspec(0), bspec(groups), bspec(2 * groups)],
        out_specs=(ospec, ospec),
        scratch_shapes=[pltpu.VMEM((3, seq + 16, LANES), F32)],
        compiler_params=_cparams(("parallel", "parallel")),
        name="hyena_front",
    )(u_t, u_t, u_t, cw, cw, cw, cb, cb, cb)


def _hy_conv_kernel(z_ref, x_ref, hr_ref, hi_ref, f1_ref, f1i_ref,
                    g_ref, gi_ref, bias_ref, gain_ref, o_ref,
                    zs_ref, ts_ref, ys_ref, *, seq):
    r = FFT_R
    a_half = seq // r
    a_n = _fft_len(seq) // r
    for c in range(2):
        for a in range(a_half):
            zs_ref[c, pl.ds(a * FFT_PITCH, r), :] = (
                z_ref[pl.ds(c * seq + a * r, r), :].astype(F32))

    def stage1(bp, carry):
        cols = []
        for e in range(2):
            rows = pl.ds(2 * bp + e, a_half, stride=FFT_PITCH)
            cols.append(jnp.concatenate([zs_ref[0, rows, :], zs_ref[1, rows, :]], axis=0))
        rhs = jnp.concatenate(cols, axis=1).astype(BF16)
        t = jnp.dot(f1_ref[...], rhs, preferred_element_type=F32)
        for e in range(2):
            for c in range(2):
                ts_ref[c, pl.ds(2 * bp + e, a_n, stride=FFT_PITCH), :] = (
                    t[c * a_n:(c + 1) * a_n, e * LANES:(e + 1) * LANES])
        return carry

    lax.fori_loop(0, r // 2, stage1, 0, unroll=8)

    def stage2(ka, carry):
        rows = pl.ds(pl.multiple_of(ka * FFT_PITCH, 8), r)
        rhs = jnp.concatenate([ts_ref[0, rows, :], ts_ref[1, rows, :]], axis=0).astype(BF16)
        s = jnp.dot(g_ref[ka], rhs, preferred_element_type=F32)
        hrows = pl.ds(pl.multiple_of(ka * r, r), r)
        hr = hr_ref[hrows, :]
        hi = hi_ref[hrows, :]
        sr, si = s[:r], s[r:]
        prod = jnp.concatenate([sr * hr - si * hi, sr * hi + si * hr], axis=0).astype(BF16)
        u = jnp.dot(gi_ref[ka], prod, preferred_element_type=F32)
        ts_ref[0, rows, :] = u[:r]
        ts_ref[1, rows, :] = u[r:]
        return carry

    lax.fori_loop(0, a_n, stage2, 0, unroll=32)

    def stage3(bp, carry):
        cols = []
        for e in range(2):
            rows = pl.ds(2 * bp + e, a_n, stride=FFT_PITCH)
            cols.append(jnp.concatenate([ts_ref[0, rows, :], ts_ref[1, rows, :]], axis=0))
        rhs = jnp.concatenate(cols, axis=1).astype(BF16)
        y = jnp.dot(f1i_ref[...], rhs, preferred_element_type=F32)
        for e in range(2):
            for c in range(2):
                ys_ref[c, pl.ds(2 * bp + e, a_half, stride=FFT_PITCH), :] = (
                    y[c * a_half:(c + 1) * a_half, e * LANES:(e + 1) * LANES])
        return carry

    lax.fori_loop(0, r // 2, stage3, 0, unroll=8)

    def finish(a, carry):
        prow = pl.ds(pl.multiple_of(a * FFT_PITCH, 8), r)
        for c in range(2):
            orow = pl.ds(pl.multiple_of(c * seq + a * r, r), r)
            z = ys_ref[c, prow, :] + zs_ref[c, prow, :] * bias_ref[...]
            y = z * x_ref[orow, :].astype(F32)
            o_ref[orow, :] = _rms(y, gain_ref[...]).astype(BF16)
        return carry

    lax.fori_loop(0, a_half, finish, 0, unroll=8)


def _hyena_conv(z_t, x0c_t, h_re, h_im, hy_bias, hy_out_norm, batch, seq, groups):
    n = _fft_len(seq)
    f1_fwd, _, f1_inv, g_fwd, g_inv = _dft_constants(seq)
    a_n = n // FFT_R
    zspec = pl.BlockSpec((None, 2 * seq, LANES), lambda g, p: (g, p, 0))
    hspec = pl.BlockSpec((n, LANES), lambda g, p: (0, g), pipeline_mode=pl.Buffered(1))
    cspec = lambda arr: pl.BlockSpec(arr.shape, lambda g, p: (0,) * arr.ndim,
                                     pipeline_mode=pl.Buffered(1))
    vspec = pl.BlockSpec((None, 1, LANES), lambda g, p: (g, 0, 0))
    out = jax.ShapeDtypeStruct((groups, batch * seq, LANES), BF16)
    return pl.pallas_call(
        functools.partial(_hy_conv_kernel, seq=seq),
        out_shape=out,
        grid=(groups, batch // 2),
        in_specs=[zspec, zspec, hspec, hspec,
                  cspec(f1_fwd), cspec(f1_inv), cspec(g_fwd), cspec(g_inv), vspec, vspec],
        out_specs=zspec,
        scratch_shapes=[
            pltpu.VMEM((2, seq // FFT_R * FFT_PITCH, LANES), F32),
            pltpu.VMEM((2, a_n * FFT_PITCH, LANES), F32),
            pltpu.VMEM((2, seq // FFT_R * FFT_PITCH, LANES), F32),
        ],
        compiler_params=_cparams(("parallel", "arbitrary")),
        name="hyena_conv",
    )(z_t, x0c_t, h_re, h_im,
      jnp.asarray(f1_fwd, BF16), jnp.asarray(f1_inv, BF16),
      jnp.asarray(g_fwd, BF16), jnp.asarray(g_inv, BF16),
      hy_bias.astype(F32).reshape(groups, 1, LANES),
      hy_out_norm.astype(F32).reshape(groups, 1, LANES))


def _out_proj_kernel(ya_ref, yh_ref, x_ref, w_ref, o_ref, *, heads, groups):
    lhs = jnp.concatenate([ya_ref[g] for g in range(heads)]
                          + [yh_ref[g] for g in range(groups)], axis=-1)
    o_ref[...] = x_ref[...] + jnp.dot(lhs, w_ref[...], preferred_element_type=F32)


def _out_proj(ya_t, yh_t, x2, w_out, tm):
    m, d = x2.shape
    heads, groups = ya_t.shape[0], yh_t.shape[0]
    return pl.pallas_call(
        functools.partial(_out_proj_kernel, heads=heads, groups=groups),
        out_shape=jax.ShapeDtypeStruct((m, d), F32),
        grid=(m // tm,),
        in_specs=[
            pl.BlockSpec((heads, tm, LANES), lambda i: (0, i, 0)),
            pl.BlockSpec((groups, tm, LANES), lambda i: (0, i, 0)),
            pl.BlockSpec((tm, d), lambda i: (i, 0)),
            pl.BlockSpec(w_out.shape, lambda i: (0, 0)),
        ],
        out_specs=pl.BlockSpec((tm, d), lambda i: (i, 0)),
        compiler_params=_cparams(("parallel",)),
        name="out_proj",
    )(ya_t, yh_t, x2, w_out.astype(BF16))


def _ffn_kernel(h_ref, g_ref, wa_ref, wg_ref, wd_ref, o_ref, xn_ref):
    def hidden_tile(xn):
        a = jnp.dot(xn, wa_ref[...], preferred_element_type=F32)
        g = jnp.dot(xn, wg_ref[...], preferred_element_type=F32)
        act = (a * jax.nn.sigmoid(a) * g).astype(BF16)
        return jnp.dot(act, wd_ref[...], preferred_element_type=F32)

    @pl.when(pl.program_id(1) == 0)
    def _():
        chunk = 256
        for c in range(h_ref.shape[0] // chunk):
            rows = pl.ds(c * chunk, chunk)
            h = h_ref[rows, :]
            xn = _rms(h, g_ref[...]).astype(BF16)
            xn_ref[rows, :] = xn
            o_ref[rows, :] = h + hidden_tile(xn)

    @pl.when(pl.program_id(1) > 0)
    def _():
        o_ref[...] += hidden_tile(xn_ref[...])


def _ffn(h1, norm2, w_gu, w_down, tm, th):
    m, d = h1.shape
    hidden = w_down.shape[0]
    nh = hidden // th
    w_gu = w_gu.astype(BF16)
    return pl.pallas_call(
        _ffn_kernel,
        out_shape=jax.ShapeDtypeStruct((m, d), F32),
        grid=(m // tm, nh),
        in_specs=[
            pl.BlockSpec((tm, d), lambda i, j: (i, 0)),
            pl.BlockSpec((1, d), lambda i, j: (0, 0)),
            pl.BlockSpec((d, th), lambda i, j: (0, j)),
            pl.BlockSpec((d, th), lambda i, j: (0, nh + j)),
            pl.BlockSpec((th, d), lambda i, j: (j, 0)),
        ],
        out_specs=pl.BlockSpec((tm, d), lambda i, j: (i, 0)),
        scratch_shapes=[pltpu.VMEM((tm, d), BF16)],
        compiler_params=_cparams(("parallel", "arbitrary")),
        name="ffn",
    )(h1, norm2.reshape(1, d), w_gu, w_gu, w_down.astype(BF16))


def _ple_kernel(h_ref, p_ref, gn_ref, wg_ref, wp_ref, pn_ref, o_ref):
    h = h_ref[...]
    e = _rms(jnp.dot(p_ref[...].astype(BF16), wp_ref[...], preferred_element_type=F32),
             pn_ref[...])
    inv = lax.rsqrt(jnp.mean(h * h, axis=-1, keepdims=True) + EPS)
    logits = jnp.dot((h * gn_ref[...]).astype(BF16), wg_ref[...], preferred_element_type=F32)
    o_ref[...] = h + jax.nn.sigmoid(logits * inv) * e


def _ple(h2, p2, ple_norm, w_gate, w_proj, ple_post_norm, tm):
    m, d = h2.shape
    pd = p2.shape[1]
    return pl.pallas_call(
        _ple_kernel,
        out_shape=jax.ShapeDtypeStruct((m, d), F32),
        grid=(m // tm,),
        in_specs=[
            pl.BlockSpec((tm, d), lambda i: (i, 0)),
            pl.BlockSpec((tm, pd), lambda i: (i, 0)),
            pl.BlockSpec((1, d), lambda i: (0, 0)),
            pl.BlockSpec((d, d), lambda i: (0, 0)),
            pl.BlockSpec((pd, d), lambda i: (0, 0)),
            pl.BlockSpec((1, d), lambda i: (0, 0)),
        ],
        out_specs=pl.BlockSpec((tm, d), lambda i: (i, 0)),
        compiler_params=_cparams(("parallel",)),
        name="ple",
    )(h2, p2, ple_norm.reshape(1, d), w_gate.astype(BF16), w_proj.astype(BF16),
      ple_post_norm.reshape(1, d))


def kernel(x, p, rel_bias, norm1, w_in, q_norm, k_norm, conv_w, conv_b, hy_w1, hy_b1, hy_wi, hy_bi, hy_wo, hy_freq, hy_decay, hy_bias, attn_out_norm, hy_out_norm, w_out, norm2, w_gu, w_down, ple_norm, w_ple_gate, w_ple_proj, ple_post_norm):
    batch, seq, d = x.shape
    attn_w = d // 2
    heads = attn_w // HEAD_DIM
    groups = heads
    hy_w = groups * LANES
    m = batch * seq
    tm = 512
    tm_big = min(1024, m)
    h = x.reshape(m, d)
    bias_tab = _attn_bias_tables(rel_bias)
    for i in range(norm1.shape[0]):
        h3 = h.reshape(batch, seq, d)
        u_p, u_t = _in_proj(h3, norm1[i], w_in[i], q_norm[i], k_norm[i], attn_w, tm_big)
        ya_t = _attention(u_p, bias_tab, attn_out_norm[i], batch, seq, heads)
        filt = _hyena_filter(seq, hy_w, hy_w1[i], hy_b1[i], hy_wi[i], hy_bi[i], hy_wo[i],
                             hy_freq[i], hy_decay[i])
        h_re, h_im = _filter_spectrum(filt, seq)
        z_t, x0c_t = _hyena_front(u_t, conv_w[i], conv_b[i], batch, seq, groups)
        yh_t = _hyena_conv(z_t, x0c_t, h_re, h_im, hy_bias[i], hy_out_norm[i],
                           batch, seq, groups)
        h = _out_proj(ya_t, yh_t, h, w_out[i], tm)
        h = _ffn(h, norm2[i], w_gu[i], w_down[i], tm_big, 512)
        h = _ple(h, p[i].reshape(m, PLE_DIM), ple_norm[i], w_ple_gate[i], w_ple_proj[i],
                 ple_post_norm[i], tm)
    return h.reshape(batch, seq, d)
```

```python
import functools
import math

import jax
import jax.numpy as jnp
import numpy as np
from jax import lax
from jax.experimental import pallas as pl
from jax.experimental.pallas import tpu as pltpu

F32 = jnp.float32
BF16 = jnp.bfloat16

LANES = 128
HEAD_DIM = 128
ATTN_PATTERNS = ((128, 1), (512, 4), (2048, 16))
PERM = 16
BAND = 64
QBLK = 128
KBLK = QBLK + 2 * BAND
N_BUCKETS = 32
REL_MAX_DIST = 1024
HY_EMB = 33
HY_FILTER_WIDTH = 64
HY_INNER = 2
EPS = 1e-6
NEG = -1e30
PLE_DIM = 256

FFT_R = 64
FFT_PITCH = FFT_R + 8
VMEM_LIMIT = 56 * 1024 * 1024


def _cparams(sem, vmem=VMEM_LIMIT):
    return pltpu.CompilerParams(dimension_semantics=sem, vmem_limit_bytes=vmem)


def _rms(x, gain):
    ms = jnp.mean(x * x, axis=-1, keepdims=True)
    return x * lax.rsqrt(ms + EPS) * gain


def _in_proj_kernel(x_ref, g_ref, w_ref, qg_ref, kg_ref, perm_ref, oq_ref, oh_ref,
                    xn_ref, xp_ref, *, nq, hpt):
    j = pl.program_id(1)
    tm = x_ref.shape[0]
    chunk = PERM * PERM

    is_q = j < nq
    is_k = (j >= nq) & (j < 2 * nq)

    def qkv_rows(c, xp):
        acc = jnp.dot(xp, w_ref[...], preferred_element_type=F32)
        gain = jnp.where(is_q, qg_ref[...] * (HEAD_DIM ** -0.5),
                         jnp.where(is_k, kg_ref[...], 1.0))
        for hh in range(hpt):
            a = acc[:, hh * LANES:(hh + 1) * LANES]
            inv = lax.rsqrt(jnp.mean(a * a, axis=-1, keepdims=True) + EPS)
            a = (a * jnp.where(is_q | is_k, inv, 1.0) * gain).astype(BF16)
            for r in range(PERM):
                oq_ref[hh, r, pl.ds(c * PERM, PERM), :] = a[r * PERM:(r + 1) * PERM]

    @pl.when(j == 0)
    def _():
        for c in range(tm // chunk):
            rows = pl.ds(c * chunk, chunk)
            xn = _rms(x_ref[rows, :], g_ref[...]).astype(BF16)
            xp = jnp.dot(perm_ref[...], xn, preferred_element_type=F32).astype(BF16)
            xn_ref[rows, :] = xn
            xp_ref[rows, :] = xp
            qkv_rows(c, xp)

    @pl.when((j > 0) & (j < 3 * nq))
    def _():
        for c in range(tm // chunk):
            qkv_rows(c, xp_ref[pl.ds(c * chunk, chunk), :])

    @pl.when(j >= 3 * nq)
    def _():
        acc = jnp.dot(xn_ref[...], w_ref[...], preferred_element_type=F32)
        for hh in range(hpt):
            oh_ref[hh] = acc[:, hh * LANES:(hh + 1) * LANES].astype(BF16)


def _in_proj(x3, norm1, w_in, q_norm, k_norm, attn_w, tm):
    batch, seq, d = x3.shape
    m = batch * seq
    in_w = w_in.shape[1]
    tn = min(1024, attn_w)
    hpt = tn // LANES
    nq = attn_w // tn
    n_qkv = 3 * nq
    n_tiles = in_w // tn
    tiles_per_seq = seq // tm
    assert tm % (PERM * PERM) == 0 and seq % tm == 0
    perm = jnp.asarray(_to_sequence_order_matrix().T, BF16)
    out_q = jax.ShapeDtypeStruct((n_qkv * hpt, batch, PERM, seq // PERM, LANES), BF16)
    out_h = jax.ShapeDtypeStruct(((n_tiles - n_qkv) * hpt, m, LANES), BF16)
    q_spec = pl.BlockSpec(
        (hpt, None, PERM, tm // PERM, LANES),
        lambda i, j: (jnp.minimum(j, n_qkv - 1), i // tiles_per_seq, 0, i % tiles_per_seq, 0))
    h_spec = pl.BlockSpec((hpt, tm, LANES), lambda i, j: (jnp.maximum(j - n_qkv, 0), i, 0))
    u_q, u_h = pl.pallas_call(
        functools.partial(_in_proj_kernel, nq=nq, hpt=hpt),
        out_shape=(out_q, out_h),
        grid=(m // tm, n_tiles),
        in_specs=[
            pl.BlockSpec((tm, d), lambda i, j: (i, 0)),
            pl.BlockSpec((1, d), lambda i, j: (0, 0)),
            pl.BlockSpec((d, tn), lambda i, j: (0, j)),
            pl.BlockSpec((1, LANES), lambda i, j: (0, 0)),
            pl.BlockSpec((1, LANES), lambda i, j: (0, 0)),
            pl.BlockSpec(perm.shape, lambda i, j: (0, 0)),
        ],
        out_specs=(q_spec, h_spec),
        scratch_shapes=[pltpu.VMEM((tm, d), BF16), pltpu.VMEM((tm, d), BF16)],
        compiler_params=_cparams(("parallel", "arbitrary")),
        name="in_proj",
    )(x3.reshape(m, d), norm1.reshape(1, d), w_in.astype(BF16), q_norm.reshape(1, LANES),
      k_norm.reshape(1, LANES), perm)
    return u_q.reshape(n_qkv * hpt, m, LANES), u_h


def _t5_bucket(rel):
    half = N_BUCKETS // 2
    exact = half // 2
    n = np.abs(rel)
    large = exact + (np.log(np.maximum(n, 1).astype(np.float32) / np.float32(exact))
                     / np.float32(math.log(REL_MAX_DIST / exact))
                     * np.float32(half - exact)).astype(np.int32)
    large = np.minimum(large, half - 1)
    return np.where(rel > 0, half, 0) + np.where(n < exact, n, large)


def _block_orders(dil):
    sub = PERM // dil
    e, i = np.divmod(np.arange(QBLK), QBLK // sub)
    q_off = sub * i + e
    if dil == 1:
        k_off = np.arange(KBLK) - BAND
    else:
        e, i = np.divmod(np.arange(KBLK), KBLK // sub)
        k_off = sub * i + e - BAND
    return q_off, k_off


def _to_sequence_order_matrix():
    n = PERM * PERM
    mat = np.zeros((n, n), np.float32)
    r, i = np.divmod(np.arange(n), PERM)
    mat[PERM * i + r, np.arange(n)] = 1.0
    return mat


def _attn_bias_tables(rel_bias):
    tabs = []
    for _, dil in ATTN_PATTERNS:
        q_off, k_off = _block_orders(dil)
        rel = k_off[None, :] - q_off[:, None]
        band = np.abs(rel) <= BAND
        first_ok = np.broadcast_to(k_off[None, :] >= 0, rel.shape)
        last_ok = np.broadcast_to(k_off[None, :] < QBLK, rel.shape)
        onehot = np.eye(N_BUCKETS, dtype=np.float32)[_t5_bucket(rel * dil)]
        bias = jnp.einsum("qkn,nh->hqk", jnp.asarray(onehot), rel_bias.astype(F32),
                          precision=lax.Precision.HIGHEST)
        variants = [jnp.where(jnp.asarray(band & ok), bias, NEG)
                    for ok in (first_ok, np.ones_like(band), last_ok)]
        tabs.append(jnp.stack(variants, axis=1))
    return jnp.stack(tabs, axis=1)


def _attn_kernel(q_ref, k_ref, v_ref, bias_ref, gain_ref, perm_ref, o_ref,
                 qf_ref, kp_ref, vp_ref, kn_ref, vn_ref, acc_ref, m_ref, *, seq):
    run = seq // PERM
    chunk = PERM * PERM
    ones = jnp.ones((KBLK, LANES), BF16)
    zpad = jnp.zeros((BAND, LANES), BF16)

    for ref in (kp_ref, vp_ref, kn_ref, vn_ref):
        ref[pl.ds(0, BAND), :] = zpad
        ref[pl.ds(BAND + seq, BAND), :] = zpad
    kp_ref[pl.ds(BAND, seq), :] = k_ref[...]
    vp_ref[pl.ds(BAND, seq), :] = v_ref[...]
    qf_ref[...] = q_ref[...].astype(F32)

    def runs(ref, base, n):
        return jnp.concatenate(
            [ref[pl.ds(pl.multiple_of(r * run + base, n), n), :] for r in range(PERM)], axis=0)

    def to_sequence_order(c, carry):
        base = c * PERM
        kv = jnp.concatenate([runs(k_ref, base, PERM), runs(v_ref, base, PERM)], axis=1)
        nat = jnp.dot(perm_ref[...], kv, preferred_element_type=F32).astype(BF16)
        rows = pl.ds(pl.multiple_of(BAND + c * chunk, BAND), chunk)
        kn_ref[rows, :] = nat[:, :LANES]
        vn_ref[rows, :] = nat[:, LANES:]
        return carry

    lax.fori_loop(0, seq // chunk, to_sequence_order, 0, unroll=8)

    def softmax_block(q, k, v, bias):
        s = lax.dot_general(q, k, (((1,), (1,)), ((), ())), preferred_element_type=F32) + bias
        m_blk = jnp.max(s, axis=-1, keepdims=True)
        p = jnp.exp(s - m_blk).astype(BF16)
        a_l = jnp.dot(p, jnp.concatenate([v, ones], axis=1), preferred_element_type=F32)
        return a_l, jnp.broadcast_to(m_blk, (QBLK, LANES))

    def variant(qb, nb):
        return jnp.where(qb == 0, 0, jnp.where(qb == nb - 1, 2, 1))

    def merge(row_slices, a_l, m_b):
        m_old = jnp.concatenate([m_ref[rows, :] for rows in row_slices], axis=0)
        a_old = jnp.concatenate([acc_ref[rows, :] for rows in row_slices], axis=0)
        m_new = jnp.maximum(m_old, m_b)
        w_old = jnp.exp(m_old - m_new)
        w_blk = jnp.exp(m_b - m_new)
        a_new = (a_old * jnp.concatenate([w_old, w_old], axis=1)
                 + a_l * jnp.concatenate([w_blk, w_blk], axis=1))
        n = QBLK // len(row_slices)
        for e, rows in enumerate(row_slices):
            acc_ref[rows, :] = a_new[e * n:(e + 1) * n]
            m_ref[rows, :] = m_new[e * n:(e + 1) * n]

    nb16 = run // QBLK

    def block16(t, carry):
        r, qb = t // nb16, t % nb16
        off = pl.multiple_of(r * run + qb * QBLK, QBLK)
        a_l, m_b = softmax_block(q_ref[pl.ds(off, QBLK), :], kp_ref[pl.ds(off, KBLK), :],
                                 vp_ref[pl.ds(off, KBLK), :], bias_ref[2, variant(qb, nb16)])
        acc_ref[pl.ds(off, QBLK), :] = a_l
        m_ref[pl.ds(off, QBLK), :] = m_b
        return carry

    lax.fori_loop(0, PERM * nb16, block16, 0, unroll=16)

    dil = ATTN_PATTERNS[1][1]
    sub = PERM // dil
    nb4 = (seq // dil) // QBLK
    qn, kn = QBLK // sub, KBLK // sub

    def block4(t, carry):
        r, qb = t // nb4, t % nb4
        q_rows = [pl.ds(pl.multiple_of((dil * e + r) * run + qb * qn, qn), qn)
                  for e in range(sub)]
        k_rows = [pl.ds(pl.multiple_of((dil * e + r) * run + qb * qn + BAND - BAND // sub,
                                       BAND // sub), kn) for e in range(sub)]
        q = jnp.concatenate([q_ref[rows, :] for rows in q_rows], axis=0)
        k = jnp.concatenate([kp_ref[rows, :] for rows in k_rows], axis=0)
        v = jnp.concatenate([vp_ref[rows, :] for rows in k_rows], axis=0)
        a_l, m_b = softmax_block(q, k, v, bias_ref[1, variant(qb, nb4)])
        merge(q_rows, a_l, m_b)
        return carry

    lax.fori_loop(0, dil * nb4, block4, 0, unroll=16)

    nb1 = seq // QBLK
    qn1 = QBLK // PERM

    def block1(qb, carry):
        q_rows = [pl.ds(pl.multiple_of(r * run + qb * qn1, qn1), qn1) for r in range(PERM)]
        q = jnp.concatenate([qf_ref[rows, :] for rows in q_rows], axis=0).astype(BF16)
        k_rows = pl.ds(pl.multiple_of(qb * QBLK, QBLK), KBLK)
        a_l, m_b = softmax_block(q, kn_ref[k_rows, :], vn_ref[k_rows, :],
                                 bias_ref[0, variant(qb, nb1)])
        merge(q_rows, a_l, m_b)
        return carry

    lax.fori_loop(0, nb1, block1, 0, unroll=16)

    def finish(c, carry):
        st = runs(acc_ref, c * PERM, PERM)
        y = _rms(st[:, :LANES] / st[:, LANES:], gain_ref[...]).astype(BF16)
        nat = jnp.dot(perm_ref[...], y, preferred_element_type=F32)
        o_ref[pl.ds(pl.multiple_of(c * chunk, chunk), chunk), :] = nat.astype(BF16)
        return carry

    lax.fori_loop(0, seq // chunk, finish, 0, unroll=8)


def _attention(u_p, bias_tab, attn_out_norm, batch, seq, heads):
    assert ATTN_PATTERNS[0][1] == 1 and ATTN_PATTERNS[2][1] == PERM
    assert seq % (PERM * QBLK) == 0
    perm = jnp.asarray(_to_sequence_order_matrix(), BF16)
    pad_rows = seq + 2 * BAND
    return pl.pallas_call(
        functools.partial(_attn_kernel, seq=seq),
        out_shape=jax.ShapeDtypeStruct((heads, batch * seq, LANES), BF16),
        grid=(batch, heads),
        in_specs=[
            pl.BlockSpec((None, seq, LANES), lambda b, h: (h, b, 0)),
            pl.BlockSpec((None, seq, LANES), lambda b, h: (heads + h, b, 0)),
            pl.BlockSpec((None, seq, LANES), lambda b, h: (2 * heads + h, b, 0)),
            pl.BlockSpec((None, len(ATTN_PATTERNS), 3, QBLK, KBLK),
                         lambda b, h: (h, 0, 0, 0, 0)),
            pl.BlockSpec((None, 1, LANES), lambda b, h: (h, 0, 0)),
            pl.BlockSpec(perm.shape, lambda b, h: (0, 0)),
        ],
        out_specs=pl.BlockSpec((None, seq, LANES), lambda b, h: (h, b, 0)),
        scratch_shapes=[
            pltpu.VMEM((seq, LANES), F32),
            pltpu.VMEM((pad_rows, LANES), BF16),
            pltpu.VMEM((pad_rows, LANES), BF16),
            pltpu.VMEM((pad_rows, LANES), BF16),
            pltpu.VMEM((pad_rows, LANES), BF16),
            pltpu.VMEM((seq, 2 * LANES), F32),
            pltpu.VMEM((seq, LANES), F32),
        ],
        compiler_params=_cparams(("parallel", "parallel")),
        name="attention",
    )(u_p, u_p, u_p, bias_tab, attn_out_norm.reshape(heads, 1, LANES), perm)


def _filter_features(seq):
    pos = np.arange(seq, dtype=np.float32)
    t = pos / np.float32(max(seq - 1, 1))
    bands = (HY_EMB - 1) // 2
    fr = np.linspace(1e-4, bands - 1, bands, dtype=np.float32)
    ang = np.float32(2.0 * math.pi / seq) * pos[:, None] * fr[None, :]
    z = np.concatenate([t[:, None], np.cos(ang), -np.sin(ang)], axis=-1).astype(np.float32)
    zp = np.zeros((seq, LANES), np.float32)
    zp[:, :HY_EMB] = z
    offs = (np.abs(pos - (seq // 2)) / np.float32(seq / 2)).astype(np.float32)
    return zp, offs[:, None]


def _filter_kernel(z_ref, offs_ref, w1_ref, b1_ref, wi_ref, bi_ref, wo_ref, fq_ref,
                   decay_ref, o_ref):
    hp = lax.Precision.HIGHEST
    fq = fq_ref[...]
    h = jnp.sin(fq * (jnp.dot(z_ref[...], w1_ref[...], precision=hp,
                              preferred_element_type=F32) + b1_ref[...]))
    for j in range(HY_INNER):
        h = jnp.sin(fq * (jnp.dot(h, wi_ref[j], precision=hp,
                                  preferred_element_type=F32) + bi_ref[j]))
    h_hi, h_lo = _split_bf16(h)
    w_hi, w_lo = _split_bf16(wo_ref[...])
    dot = functools.partial(jnp.dot, preferred_element_type=F32)
    filt = dot(h_hi, w_hi) + (dot(h_hi, w_lo) + dot(h_lo, w_hi))
    o_ref[...] = filt * jnp.exp(-offs_ref[...] * jnp.abs(decay_ref[...]))


def _hyena_filter(seq, hy_w, w1, b1, wi, bi, wo, freq, decay):
    zp, offs = _filter_features(seq)
    w1p = jnp.zeros((LANES, HY_FILTER_WIDTH), F32).at[:HY_EMB].set(w1.astype(F32))
    tr = 512
    fw = HY_FILTER_WIDTH
    const = lambda *shape: pl.BlockSpec(shape, lambda i: (0,) * len(shape))
    return pl.pallas_call(
        _filter_kernel,
        out_shape=jax.ShapeDtypeStruct((seq, hy_w), F32),
        grid=(seq // tr,),
        in_specs=[
            pl.BlockSpec((tr, LANES), lambda i: (i, 0)),
            pl.BlockSpec((tr, 1), lambda i: (i, 0)),
            const(LANES, fw), const(1, fw), const(HY_INNER, fw, fw),
            const(HY_INNER, 1, fw), const(fw, hy_w), const(1, fw), const(1, hy_w),
        ],
        out_specs=pl.BlockSpec((tr, hy_w), lambda i: (i, 0)),
        compiler_params=_cparams(("parallel",)),
        name="hyena_filter",
    )(jnp.asarray(zp), jnp.asarray(offs), w1p, b1.reshape(1, fw).astype(F32),
      wi.astype(F32), bi.reshape(HY_INNER, 1, fw).astype(F32), wo.astype(F32),
      freq.reshape(1, fw).astype(F32), decay.reshape(1, hy_w).astype(F32))


def _stack(c):
    return np.block([[c.real, -c.imag], [c.imag, c.real]])


def _fft_len(seq):
    return 3 * seq // 2


@functools.lru_cache(maxsize=None)
def _dft_constants(seq):
    n = _fft_len(seq)
    r = FFT_R
    a_n = n // r
    a_in = seq // r
    a_out = (seq // 2) // r
    ar = np.arange(a_n)
    br = np.arange(r)
    f1 = np.exp(-2j * np.pi * np.outer(ar, ar) / a_n)
    f1_fwd = _stack(f1[:, :a_in])
    f1_fwd_real = np.concatenate([f1[:, :a_in].real, f1[:, :a_in].imag], 0)
    f1_inv = _stack(np.conj(f1).T[a_out:a_out + a_in] / n)
    ph = (br[None, None, :] * br[None, :, None] / r
          + br[None, None, :] * ar[:, None, None] / n)
    g = np.exp(-2j * np.pi * ph)
    g_fwd = np.stack([_stack(g[k]) for k in range(a_n)])
    g_inv = np.stack([_stack(np.conj(g[k]).T) for k in range(a_n)])
    return (f1_fwd.astype(np.float32), f1_fwd_real.astype(np.float32),
            f1_inv.astype(np.float32), g_fwd.astype(np.float32), g_inv.astype(np.float32))


def _split_bf16(x):
    if isinstance(x, np.ndarray):
        hi = x.astype(BF16)
        lo = (x - hi.astype(np.float32)).astype(BF16)
        return jnp.asarray(hi), jnp.asarray(lo)
    hi = x.astype(BF16)
    return hi, (x - hi.astype(F32)).astype(BF16)


def _dot3(m_hi, m_lo, x):
    x_hi, x_lo = _split_bf16(x)
    dot = functools.partial(jnp.dot, preferred_element_type=F32)
    return dot(m_hi, x_hi) + (dot(m_hi, x_lo) + dot(m_lo, x_hi))


def _spectrum_kernel(filt_ref, f1h_ref, f1l_ref, gh_ref, gl_ref, hr_ref, hi_ref,
                     zs_ref, ts_ref, *, seq):
    r = FFT_R
    a_half = seq // r
    a_n = _fft_len(seq) // r
    for a in range(a_half):
        zs_ref[pl.ds(a * FFT_PITCH, r), :] = filt_ref[pl.ds(a * r, r), :]

    def stage1(bp, carry):
        rhs = jnp.concatenate(
            [zs_ref[pl.ds(2 * bp + e, a_half, stride=FFT_PITCH), :] for e in range(2)], axis=1)
        t = _dot3(f1h_ref[...], f1l_ref[...], rhs)
        for e in range(2):
            for c in range(2):
                ts_ref[c, pl.ds(2 * bp + e, a_n, stride=FFT_PITCH), :] = (
                    t[c * a_n:(c + 1) * a_n, e * LANES:(e + 1) * LANES])
        return carry

    lax.fori_loop(0, r // 2, stage1, 0, unroll=4)

    def stage2(ka, carry):
        rows = pl.ds(pl.multiple_of(ka * FFT_PITCH, 8), r)
        rhs = jnp.concatenate([ts_ref[0, rows, :], ts_ref[1, rows, :]], axis=0)
        s = _dot3(gh_ref[ka], gl_ref[ka], rhs)
        out = pl.ds(pl.multiple_of(ka * r, r), r)
        hr_ref[out, :] = s[:r]
        hi_ref[out, :] = s[r:]
        return carry

    lax.fori_loop(0, a_n, stage2, 0, unroll=16)


def _filter_spectrum(filt, seq):
    hy_w = filt.shape[1]
    n = _fft_len(seq)
    _, f1_real, _, g_fwd, _ = _dft_constants(seq)
    f1h, f1l = _split_bf16(f1_real)
    gh, gl = _split_bf16(g_fwd)
    a_n = n // FFT_R
    out = jax.ShapeDtypeStruct((n, hy_w), F32)
    gspec = pl.BlockSpec(g_fwd.shape, lambda c: (0, 0, 0), pipeline_mode=pl.Buffered(1))
    return pl.pallas_call(
        functools.partial(_spectrum_kernel, seq=seq),
        out_shape=(out, out),
        grid=(hy_w // LANES,),
        in_specs=[
            pl.BlockSpec((seq, LANES), lambda c: (0, c)),
            pl.BlockSpec(f1_real.shape, lambda c: (0, 0)),
            pl.BlockSpec(f1_real.shape, lambda c: (0, 0)),
            gspec, gspec,
        ],
        out_specs=(pl.BlockSpec((n, LANES), lambda c: (0, c)),
                   pl.BlockSpec((n, LANES), lambda c: (0, c))),
        scratch_shapes=[
            pltpu.VMEM((seq // FFT_R * FFT_PITCH, LANES), F32),
            pltpu.VMEM((2, a_n * FFT_PITCH, LANES), F32),
        ],
        compiler_params=_cparams(("parallel",)),
        name="filter_spectrum",
    )(filt, f1h, f1l, gh, gl)


def _hy_front_kernel(x0_ref, x1_ref, hv_ref, w0_ref, w1_ref, wv_ref, b0_ref, b1_ref, bv_ref,
                     z_ref, x0c_ref, pad_ref, *, seq):
    chunk = 512
    zero_row = jnp.zeros((8, LANES), F32)
    for idx, ref in enumerate((x0_ref, x1_ref, hv_ref)):
        pad_ref[idx, pl.ds(0, 8), :] = zero_row
        pad_ref[idx, pl.ds(8 + seq, 8), :] = zero_row
        pad_ref[idx, pl.ds(8, seq), :] = ref[...].astype(F32)

    def conv(idx, w_ref, b_ref, c):
        base = 8 + c * chunk
        w = w_ref[...]
        return (pad_ref[idx, pl.ds(base - 1, chunk), :] * w[0:1]
                + pad_ref[idx, pl.ds(base, chunk), :] * w[1:2]
                + pad_ref[idx, pl.ds(base + 1, chunk), :] * w[2:3]
                + b_ref[...])

    for c in range(seq // chunk):
        rows = pl.ds(c * chunk, chunk)
        x0c_ref[rows, :] = conv(0, w0_ref, b0_ref, c).astype(BF16)
        z_ref[rows, :] = (conv(2, wv_ref, bv_ref, c) * conv(1, w1_ref, b1_ref, c)).astype(BF16)


def _hyena_front(u_t, conv_w, conv_b, batch, seq, groups):
    ng = 3 * groups
    cw = conv_w.astype(F32).reshape(3, ng, LANES).transpose(1, 0, 2)
    cb = conv_b.astype(F32).reshape(ng, 1, LANES)
    uspec = lambda off: pl.BlockSpec((None, seq, LANES), lambda b, g: (off + g, b, 0))
    wspec = lambda off: pl.BlockSpec((None, 3, LANES), lambda b, g: (off + g, 0, 0))
    bspec = lambda off: pl.BlockSpec((None, 1, LANES), lambda b, g: (off + g, 0, 0))
    out = jax.ShapeDtypeStruct((groups, batch * seq, LANES), BF16)
    ospec = pl.BlockSpec((None, seq, LANES), lambda b, g: (g, b, 0))
    return pl.pallas_call(
        functools.partial(_hy_front_kernel, seq=seq),
        out_shape=(out, out),
        grid=(batch, groups),
        in_specs=[uspec(0), uspec(groups), uspec(2 * groups),
                  wspec(0), wspec(groups), wspec(2 * groups),
                  bspec(0), bspec(groups), bspec(2 * groups)],
        out_specs=(ospec, ospec),
        scratch_shapes=[pltpu.VMEM((3, seq + 16, LANES), F32)],
        compiler_params=_cparams(("parallel", "parallel")),
        name="hyena_front",
    )(u_t, u_t, u_t, cw, cw, cw, cb, cb, cb)


def _hy_conv_kernel(z_ref, x_ref, hr_ref, hi_ref, f1_ref, f1i_ref,
                    g_ref, gi_ref, bias_ref, gain_ref, o_ref,
                    zs_ref, ts_ref, ys_ref, *, seq):
    r = FFT_R
    a_half = seq // r
    a_n = _fft_len(seq) // r
    for c in range(2):
        for a in range(a_half):
            zs_ref[c, pl.ds(a * FFT_PITCH, r), :] = (
                z_ref[pl.ds(c * seq + a * r, r), :].astype(F32))

    def stage1(bp, carry):
        cols = []
        for e in range(2):
            rows = pl.ds(2 * bp + e, a_half, stride=FFT_PITCH)
            cols.append(jnp.concatenate([zs_ref[0, rows, :], zs_ref[1, rows, :]], axis=0))
        rhs = jnp.concatenate(cols, axis=1).astype(BF16)
        t = jnp.dot(f1_ref[...], rhs, preferred_element_type=F32)
        for e in range(2):
            for c in range(2):
                ts_ref[c, pl.ds(2 * bp + e, a_n, stride=FFT_PITCH), :] = (
                    t[c * a_n:(c + 1) * a_n, e * LANES:(e + 1) * LANES])
        return carry

    lax.fori_loop(0, r // 2, stage1, 0, unroll=8)

    def stage2(ka, carry):
        rows = pl.ds(pl.multiple_of(ka * FFT_PITCH, 8), r)
        rhs = jnp.concatenate([ts_ref[0, rows, :], ts_ref[1, rows, :]], axis=0).astype(BF16)
        s = jnp.dot(g_ref[ka], rhs, preferred_element_type=F32)
        hrows = pl.ds(pl.multiple_of(ka * r, r), r)
        hr = hr_ref[hrows, :]
        hi = hi_ref[hrows, :]
        sr, si = s[:r], s[r:]
        prod = jnp.concatenate([sr * hr - si * hi, sr * hi + si * hr], axis=0).astype(BF16)
        u = jnp.dot(gi_ref[ka], prod, preferred_element_type=F32)
        ts_ref[0, rows, :] = u[:r]
        ts_ref[1, rows, :] = u[r:]
        return carry

    lax.fori_loop(0, a_n, stage2, 0, unroll=32)

    def stage3(bp, carry):
        cols = []
        for e in range(2):
            rows = pl.ds(2 * bp + e, a_n, stride=FFT_PITCH)
            cols.append(jnp.concatenate([ts_ref[0, rows, :], ts_ref[1, rows, :]], axis=0))
        rhs = jnp.concatenate(cols, axis=1).astype(BF16)
        y = jnp.dot(f1i_ref[...], rhs, preferred_element_type=F32)
        for e in range(2):
            for c in range(2):
                ys_ref[c, pl.ds(2 * bp + e, a_half, stride=FFT_PITCH), :] = (
                    y[c * a_half:(c + 1) * a_half, e * LANES:(e + 1) * LANES])
        return carry

    lax.fori_loop(0, r // 2, stage3, 0, unroll=8)

    def finish(a, carry):
        prow = pl.ds(pl.multiple_of(a * FFT_PITCH, 8), r)
        for c in range(2):
            orow = pl.ds(pl.multiple_of(c * seq + a * r, r), r)
            z = ys_ref[c, prow, :] + zs_ref[c, prow, :] * bias_ref[...]
            y = z * x_ref[orow, :].astype(F32)
            o_ref[orow, :] = _rms(y, gain_ref[...]).astype(BF16)
        return carry

    lax.fori_loop(0, a_half, finish, 0, unroll=8)


def _hyena_conv(z_t, x0c_t, h_re, h_im, hy_bias, hy_out_norm, batch, seq, groups):
    n = _fft_len(seq)
    f1_fwd, _, f1_inv, g_fwd, g_inv = _dft_constants(seq)
    a_n = n // FFT_R
    zspec = pl.BlockSpec((None, 2 * seq, LANES), lambda g, p: (g, p, 0))
    hspec = pl.BlockSpec((n, LANES), lambda g, p: (0, g), pipeline_mode=pl.Buffered(1))
    cspec = lambda arr: pl.BlockSpec(arr.shape, lambda g, p: (0,) * arr.ndim,
                                     pipeline_mode=pl.Buffered(1))
    vspec = pl.BlockSpec((None, 1, LANES), lambda g, p: (g, 0, 0))
    out = jax.ShapeDtypeStruct((groups, batch * seq, LANES), BF16)
    return pl.pallas_call(
        functools.partial(_hy_conv_kernel, seq=seq),
        out_shape=out,
        grid=(groups, batch // 2),
        in_specs=[zspec, zspec, hspec, hspec,
                  cspec(f1_fwd), cspec(f1_inv), cspec(g_fwd), cspec(g_inv), vspec, vspec],
        out_specs=zspec,
        scratch_shapes=[
            pltpu.VMEM((2, seq // FFT_R * FFT_PITCH, LANES), F32),
            pltpu.VMEM((2, a_n * FFT_PITCH, LANES), F32),
            pltpu.VMEM((2, seq // FFT_R * FFT_PITCH, LANES), F32),
        ],
        compiler_params=_cparams(("parallel", "arbitrary")),
        name="hyena_conv",
    )(z_t, x0c_t, h_re, h_im,
      jnp.asarray(f1_fwd, BF16), jnp.asarray(f1_inv, BF16),
      jnp.asarray(g_fwd, BF16), jnp.asarray(g_inv, BF16),
      hy_bias.astype(F32).reshape(groups, 1, LANES),
      hy_out_norm.astype(F32).reshape(groups, 1, LANES))


def _out_proj_kernel(ya_ref, yh_ref, x_ref, w_ref, o_ref, *, heads, groups):
    lhs = jnp.concatenate([ya_ref[g] for g in range(heads)]
                          + [yh_ref[g] for g in range(groups)], axis=-1)
    o_ref[...] = x_ref[...] + jnp.dot(lhs, w_ref[...], preferred_element_type=F32)


def _out_proj(ya_t, yh_t, x2, w_out, tm):
    m, d = x2.shape
    heads, groups = ya_t.shape[0], yh_t.shape[0]
    return pl.pallas_call(
        functools.partial(_out_proj_kernel, heads=heads, groups=groups),
        out_shape=jax.ShapeDtypeStruct((m, d), F32),
        grid=(m // tm,),
        in_specs=[
            pl.BlockSpec((heads, tm, LANES), lambda i: (0, i, 0)),
            pl.BlockSpec((groups, tm, LANES), lambda i: (0, i, 0)),
            pl.BlockSpec((tm, d), lambda i: (i, 0)),
            pl.BlockSpec(w_out.shape, lambda i: (0, 0), pipeline_mode=pl.Buffered(1)),
        ],
        out_specs=pl.BlockSpec((tm, d), lambda i: (i, 0)),
        compiler_params=_cparams(("parallel",)),
        name="out_proj",
    )(ya_t, yh_t, x2, w_out.astype(BF16))


def _ffn_kernel(h_ref, g_ref, wa_ref, wg_ref, wd_ref, o_ref, xn_ref):
    def hidden_tile(xn):
        a = jnp.dot(xn, wa_ref[...], preferred_element_type=F32)
        g = jnp.dot(xn, wg_ref[...], preferred_element_type=F32)
        act = (a * jax.nn.sigmoid(a) * g).astype(BF16)
        return jnp.dot(act, wd_ref[...], preferred_element_type=F32)

    @pl.when(pl.program_id(1) == 0)
    def _():
        chunk = 256
        for c in range(h_ref.shape[0] // chunk):
            rows = pl.ds(c * chunk, chunk)
            h = h_ref[rows, :]
            xn = _rms(h, g_ref[...]).astype(BF16)
            xn_ref[rows, :] = xn
            o_ref[rows, :] = h + hidden_tile(xn)

    @pl.when(pl.program_id(1) > 0)
    def _():
        o_ref[...] += hidden_tile(xn_ref[...])


def _ffn(h1, norm2, w_gu, w_down, tm, th):
    m, d = h1.shape
    hidden = w_down.shape[0]
    nh = hidden // th
    w_gu = w_gu.astype(BF16)
    return pl.pallas_call(
        _ffn_kernel,
        out_shape=jax.ShapeDtypeStruct((m, d), F32),
        grid=(m // tm, nh),
        in_specs=[
            pl.BlockSpec((tm, d), lambda i, j: (i, 0)),
            pl.BlockSpec((1, d), lambda i, j: (0, 0)),
            pl.BlockSpec((d, th), lambda i, j: (0, j)),
            pl.BlockSpec((d, th), lambda i, j: (0, nh + j)),
            pl.BlockSpec((th, d), lambda i, j: (j, 0)),
        ],
        out_specs=pl.BlockSpec((tm, d), lambda i, j: (i, 0)),
        scratch_shapes=[pltpu.VMEM((tm, d), BF16)],
        compiler_params=_cparams(("parallel", "arbitrary")),
        name="ffn",
    )(h1, norm2.reshape(1, d), w_gu, w_gu, w_down.astype(BF16))


def _ple_kernel(h_ref, p_ref, gn_ref, wg_ref, wp_ref, pn_ref, o_ref):
    h = h_ref[...]
    e = _rms(jnp.dot(p_ref[...].astype(BF16), wp_ref[...], preferred_element_type=F32),
             pn_ref[...])
    inv = lax.rsqrt(jnp.mean(h * h, axis=-1, keepdims=True) + EPS)
    logits = jnp.dot((h * gn_ref[...]).astype(BF16), wg_ref[...], preferred_element_type=F32)
    o_ref[...] = h + jax.nn.sigmoid(logits * inv) * e


def _ple(h2, p2, ple_norm, w_gate, w_proj, ple_post_norm, tm):
    m, d = h2.shape
    pd = p2.shape[1]
    return pl.pallas_call(
        _ple_kernel,
        out_shape=jax.ShapeDtypeStruct((m, d), F32),
        grid=(m // tm,),
        in_specs=[
            pl.BlockSpec((tm, d), lambda i: (i, 0)),
            pl.BlockSpec((tm, pd), lambda i: (i, 0)),
            pl.BlockSpec((1, d), lambda i: (0, 0)),
            pl.BlockSpec((d, d), lambda i: (0, 0), pipeline_mode=pl.Buffered(1)),
            pl.BlockSpec((pd, d), lambda i: (0, 0), pipeline_mode=pl.Buffered(1)),
            pl.BlockSpec((1, d), lambda i: (0, 0)),
        ],
        out_specs=pl.BlockSpec((tm, d), lambda i: (i, 0)),
        compiler_params=_cparams(("parallel",)),
        name="ple",
    )(h2, p2, ple_norm.reshape(1, d), w_gate.astype(BF16), w_proj.astype(BF16),
      ple_post_norm.reshape(1, d))


def kernel(x, p, rel_bias, norm1, w_in, q_norm, k_norm, conv_w, conv_b, hy_w1, hy_b1, hy_wi, hy_bi, hy_wo, hy_freq, hy_decay, hy_bias, attn_out_norm, hy_out_norm, w_out, norm2, w_gu, w_down, ple_norm, w_ple_gate, w_ple_proj, ple_post_norm):
    batch, seq, d = x.shape
    attn_w = d // 2
    heads = attn_w // HEAD_DIM
    groups = heads
    hy_w = groups * LANES
    m = batch * seq
    tm = min(1024, m)
    th = 512
    h = x.reshape(m, d)
    bias_tab = _attn_bias_tables(rel_bias)
    for i in range(norm1.shape[0]):
        h3 = h.reshape(batch, seq, d)
        u_p, u_t = _in_proj(h3, norm1[i], w_in[i], q_norm[i], k_norm[i], attn_w, tm)
        ya_t = _attention(u_p, bias_tab, attn_out_norm[i], batch, seq, heads)
        filt = _hyena_filter(seq, hy_w, hy_w1[i], hy_b1[i], hy_wi[i], hy_bi[i], hy_wo[i],
                             hy_freq[i], hy_decay[i])
        h_re, h_im = _filter_spectrum(filt, seq)
        z_t, x0c_t = _hyena_front(u_t, conv_w[i], conv_b[i], batch, seq, groups)
        yh_t = _hyena_conv(z_t, x0c_t, h_re, h_im, hy_bias[i], hy_out_norm[i],
                           batch, seq, groups)
        h = _out_proj(ya_t, yh_t, h, w_out[i], tm)
        h = _ffn(h, norm2[i], w_gu[i], w_down[i], tm, th)
        h = _ple(h, p[i].reshape(m, PLE_DIM), ple_norm[i], w_ple_gate[i], w_ple_proj[i],
                 ple_post_norm[i], tm // 2)
    return h.reshape(batch, seq, d)
```

```python
import functools
import math

import jax
import jax.numpy as jnp
import numpy as np
from jax import lax
from jax.experimental import pallas as pl
from jax.experimental.pallas import tpu as pltpu

F32 = jnp.float32
BF16 = jnp.bfloat16

LANES = 128
HEAD_DIM = 128
ATTN_PATTERNS = ((128, 1), (512, 4), (2048, 16))
PERM = 16
BAND = 64
QBLK = 128
KBLK = QBLK + 2 * BAND
N_BUCKETS = 32
REL_MAX_DIST = 1024
HY_EMB = 33
HY_FILTER_WIDTH = 64
HY_INNER = 2
EPS = 1e-6
NEG = -1e30
PLE_DIM = 256

FFT_R = 64
FFT_PITCH = FFT_R + 8
VMEM_LIMIT = 56 * 1024 * 1024


def _cparams(sem, vmem=VMEM_LIMIT):
    return pltpu.CompilerParams(dimension_semantics=sem, vmem_limit_bytes=vmem)


def _rms(x, gain):
    ms = jnp.mean(x * x, axis=-1, keepdims=True)
    return x * lax.rsqrt(ms + EPS) * gain


def _in_proj_kernel(x_ref, g_ref, w_ref, qg_ref, kg_ref, perm_ref, oq_ref, oh_ref, *,
                    attn_w, tn):
    tm = x_ref.shape[0]
    chunk = PERM * PERM
    hpt = tn // LANES
    n_tiles = w_ref.shape[1] // tn
    n_qkv = 3 * attn_w // tn
    dot = functools.partial(jnp.dot, preferred_element_type=F32)

    xn, xp = [], []
    for c in range(tm // chunk):
        xn.append(_rms(x_ref[pl.ds(c * chunk, chunk), :], g_ref[...]).astype(BF16))
        xp.append(dot(perm_ref[...], xn[c]).astype(BF16))

    for t in range(n_qkv):
        w = w_ref[:, t * tn:(t + 1) * tn]
        gain = (qg_ref[...] * (HEAD_DIM ** -0.5) if t * tn < attn_w
                else kg_ref[...] if t * tn < 2 * attn_w else None)
        for c in range(tm // chunk):
            acc = dot(xp[c], w)
            for hh in range(hpt):
                a = acc[:, hh * LANES:(hh + 1) * LANES]
                if gain is not None:
                    a = _rms(a, gain)
                a = a.astype(BF16)
                for r in range(PERM):
                    oq_ref[t * hpt + hh, r, pl.ds(c * PERM, PERM), :] = (
                        a[r * PERM:(r + 1) * PERM])

    xn = jnp.concatenate(xn, axis=0)
    for t in range(n_qkv, n_tiles):
        acc = dot(xn, w_ref[:, t * tn:(t + 1) * tn])
        for hh in range(hpt):
            oh_ref[(t - n_qkv) * hpt + hh] = acc[:, hh * LANES:(hh + 1) * LANES].astype(BF16)


def _in_proj(x3, norm1, w_in, q_norm, k_norm, attn_w, tm):
    batch, seq, d = x3.shape
    m = batch * seq
    in_w = w_in.shape[1]
    tn = min(1024, attn_w)
    g_qkv = 3 * attn_w // LANES
    g_hy = in_w // LANES - g_qkv
    tiles_per_seq = seq // tm
    assert tm % (PERM * PERM) == 0 and seq % tm == 0
    perm = jnp.asarray(_to_sequence_order_matrix().T, BF16)
    const = lambda shape: pl.BlockSpec(shape, lambda i: (0,) * len(shape))
    u_q, u_h = pl.pallas_call(
        functools.partial(_in_proj_kernel, attn_w=attn_w, tn=tn),
        out_shape=(jax.ShapeDtypeStruct((g_qkv, batch, PERM, seq // PERM, LANES), BF16),
                   jax.ShapeDtypeStruct((g_hy, m, LANES), BF16)),
        grid=(m // tm,),
        in_specs=[
            pl.BlockSpec((tm, d), lambda i: (i, 0)),
            const((1, d)),
            pl.BlockSpec((d, in_w), lambda i: (0, 0), pipeline_mode=pl.Buffered(1)),
            const((1, LANES)), const((1, LANES)), const(perm.shape),
        ],
        out_specs=(
            pl.BlockSpec((g_qkv, None, PERM, tm // PERM, LANES),
                         lambda i: (0, i // tiles_per_seq, 0, i % tiles_per_seq, 0)),
            pl.BlockSpec((g_hy, tm, LANES), lambda i: (0, i, 0)),
        ),
        compiler_params=_cparams(("parallel",)),
        name="in_proj",
    )(x3.reshape(m, d), norm1.reshape(1, d), w_in.astype(BF16), q_norm.reshape(1, LANES),
      k_norm.reshape(1, LANES), perm)
    return u_q.reshape(g_qkv, m, LANES), u_h


def _t5_bucket(rel):
    half = N_BUCKETS // 2
    exact = half // 2
    n = np.abs(rel)
    large = exact + (np.log(np.maximum(n, 1).astype(np.float32) / np.float32(exact))
                     / np.float32(math.log(REL_MAX_DIST / exact))
                     * np.float32(half - exact)).astype(np.int32)
    large = np.minimum(large, half - 1)
    return np.where(rel > 0, half, 0) + np.where(n < exact, n, large)


def _block_orders(dil):
    sub = PERM // dil
    e, i = np.divmod(np.arange(QBLK), QBLK // sub)
    q_off = sub * i + e
    if dil == 1:
        k_off = np.arange(KBLK) - BAND
    else:
        e, i = np.divmod(np.arange(KBLK), KBLK // sub)
        k_off = sub * i + e - BAND
    return q_off, k_off


def _to_sequence_order_matrix():
    n = PERM * PERM
    mat = np.zeros((n, n), np.float32)
    r, i = np.divmod(np.arange(n), PERM)
    mat[PERM * i + r, np.arange(n)] = 1.0
    return mat


def _attn_bias_tables(rel_bias):
    tabs = []
    for _, dil in ATTN_PATTERNS:
        q_off, k_off = _block_orders(dil)
        rel = k_off[None, :] - q_off[:, None]
        band = np.abs(rel) <= BAND
        first_ok = np.broadcast_to(k_off[None, :] >= 0, rel.shape)
        last_ok = np.broadcast_to(k_off[None, :] < QBLK, rel.shape)
        onehot = np.eye(N_BUCKETS, dtype=np.float32)[_t5_bucket(rel * dil)]
        bias = jnp.einsum("qkn,nh->hqk", jnp.asarray(onehot), rel_bias.astype(F32),
                          precision=lax.Precision.HIGHEST)
        variants = [jnp.where(jnp.asarray(band & ok), bias, NEG)
                    for ok in (first_ok, np.ones_like(band), last_ok)]
        tabs.append(jnp.stack(variants, axis=1))
    return jnp.stack(tabs, axis=1)


def _attn_kernel(q_ref, k_ref, v_ref, bias_ref, gain_ref, perm_ref, o_ref,
                 qf_ref, kp_ref, vp_ref, kn_ref, vn_ref, acc_ref, m_ref, *, seq):
    run = seq // PERM
    chunk = PERM * PERM
    ones = jnp.ones((KBLK, LANES), BF16)
    zpad = jnp.zeros((BAND, LANES), BF16)

    for ref in (kp_ref, vp_ref, kn_ref, vn_ref):
        ref[pl.ds(0, BAND), :] = zpad
        ref[pl.ds(BAND + seq, BAND), :] = zpad
    kp_ref[pl.ds(BAND, seq), :] = k_ref[...]
    vp_ref[pl.ds(BAND, seq), :] = v_ref[...]
    qf_ref[...] = q_ref[...].astype(F32)

    def runs(ref, base, n):
        return jnp.concatenate(
            [ref[pl.ds(pl.multiple_of(r * run + base, n), n), :] for r in range(PERM)], axis=0)

    def to_sequence_order(c, carry):
        base = c * PERM
        kv = jnp.concatenate([runs(k_ref, base, PERM), runs(v_ref, base, PERM)], axis=1)
        nat = jnp.dot(perm_ref[...], kv, preferred_element_type=F32).astype(BF16)
        rows = pl.ds(pl.multiple_of(BAND + c * chunk, BAND), chunk)
        kn_ref[rows, :] = nat[:, :LANES]
        vn_ref[rows, :] = nat[:, LANES:]
        return carry

    lax.fori_loop(0, seq // chunk, to_sequence_order, 0, unroll=8)

    def softmax_block(q, k, v, bias):
        s = lax.dot_general(q, k, (((1,), (1,)), ((), ())), preferred_element_type=F32) + bias
        m_blk = jnp.max(s, axis=-1, keepdims=True)
        p = jnp.exp(s - m_blk).astype(BF16)
        a_l = jnp.dot(p, jnp.concatenate([v, ones], axis=1), preferred_element_type=F32)
        return a_l, jnp.broadcast_to(m_blk, (QBLK, LANES))

    def variant(qb, nb):
        return jnp.where(qb == 0, 0, jnp.where(qb == nb - 1, 2, 1))

    def merge(row_slices, a_l, m_b):
        m_old = jnp.concatenate([m_ref[rows, :] for rows in row_slices], axis=0)
        a_old = jnp.concatenate([acc_ref[rows, :] for rows in row_slices], axis=0)
        m_new = jnp.maximum(m_old, m_b)
        w_old = jnp.exp(m_old - m_new)
        w_blk = jnp.exp(m_b - m_new)
        a_new = (a_old * jnp.concatenate([w_old, w_old], axis=1)
                 + a_l * jnp.concatenate([w_blk, w_blk], axis=1))
        n = QBLK // len(row_slices)
        for e, rows in enumerate(row_slices):
            acc_ref[rows, :] = a_new[e * n:(e + 1) * n]
            m_ref[rows, :] = m_new[e * n:(e + 1) * n]

    nb16 = run // QBLK

    def block16(t, carry):
        r, qb = t // nb16, t % nb16
        off = pl.multiple_of(r * run + qb * QBLK, QBLK)
        a_l, m_b = softmax_block(q_ref[pl.ds(off, QBLK), :], kp_ref[pl.ds(off, KBLK), :],
                                 vp_ref[pl.ds(off, KBLK), :], bias_ref[2, variant(qb, nb16)])
        acc_ref[pl.ds(off, QBLK), :] = a_l
        m_ref[pl.ds(off, QBLK), :] = m_b
        return carry

    lax.fori_loop(0, PERM * nb16, block16, 0, unroll=16)

    dil = ATTN_PATTERNS[1][1]
    sub = PERM // dil
    nb4 = (seq // dil) // QBLK
    qn, kn = QBLK // sub, KBLK // sub

    def block4(t, carry):
        r, qb = t // nb4, t % nb4
        q_rows = [pl.ds(pl.multiple_of((dil * e + r) * run + qb * qn, qn), qn)
                  for e in range(sub)]
        k_rows = [pl.ds(pl.multiple_of((dil * e + r) * run + qb * qn + BAND - BAND // sub,
                                       BAND // sub), kn) for e in range(sub)]
        q = jnp.concatenate([q_ref[rows, :] for rows in q_rows], axis=0)
        k = jnp.concatenate([kp_ref[rows, :] for rows in k_rows], axis=0)
        v = jnp.concatenate([vp_ref[rows, :] for rows in k_rows], axis=0)
        a_l, m_b = softmax_block(q, k, v, bias_ref[1, variant(qb, nb4)])
        merge(q_rows, a_l, m_b)
        return carry

    lax.fori_loop(0, dil * nb4, block4, 0, unroll=16)

    nb1 = seq // QBLK
    qn1 = QBLK // PERM

    def block1(qb, carry):
        q_rows = [pl.ds(pl.multiple_of(r * run + qb * qn1, qn1), qn1) for r in range(PERM)]
        q = jnp.concatenate([qf_ref[rows, :] for rows in q_rows], axis=0).astype(BF16)
        k_rows = pl.ds(pl.multiple_of(qb * QBLK, QBLK), KBLK)
        a_l, m_b = softmax_block(q, kn_ref[k_rows, :], vn_ref[k_rows, :],
                                 bias_ref[0, variant(qb, nb1)])
        merge(q_rows, a_l, m_b)
        return carry

    lax.fori_loop(0, nb1, block1, 0, unroll=16)

    def finish(c, carry):
        st = runs(acc_ref, c * PERM, PERM)
        y = _rms(st[:, :LANES] / st[:, LANES:], gain_ref[...]).astype(BF16)
        nat = jnp.dot(perm_ref[...], y, preferred_element_type=F32)
        o_ref[pl.ds(pl.multiple_of(c * chunk, chunk), chunk), :] = nat.astype(BF16)
        return carry

    lax.fori_loop(0, seq // chunk, finish, 0, unroll=8)


def _attention(u_p, bias_tab, attn_out_norm, batch, seq, heads):
    assert ATTN_PATTERNS[0][1] == 1 and ATTN_PATTERNS[2][1] == PERM
    assert seq % (PERM * QBLK) == 0
    perm = jnp.asarray(_to_sequence_order_matrix(), BF16)
    pad_rows = seq + 2 * BAND
    return pl.pallas_call(
        functools.partial(_attn_kernel, seq=seq),
        out_shape=jax.ShapeDtypeStruct((heads, batch * seq, LANES), BF16),
        grid=(batch, heads),
        in_specs=[
            pl.BlockSpec((None, seq, LANES), lambda b, h: (h, b, 0)),
            pl.BlockSpec((None, seq, LANES), lambda b, h: (heads + h, b, 0)),
            pl.BlockSpec((None, seq, LANES), lambda b, h: (2 * heads + h, b, 0)),
            pl.BlockSpec((None, len(ATTN_PATTERNS), 3, QBLK, KBLK),
                         lambda b, h: (h, 0, 0, 0, 0)),
            pl.BlockSpec((None, 1, LANES), lambda b, h: (h, 0, 0)),
            pl.BlockSpec(perm.shape, lambda b, h: (0, 0)),
        ],
        out_specs=pl.BlockSpec((None, seq, LANES), lambda b, h: (h, b, 0)),
        scratch_shapes=[
            pltpu.VMEM((seq, LANES), F32),
            pltpu.VMEM((pad_rows, LANES), BF16),
            pltpu.VMEM((pad_rows, LANES), BF16),
            pltpu.VMEM((pad_rows, LANES), BF16),
            pltpu.VMEM((pad_rows, LANES), BF16),
            pltpu.VMEM((seq, 2 * LANES), F32),
            pltpu.VMEM((seq, LANES), F32),
        ],
        compiler_params=_cparams(("parallel", "parallel")),
        name="attention",
    )(u_p, u_p, u_p, bias_tab, attn_out_norm.reshape(heads, 1, LANES), perm)


def _filter_features(seq):
    pos = np.arange(seq, dtype=np.float32)
    t = pos / np.float32(max(seq - 1, 1))
    bands = (HY_EMB - 1) // 2
    fr = np.linspace(1e-4, bands - 1, bands, dtype=np.float32)
    ang = np.float32(2.0 * math.pi / seq) * pos[:, None] * fr[None, :]
    z = np.concatenate([t[:, None], np.cos(ang), -np.sin(ang)], axis=-1).astype(np.float32)
    zp = np.zeros((seq, LANES), np.float32)
    zp[:, :HY_EMB] = z
    offs = (np.abs(pos - (seq // 2)) / np.float32(seq / 2)).astype(np.float32)
    return zp, offs[:, None]


def _filter_kernel(z_ref, offs_ref, w1_ref, b1_ref, wi_ref, bi_ref, wo_ref, fq_ref,
                   decay_ref, o_ref):
    hp = lax.Precision.HIGHEST
    fq = fq_ref[...]
    h = jnp.sin(fq * (jnp.dot(z_ref[...], w1_ref[...], precision=hp,
                              preferred_element_type=F32) + b1_ref[...]))
    for j in range(HY_INNER):
        h = jnp.sin(fq * (jnp.dot(h, wi_ref[j], precision=hp,
                                  preferred_element_type=F32) + bi_ref[j]))
    h_hi, h_lo = _split_bf16(h)
    w_hi, w_lo = _split_bf16(wo_ref[...])
    dot = functools.partial(jnp.dot, preferred_element_type=F32)
    filt = dot(h_hi, w_hi) + (dot(h_hi, w_lo) + dot(h_lo, w_hi))
    o_ref[...] = filt * jnp.exp(-offs_ref[...] * jnp.abs(decay_ref[...]))


def _hyena_filter(seq, hy_w, w1, b1, wi, bi, wo, freq, decay):
    zp, offs = _filter_features(seq)
    w1p = jnp.zeros((LANES, HY_FILTER_WIDTH), F32).at[:HY_EMB].set(w1.astype(F32))
    tr = 512
    fw = HY_FILTER_WIDTH
    const = lambda *shape: pl.BlockSpec(shape, lambda i: (0,) * len(shape))
    return pl.pallas_call(
        _filter_kernel,
        out_shape=jax.ShapeDtypeStruct((seq, hy_w), F32),
        grid=(seq // tr,),
        in_specs=[
            pl.BlockSpec((tr, LANES), lambda i: (i, 0)),
            pl.BlockSpec((tr, 1), lambda i: (i, 0)),
            const(LANES, fw), const(1, fw), const(HY_INNER, fw, fw),
            const(HY_INNER, 1, fw), const(fw, hy_w), const(1, fw), const(1, hy_w),
        ],
        out_specs=pl.BlockSpec((tr, hy_w), lambda i: (i, 0)),
        compiler_params=_cparams(("parallel",)),
        name="hyena_filter",
    )(jnp.asarray(zp), jnp.asarray(offs), w1p, b1.reshape(1, fw).astype(F32),
      wi.astype(F32), bi.reshape(HY_INNER, 1, fw).astype(F32), wo.astype(F32),
      freq.reshape(1, fw).astype(F32), decay.reshape(1, hy_w).astype(F32))


def _stack(c):
    return np.block([[c.real, -c.imag], [c.imag, c.real]])


def _fft_len(seq):
    return 3 * seq // 2


@functools.lru_cache(maxsize=None)
def _dft_constants(seq):
    n = _fft_len(seq)
    r = FFT_R
    a_n = n // r
    a_in = seq // r
    a_out = (seq // 2) // r
    ar = np.arange(a_n)
    br = np.arange(r)
    f1 = np.exp(-2j * np.pi * np.outer(ar, ar) / a_n)
    f1_fwd = _stack(f1[:, :a_in])
    f1_fwd_real = np.concatenate([f1[:, :a_in].real, f1[:, :a_in].imag], 0)
    f1_inv = _stack(np.conj(f1).T[a_out:a_out + a_in] / n)
    ph = (br[None, None, :] * br[None, :, None] / r
          + br[None, None, :] * ar[:, None, None] / n)
    g = np.exp(-2j * np.pi * ph)
    g_fwd = np.stack([_stack(g[k]) for k in range(a_n)])
    g_inv = np.stack([_stack(np.conj(g[k]).T) for k in range(a_n)])
    return (f1_fwd.astype(np.float32), f1_fwd_real.astype(np.float32),
            f1_inv.astype(np.float32), g_fwd.astype(np.float32), g_inv.astype(np.float32))


def _split_bf16(x):
    if isinstance(x, np.ndarray):
        hi = x.astype(BF16)
        lo = (x - hi.astype(np.float32)).astype(BF16)
        return jnp.asarray(hi), jnp.asarray(lo)
    hi = x.astype(BF16)
    return hi, (x - hi.astype(F32)).astype(BF16)


def _dot3(m_hi, m_lo, x):
    x_hi, x_lo = _split_bf16(x)
    dot = functools.partial(jnp.dot, preferred_element_type=F32)
    return dot(m_hi, x_hi) + (dot(m_hi, x_lo) + dot(m_lo, x_hi))


def _spectrum_kernel(filt_ref, f1h_ref, f1l_ref, gh_ref, gl_ref, hr_ref, hi_ref,
                     zs_ref, ts_ref, *, seq):
    r = FFT_R
    a_half = seq // r
    a_n = _fft_len(seq) // r
    for a in range(a_half):
        zs_ref[pl.ds(a * FFT_PITCH, r), :] = filt_ref[pl.ds(a * r, r), :]

    def stage1(bp, carry):
        rhs = jnp.concatenate(
            [zs_ref[pl.ds(2 * bp + e, a_half, stride=FFT_PITCH), :] for e in range(2)], axis=1)
        t = _dot3(f1h_ref[...], f1l_ref[...], rhs)
        for e in range(2):
            for c in range(2):
                ts_ref[c, pl.ds(2 * bp + e, a_n, stride=FFT_PITCH), :] = (
                    t[c * a_n:(c + 1) * a_n, e * LANES:(e + 1) * LANES])
        return carry

    lax.fori_loop(0, r // 2, stage1, 0, unroll=4)

    def stage2(ka, carry):
        rows = pl.ds(pl.multiple_of(ka * FFT_PITCH, 8), r)
        rhs = jnp.concatenate([ts_ref[0, rows, :], ts_ref[1, rows, :]], axis=0)
        s = _dot3(gh_ref[ka], gl_ref[ka], rhs)
        out = pl.ds(pl.multiple_of(ka * r, r), r)
        hr_ref[out, :] = s[:r]
        hi_ref[out, :] = s[r:]
        return carry

    lax.fori_loop(0, a_n, stage2, 0, unroll=16)


def _filter_spectrum(filt, seq):
    hy_w = filt.shape[1]
    n = _fft_len(seq)
    _, f1_real, _, g_fwd, _ = _dft_constants(seq)
    f1h, f1l = _split_bf16(f1_real)
    gh, gl = _split_bf16(g_fwd)
    a_n = n // FFT_R
    out = jax.ShapeDtypeStruct((n, hy_w), F32)
    gspec = pl.BlockSpec(g_fwd.shape, lambda c: (0, 0, 0), pipeline_mode=pl.Buffered(1))
    return pl.pallas_call(
        functools.partial(_spectrum_kernel, seq=seq),
        out_shape=(out, out),
        grid=(hy_w // LANES,),
        in_specs=[
            pl.BlockSpec((seq, LANES), lambda c: (0, c)),
            pl.BlockSpec(f1_real.shape, lambda c: (0, 0)),
            pl.BlockSpec(f1_real.shape, lambda c: (0, 0)),
            gspec, gspec,
        ],
        out_specs=(pl.BlockSpec((n, LANES), lambda c: (0, c)),
                   pl.BlockSpec((n, LANES), lambda c: (0, c))),
        scratch_shapes=[
            pltpu.VMEM((seq // FFT_R * FFT_PITCH, LANES), F32),
            pltpu.VMEM((2, a_n * FFT_PITCH, LANES), F32),
        ],
        compiler_params=_cparams(("parallel",)),
        name="filter_spectrum",
    )(filt, f1h, f1l, gh, gl)


def _hy_front_kernel(x0_ref, x1_ref, hv_ref, w0_ref, w1_ref, wv_ref, b0_ref, b1_ref, bv_ref,
                     z_ref, x0c_ref, pad_ref, *, seq):
    chunk = 512
    zero_row = jnp.zeros((8, LANES), F32)
    for idx, ref in enumerate((x0_ref, x1_ref, hv_ref)):
        pad_ref[idx, pl.ds(0, 8), :] = zero_row
        pad_ref[idx, pl.ds(8 + seq, 8), :] = zero_row
        pad_ref[idx, pl.ds(8, seq), :] = ref[...].astype(F32)

    def conv(idx, w_ref, b_ref, c):
        base = 8 + c * chunk
        w = w_ref[...]
        return (pad_ref[idx, pl.ds(base - 1, chunk), :] * w[0:1]
                + pad_ref[idx, pl.ds(base, chunk), :] * w[1:2]
                + pad_ref[idx, pl.ds(base + 1, chunk), :] * w[2:3]
                + b_ref[...])

    for c in range(seq // chunk):
        rows = pl.ds(c * chunk, chunk)
        x0c_ref[rows, :] = conv(0, w0_ref, b0_ref, c).astype(BF16)
        z_ref[rows, :] = (conv(2, wv_ref, bv_ref, c) * conv(1, w1_ref, b1_ref, c)).astype(BF16)


def _hyena_front(u_t, conv_w, conv_b, batch, seq, groups):
    ng = 3 * groups
    cw = conv_w.astype(F32).reshape(3, ng, LANES).transpose(1, 0, 2)
    cb = conv_b.astype(F32).reshape(ng, 1, LANES)
    uspec = lambda off: pl.BlockSpec((None, seq, LANES), lambda b, g: (off + g, b, 0))
    wspec = lambda off: pl.BlockSpec((None, 3, LANES), lambda b, g: (off + g, 0, 0))
    bspec = lambda off: pl.BlockSpec((None, 1, LANES), lambda b, g: (off + g, 0, 0))
    out = jax.ShapeDtypeStruct((groups, batch * seq, LANES), BF16)
    ospec = pl.BlockSpec((None, seq, LANES), lambda b, g: (g, b, 0))
    return pl.pallas_call(
        functools.partial(_hy_front_kernel, seq=seq),
        out_shape=(out, out),
        grid=(batch, groups),
        in_specs=[uspec(0), uspec(groups), uspec(2 * groups),
                  wspec(0), wspec(groups), wspec(2 * groups),
                  bspec(0), bspec(groups), bspec(2 * groups)],
        out_specs=(ospec, ospec),
        scratch_shapes=[pltpu.VMEM((3, seq + 16, LANES), F32)],
        compiler_params=_cparams(("parallel", "parallel")),
        name="hyena_front",
    )(u_t, u_t, u_t, cw, cw, cw, cb, cb, cb)


def _hy_conv_kernel(z_ref, x_ref, hr_ref, hi_ref, f1_ref, f1i_ref,
                    g_ref, gi_ref, bias_ref, gain_ref, o_ref,
                    zs_ref, ts_ref, ys_ref, *, seq):
    r = FFT_R
    a_half = seq // r
    a_n = _fft_len(seq) // r
    for c in range(2):
        for a in range(a_half):
            zs_ref[c, pl.ds(a * FFT_PITCH, r), :] = (
                z_ref[pl.ds(c * seq + a * r, r), :].astype(F32))

    def stage1(bp, carry):
        cols = []
        for e in range(2):
            rows = pl.ds(2 * bp + e, a_half, stride=FFT_PITCH)
            cols.append(jnp.concatenate([zs_ref[0, rows, :], zs_ref[1, rows, :]], axis=0))
        rhs = jnp.concatenate(cols, axis=1).astype(BF16)
        t = jnp.dot(f1_ref[...], rhs, preferred_element_type=F32)
        for e in range(2):
            for c in range(2):
                ts_ref[c, pl.ds(2 * bp + e, a_n, stride=FFT_PITCH), :] = (
                    t[c * a_n:(c + 1) * a_n, e * LANES:(e + 1) * LANES])
        return carry

    lax.fori_loop(0, r // 2, stage1, 0, unroll=8)

    def stage2(ka, carry):
        rows = pl.ds(pl.multiple_of(ka * FFT_PITCH, 8), r)
        rhs = jnp.concatenate([ts_ref[0, rows, :], ts_ref[1, rows, :]], axis=0).astype(BF16)
        s = jnp.dot(g_ref[ka], rhs, preferred_element_type=F32)
        hrows = pl.ds(pl.multiple_of(ka * r, r), r)
        hr = hr_ref[hrows, :]
        hi = hi_ref[hrows, :]
        sr, si = s[:r], s[r:]
        prod = jnp.concatenate([sr * hr - si * hi, sr * hi + si * hr], axis=0).astype(BF16)
        u = jnp.dot(gi_ref[ka], prod, preferred_element_type=F32)
        ts_ref[0, rows, :] = u[:r]
        ts_ref[1, rows, :] = u[r:]
        return carry

    lax.fori_loop(0, a_n, stage2, 0, unroll=32)

    def stage3(bp, carry):
        cols = []
        for e in range(2):
            rows = pl.ds(2 * bp + e, a_n, stride=FFT_PITCH)
            cols.append(jnp.concatenate([ts_ref[0, rows, :], ts_ref[1, rows, :]], axis=0))
        rhs = jnp.concatenate(cols, axis=1).astype(BF16)
        y = jnp.dot(f1i_ref[...], rhs, preferred_element_type=F32)
        for e in range(2):
            for c in range(2):
                ys_ref[c, pl.ds(2 * bp + e, a_half, stride=FFT_PITCH), :] = (
                    y[c * a_half:(c + 1) * a_half, e * LANES:(e + 1) * LANES])
        return carry

    lax.fori_loop(0, r // 2, stage3, 0, unroll=8)

    def finish(a, carry):
        prow = pl.ds(pl.multiple_of(a * FFT_PITCH, 8), r)
        for c in range(2):
            orow = pl.ds(pl.multiple_of(c * seq + a * r, r), r)
            z = ys_ref[c, prow, :] + zs_ref[c, prow, :] * bias_ref[...]
            y = z * x_ref[orow, :].astype(F32)
            o_ref[orow, :] = _rms(y, gain_ref[...]).astype(BF16)
        return carry

    lax.fori_loop(0, a_half, finish, 0, unroll=8)


def _hyena_conv(z_t, x0c_t, h_re, h_im, hy_bias, hy_out_norm, batch, seq, groups):
    n = _fft_len(seq)
    f1_fwd, _, f1_inv, g_fwd, g_inv = _dft_constants(seq)
    a_n = n // FFT_R
    zspec = pl.BlockSpec((None, 2 * seq, LANES), lambda g, p: (g, p, 0))
    hspec = pl.BlockSpec((n, LANES), lambda g, p: (0, g), pipeline_mode=pl.Buffered(1))
    cspec = lambda arr: pl.BlockSpec(arr.shape, lambda g, p: (0,) * arr.ndim,
                                     pipeline_mode=pl.Buffered(1))
    vspec = pl.BlockSpec((None, 1, LANES), lambda g, p: (g, 0, 0))
    out = jax.ShapeDtypeStruct((groups, batch * seq, LANES), BF16)
    return pl.pallas_call(
        functools.partial(_hy_conv_kernel, seq=seq),
        out_shape=out,
        grid=(groups, batch // 2),
        in_specs=[zspec, zspec, hspec, hspec,
                  cspec(f1_fwd), cspec(f1_inv), cspec(g_fwd), cspec(g_inv), vspec, vspec],
        out_specs=zspec,
        scratch_shapes=[
            pltpu.VMEM((2, seq // FFT_R * FFT_PITCH, LANES), F32),
            pltpu.VMEM((2, a_n * FFT_PITCH, LANES), F32),
            pltpu.VMEM((2, seq // FFT_R * FFT_PITCH, LANES), F32),
        ],
        compiler_params=_cparams(("parallel", "arbitrary")),
        name="hyena_conv",
    )(z_t, x0c_t, h_re, h_im,
      jnp.asarray(f1_fwd, BF16), jnp.asarray(f1_inv, BF16),
      jnp.asarray(g_fwd, BF16), jnp.asarray(g_inv, BF16),
      hy_bias.astype(F32).reshape(groups, 1, LANES),
      hy_out_norm.astype(F32).reshape(groups, 1, LANES))


def _out_proj_kernel(ya_ref, yh_ref, x_ref, w_ref, o_ref, *, heads, groups):
    lhs = jnp.concatenate([ya_ref[g] for g in range(heads)]
                          + [yh_ref[g] for g in range(groups)], axis=-1)
    o_ref[...] = x_ref[...] + jnp.dot(lhs, w_ref[...], preferred_element_type=F32)


def _out_proj(ya_t, yh_t, x2, w_out, tm):
    m, d = x2.shape
    heads, groups = ya_t.shape[0], yh_t.shape[0]
    return pl.pallas_call(
        functools.partial(_out_proj_kernel, heads=heads, groups=groups),
        out_shape=jax.ShapeDtypeStruct((m, d), F32),
        grid=(m // tm,),
        in_specs=[
            pl.BlockSpec((heads, tm, LANES), lambda i: (0, i, 0)),
            pl.BlockSpec((groups, tm, LANES), lambda i: (0, i, 0)),
            pl.BlockSpec((tm, d), lambda i: (i, 0)),
            pl.BlockSpec(w_out.shape, lambda i: (0, 0), pipeline_mode=pl.Buffered(1)),
        ],
        out_specs=pl.BlockSpec((tm, d), lambda i: (i, 0)),
        compiler_params=_cparams(("parallel",)),
        name="out_proj",
    )(ya_t, yh_t, x2, w_out.astype(BF16))


def _ffn_kernel(h_ref, g_ref, wa_ref, wg_ref, wd_ref, o_ref, xn_ref):
    def hidden_tile(xn):
        a = jnp.dot(xn, wa_ref[...], preferred_element_type=F32)
        g = jnp.dot(xn, wg_ref[...], preferred_element_type=F32)
        act = (a * jax.nn.sigmoid(a) * g).astype(BF16)
        return jnp.dot(act, wd_ref[...], preferred_element_type=F32)

    @pl.when(pl.program_id(1) == 0)
    def _():
        chunk = 256
        for c in range(h_ref.shape[0] // chunk):
            rows = pl.ds(c * chunk, chunk)
            h = h_ref[rows, :]
            xn = _rms(h, g_ref[...]).astype(BF16)
            xn_ref[rows, :] = xn
            o_ref[rows, :] = h + hidden_tile(xn)

    @pl.when(pl.program_id(1) > 0)
    def _():
        o_ref[...] += hidden_tile(xn_ref[...])


def _ffn(h1, norm2, w_gu, w_down, tm, th):
    m, d = h1.shape
    hidden = w_down.shape[0]
    nh = hidden // th
    w_gu = w_gu.astype(BF16)
    return pl.pallas_call(
        _ffn_kernel,
        out_shape=jax.ShapeDtypeStruct((m, d), F32),
        grid=(m // tm, nh),
        in_specs=[
            pl.BlockSpec((tm, d), lambda i, j: (i, 0)),
            pl.BlockSpec((1, d), lambda i, j: (0, 0)),
            pl.BlockSpec((d, th), lambda i, j: (0, j)),
            pl.BlockSpec((d, th), lambda i, j: (0, nh + j)),
            pl.BlockSpec((th, d), lambda i, j: (j, 0)),
        ],
        out_specs=pl.BlockSpec((tm, d), lambda i, j: (i, 0)),
        scratch_shapes=[pltpu.VMEM((tm, d), BF16)],
        compiler_params=_cparams(("parallel", "arbitrary")),
        name="ffn",
    )(h1, norm2.reshape(1, d), w_gu, w_gu, w_down.astype(BF16))


def _ple_kernel(h_ref, p_ref, gn_ref, wg_ref, wp_ref, pn_ref, o_ref):
    chunk = 256
    for c in range(h_ref.shape[0] // chunk):
        rows = pl.ds(c * chunk, chunk)
        h = h_ref[rows, :]
        e = _rms(jnp.dot(p_ref[rows, :].astype(BF16), wp_ref[...],
                         preferred_element_type=F32), pn_ref[...])
        inv = lax.rsqrt(jnp.mean(h * h, axis=-1, keepdims=True) + EPS)
        logits = jnp.dot((h * gn_ref[...]).astype(BF16), wg_ref[...],
                         preferred_element_type=F32)
        o_ref[rows, :] = h + jax.nn.sigmoid(logits * inv) * e


def _ple(h2, p2, ple_norm, w_gate, w_proj, ple_post_norm, tm):
    m, d = h2.shape
    pd = p2.shape[1]
    return pl.pallas_call(
        _ple_kernel,
        out_shape=jax.ShapeDtypeStruct((m, d), F32),
        grid=(m // tm,),
        in_specs=[
            pl.BlockSpec((tm, d), lambda i: (i, 0)),
            pl.BlockSpec((tm, pd), lambda i: (i, 0)),
            pl.BlockSpec((1, d), lambda i: (0, 0)),
            pl.BlockSpec((d, d), lambda i: (0, 0), pipeline_mode=pl.Buffered(1)),
            pl.BlockSpec((pd, d), lambda i: (0, 0), pipeline_mode=pl.Buffered(1)),
            pl.BlockSpec((1, d), lambda i: (0, 0)),
        ],
        out_specs=pl.BlockSpec((tm, d), lambda i: (i, 0)),
        compiler_params=_cparams(("parallel",)),
        name="ple",
    )(h2, p2, ple_norm.reshape(1, d), w_gate.astype(BF16), w_proj.astype(BF16),
      ple_post_norm.reshape(1, d))


def kernel(x, p, rel_bias, norm1, w_in, q_norm, k_norm, conv_w, conv_b, hy_w1, hy_b1, hy_wi, hy_bi, hy_wo, hy_freq, hy_decay, hy_bias, attn_out_norm, hy_out_norm, w_out, norm2, w_gu, w_down, ple_norm, w_ple_gate, w_ple_proj, ple_post_norm):
    batch, seq, d = x.shape
    attn_w = d // 2
    heads = attn_w // HEAD_DIM
    groups = heads
    hy_w = groups * LANES
    m = batch * seq
    tm = min(1024, m)
    th = 512
    h = x.reshape(m, d)
    bias_tab = _attn_bias_tables(rel_bias)
    for i in range(norm1.shape[0]):
        h3 = h.reshape(batch, seq, d)
        u_p, u_t = _in_proj(h3, norm1[i], w_in[i], q_norm[i], k_norm[i], attn_w, tm // 2)
        ya_t = _attention(u_p, bias_tab, attn_out_norm[i], batch, seq, heads)
        filt = _hyena_filter(seq, hy_w, hy_w1[i], hy_b1[i], hy_wi[i], hy_bi[i], hy_wo[i],
                             hy_freq[i], hy_decay[i])
        h_re, h_im = _filter_spectrum(filt, seq)
        z_t, x0c_t = _hyena_front(u_t, conv_w[i], conv_b[i], batch, seq, groups)
        yh_t = _hyena_conv(z_t, x0c_t, h_re, h_im, hy_bias[i], hy_out_norm[i],
                           batch, seq, groups)
        h = _out_proj(ya_t, yh_t, h, w_out[i], tm)
        h = _ffn(h, norm2[i], w_gu[i], w_down[i], tm, th)
        h = _ple(h, p[i].reshape(m, PLE_DIM), ple_norm[i], w_ple_gate[i], w_ple_proj[i],
                 ple_post_norm[i], tm)
    return h.reshape(batch, seq, d)
```

```python
import functools
import math

import jax
import jax.numpy as jnp
import numpy as np
from jax import lax
from jax.experimental import pallas as pl
from jax.experimental.pallas import tpu as pltpu

F32 = jnp.float32
BF16 = jnp.bfloat16

LANES = 128
HEAD_DIM = 128
ATTN_PATTERNS = ((128, 1), (512, 4), (2048, 16))
PERM = 16
HALO = 16
BAND = 64
QBLK = 128
KBLK = QBLK + 2 * BAND
N_BUCKETS = 32
REL_MAX_DIST = 1024
HY_EMB = 33
HY_FILTER_WIDTH = 64
HY_INNER = 2
EPS = 1e-6
NEG = -1e30
PLE_DIM = 256

FFT_R = 64
FFT_PITCH = FFT_R + 8
VMEM_LIMIT = 56 * 1024 * 1024


def _cparams(sem, vmem=VMEM_LIMIT):
    return pltpu.CompilerParams(dimension_semantics=sem, vmem_limit_bytes=vmem)


def _rms(x, gain):
    ms = jnp.mean(x * x, axis=-1, keepdims=True)
    return x * lax.rsqrt(ms + EPS) * gain


def _in_proj_kernel(x_ref, xa_ref, xb_ref, g_ref, w_ref, qg_ref, kg_ref, perm_ref,
                    cw_ref, cb_ref, oq_ref, z_ref, x0_ref, ext_ref, x1_ref, *,
                    attn_w, tn, tiles_per_seq):
    tm = x_ref.shape[0]
    chunk = PERM * PERM
    hpt = tn // LANES
    n_qkv = 3 * attn_w // tn
    dot = functools.partial(jnp.dot, preferred_element_type=F32)

    xn, xp = [], []
    for c in range(tm // chunk):
        xn.append(_rms(x_ref[pl.ds(c * chunk, chunk), :], g_ref[...]).astype(BF16))
        xp.append(dot(perm_ref[...], xn[c]).astype(BF16))

    for t in range(n_qkv):
        w = w_ref[:, t * tn:(t + 1) * tn]
        gain = (qg_ref[...] * (HEAD_DIM ** -0.5) if t * tn < attn_w
                else kg_ref[...] if t * tn < 2 * attn_w else None)
        for c in range(tm // chunk):
            acc = dot(xp[c], w)
            for hh in range(hpt):
                a = acc[:, hh * LANES:(hh + 1) * LANES]
                if gain is not None:
                    a = _rms(a, gain)
                a = a.astype(BF16)
                for r in range(PERM):
                    oq_ref[t * hpt + hh, r, pl.ds(c * PERM, PERM), :] = (
                        a[r * PERM:(r + 1) * PERM])

    tile = pl.program_id(0) % tiles_per_seq
    before = _rms(xa_ref[...], g_ref[...]) * jnp.where(tile == 0, 0.0, 1.0)
    after = _rms(xb_ref[...], g_ref[...]) * jnp.where(tile == tiles_per_seq - 1, 0.0, 1.0)
    xe = jnp.concatenate([before.astype(BF16)] + xn + [after.astype(BF16)], axis=0)
    for kind in range(3):
        col = (n_qkv + kind) * tn
        acc = dot(xe, w_ref[:, col:col + tn])
        for hh in range(hpt):
            lanes = slice(hh * LANES, (hh + 1) * LANES)
            ext_ref[hh] = acc[:, lanes]
            taps = cw_ref[:, kind * tn + hh * LANES:kind * tn + (hh + 1) * LANES]
            y = (ext_ref[hh, pl.ds(HALO - 1, tm), :] * taps[0:1]
                 + ext_ref[hh, pl.ds(HALO, tm), :] * taps[1:2]
                 + ext_ref[hh, pl.ds(HALO + 1, tm), :] * taps[2:3]
                 + cb_ref[:, kind * tn + hh * LANES:kind * tn + (hh + 1) * LANES])
            if kind == 0:
                x0_ref[hh] = y.astype(BF16)
            elif kind == 1:
                x1_ref[:, lanes] = y
            else:
                z_ref[hh] = (y * x1_ref[:, lanes]).astype(BF16)


def _in_proj(x3, norm1, w_in, q_norm, k_norm, conv_w, conv_b, attn_w, tm):
    batch, seq, d = x3.shape
    m = batch * seq
    in_w = w_in.shape[1]
    tn = min(1024, attn_w)
    g_qkv = 3 * attn_w // LANES
    groups = tn // LANES
    tiles_per_seq = seq // tm
    assert tm % (PERM * PERM) == 0 and seq % tm == 0 and tm % HALO == 0
    assert in_w == 3 * attn_w + 3 * tn, "one column tile per Hyena operand"
    perm = jnp.asarray(_to_sequence_order_matrix().T, BF16)
    const = lambda shape: pl.BlockSpec(shape, lambda i: (0,) * len(shape))
    halo_blocks = tm // HALO
    x2 = x3.reshape(m, d)
    hy_out = jax.ShapeDtypeStruct((groups, m, LANES), BF16)
    hy_spec = pl.BlockSpec((groups, tm, LANES), lambda i: (0, i, 0))
    u_q, z_t, x0c_t = pl.pallas_call(
        functools.partial(_in_proj_kernel, attn_w=attn_w, tn=tn, tiles_per_seq=tiles_per_seq),
        out_shape=(jax.ShapeDtypeStruct((g_qkv, batch, PERM, seq // PERM, LANES), BF16),
                   hy_out, hy_out),
        grid=(m // tm,),
        in_specs=[
            pl.BlockSpec((tm, d), lambda i: (i, 0)),
            pl.BlockSpec((HALO, d), lambda i: (jnp.maximum(i * halo_blocks - 1, 0), 0)),
            pl.BlockSpec((HALO, d),
                         lambda i: (jnp.minimum((i + 1) * halo_blocks, m // HALO - 1), 0)),
            const((1, d)),
            pl.BlockSpec((d, in_w), lambda i: (0, 0), pipeline_mode=pl.Buffered(1)),
            const((1, LANES)), const((1, LANES)), const(perm.shape),
            const(conv_w.shape), const((1, conv_b.shape[0])),
        ],
        out_specs=(
            pl.BlockSpec((g_qkv, None, PERM, tm // PERM, LANES),
                         lambda i: (0, i // tiles_per_seq, 0, i % tiles_per_seq, 0)),
            hy_spec, hy_spec,
        ),
        scratch_shapes=[pltpu.VMEM((groups, tm + 2 * HALO, LANES), F32),
                        pltpu.VMEM((tm, tn), F32)],
        compiler_params=_cparams(("parallel",)),
        name="in_proj",
    )(x2, x2, x2, norm1.reshape(1, d), w_in.astype(BF16), q_norm.reshape(1, LANES),
      k_norm.reshape(1, LANES), perm, conv_w.astype(F32), conv_b.astype(F32).reshape(1, -1))
    return u_q.reshape(g_qkv, m, LANES), z_t, x0c_t


def _t5_bucket(rel):
    half = N_BUCKETS // 2
    exact = half // 2
    n = np.abs(rel)
    large = exact + (np.log(np.maximum(n, 1).astype(np.float32) / np.float32(exact))
                     / np.float32(math.log(REL_MAX_DIST / exact))
                     * np.float32(half - exact)).astype(np.int32)
    large = np.minimum(large, half - 1)
    return np.where(rel > 0, half, 0) + np.where(n < exact, n, large)


def _block_orders(dil):
    sub = PERM // dil
    e, i = np.divmod(np.arange(QBLK), QBLK // sub)
    q_off = sub * i + e
    if dil == 1:
        k_off = np.arange(KBLK) - BAND
    else:
        e, i = np.divmod(np.arange(KBLK), KBLK // sub)
        k_off = sub * i + e - BAND
    return q_off, k_off


def _to_sequence_order_matrix():
    n = PERM * PERM
    mat = np.zeros((n, n), np.float32)
    r, i = np.divmod(np.arange(n), PERM)
    mat[PERM * i + r, np.arange(n)] = 1.0
    return mat


def _attn_bias_tables(rel_bias):
    tabs = []
    for _, dil in ATTN_PATTERNS:
        q_off, k_off = _block_orders(dil)
        rel = k_off[None, :] - q_off[:, None]
        band = np.abs(rel) <= BAND
        first_ok = np.broadcast_to(k_off[None, :] >= 0, rel.shape)
        last_ok = np.broadcast_to(k_off[None, :] < QBLK, rel.shape)
        onehot = np.eye(N_BUCKETS, dtype=np.float32)[_t5_bucket(rel * dil)]
        bias = jnp.einsum("qkn,nh->hqk", jnp.asarray(onehot), rel_bias.astype(F32),
                          precision=lax.Precision.HIGHEST)
        variants = [jnp.where(jnp.asarray(band & ok), bias, NEG)
                    for ok in (first_ok, np.ones_like(band), last_ok)]
        tabs.append(jnp.stack(variants, axis=1))
    return jnp.stack(tabs, axis=1)


def _attn_kernel(q_ref, k_ref, v_ref, bias_ref, gain_ref, perm_ref, o_ref,
                 qf_ref, kp_ref, vp_ref, kn_ref, vn_ref, acc_ref, m_ref, *, seq):
    run = seq // PERM
    chunk = PERM * PERM
    ones = jnp.ones((KBLK, LANES), BF16)
    zpad = jnp.zeros((BAND, LANES), BF16)

    for ref in (kp_ref, vp_ref, kn_ref, vn_ref):
        ref[pl.ds(0, BAND), :] = zpad
        ref[pl.ds(BAND + seq, BAND), :] = zpad
    kp_ref[pl.ds(BAND, seq), :] = k_ref[...]
    vp_ref[pl.ds(BAND, seq), :] = v_ref[...]
    qf_ref[...] = q_ref[...].astype(F32)

    def runs(ref, base, n):
        return jnp.concatenate(
            [ref[pl.ds(pl.multiple_of(r * run + base, n), n), :] for r in range(PERM)], axis=0)

    def to_sequence_order(c, carry):
        base = c * PERM
        kv = jnp.concatenate([runs(k_ref, base, PERM), runs(v_ref, base, PERM)], axis=1)
        nat = jnp.dot(perm_ref[...], kv, preferred_element_type=F32).astype(BF16)
        rows = pl.ds(pl.multiple_of(BAND + c * chunk, BAND), chunk)
        kn_ref[rows, :] = nat[:, :LANES]
        vn_ref[rows, :] = nat[:, LANES:]
        return carry

    lax.fori_loop(0, seq // chunk, to_sequence_order, 0, unroll=8)

    def softmax_block(q, k, v, bias):
        s = lax.dot_general(q, k, (((1,), (1,)), ((), ())), preferred_element_type=F32) + bias
        m_blk = jnp.max(s, axis=-1, keepdims=True)
        p = jnp.exp(s - m_blk).astype(BF16)
        a_l = jnp.dot(p, jnp.concatenate([v, ones], axis=1), preferred_element_type=F32)
        return a_l, jnp.broadcast_to(m_blk, (QBLK, LANES))

    def variant(qb, nb):
        return jnp.where(qb == 0, 0, jnp.where(qb == nb - 1, 2, 1))

    def merge(row_slices, a_l, m_b):
        m_old = jnp.concatenate([m_ref[rows, :] for rows in row_slices], axis=0)
        a_old = jnp.concatenate([acc_ref[rows, :] for rows in row_slices], axis=0)
        m_new = jnp.maximum(m_old, m_b)
        w_old = jnp.exp(m_old - m_new)
        w_blk = jnp.exp(m_b - m_new)
        a_new = (a_old * jnp.concatenate([w_old, w_old], axis=1)
                 + a_l * jnp.concatenate([w_blk, w_blk], axis=1))
        n = QBLK // len(row_slices)
        for e, rows in enumerate(row_slices):
            acc_ref[rows, :] = a_new[e * n:(e + 1) * n]
            m_ref[rows, :] = m_new[e * n:(e + 1) * n]

    nb16 = run // QBLK

    def block16(t, carry):
        r, qb = t // nb16, t % nb16
        off = pl.multiple_of(r * run + qb * QBLK, QBLK)
        a_l, m_b = softmax_block(q_ref[pl.ds(off, QBLK), :], kp_ref[pl.ds(off, KBLK), :],
                                 vp_ref[pl.ds(off, KBLK), :], bias_ref[2, variant(qb, nb16)])
        acc_ref[pl.ds(off, QBLK), :] = a_l
        m_ref[pl.ds(off, QBLK), :] = m_b
        return carry

    lax.fori_loop(0, PERM * nb16, block16, 0, unroll=16)

    dil = ATTN_PATTERNS[1][1]
    sub = PERM // dil
    nb4 = (seq // dil) // QBLK
    qn, kn = QBLK // sub, KBLK // sub

    def block4(t, carry):
        r, qb = t // nb4, t % nb4
        q_rows = [pl.ds(pl.multiple_of((dil * e + r) * run + qb * qn, qn), qn)
                  for e in range(sub)]
        k_rows = [pl.ds(pl.multiple_of((dil * e + r) * run + qb * qn + BAND - BAND // sub,
                                       BAND // sub), kn) for e in range(sub)]
        q = jnp.concatenate([q_ref[rows, :] for rows in q_rows], axis=0)
        k = jnp.concatenate([kp_ref[rows, :] for rows in k_rows], axis=0)
        v = jnp.concatenate([vp_ref[rows, :] for rows in k_rows], axis=0)
        a_l, m_b = softmax_block(q, k, v, bias_ref[1, variant(qb, nb4)])
        merge(q_rows, a_l, m_b)
        return carry

    lax.fori_loop(0, dil * nb4, block4, 0, unroll=16)

    nb1 = seq // QBLK
    qn1 = QBLK // PERM

    def block1(qb, carry):
        q_rows = [pl.ds(pl.multiple_of(r * run + qb * qn1, qn1), qn1) for r in range(PERM)]
        q = jnp.concatenate([qf_ref[rows, :] for rows in q_rows], axis=0).astype(BF16)
        k_rows = pl.ds(pl.multiple_of(qb * QBLK, QBLK), KBLK)
        a_l, m_b = softmax_block(q, kn_ref[k_rows, :], vn_ref[k_rows, :],
                                 bias_ref[0, variant(qb, nb1)])
        merge(q_rows, a_l, m_b)
        return carry

    lax.fori_loop(0, nb1, block1, 0, unroll=16)

    def finish(c, carry):
        st = runs(acc_ref, c * PERM, PERM)
        y = _rms(st[:, :LANES] / st[:, LANES:], gain_ref[...]).astype(BF16)
        nat = jnp.dot(perm_ref[...], y, preferred_element_type=F32)
        o_ref[pl.ds(pl.multiple_of(c * chunk, chunk), chunk), :] = nat.astype(BF16)
        return carry

    lax.fori_loop(0, seq // chunk, finish, 0, unroll=8)


def _attention(u_p, bias_tab, attn_out_norm, batch, seq, heads):
    assert ATTN_PATTERNS[0][1] == 1 and ATTN_PATTERNS[2][1] == PERM
    assert seq % (PERM * QBLK) == 0
    perm = jnp.asarray(_to_sequence_order_matrix(), BF16)
    pad_rows = seq + 2 * BAND
    return pl.pallas_call(
        functools.partial(_attn_kernel, seq=seq),
        out_shape=jax.ShapeDtypeStruct((heads, batch * seq, LANES), BF16),
        grid=(batch, heads),
        in_specs=[
            pl.BlockSpec((None, seq, LANES), lambda b, h: (h, b, 0)),
            pl.BlockSpec((None, seq, LANES), lambda b, h: (heads + h, b, 0)),
            pl.BlockSpec((None, seq, LANES), lambda b, h: (2 * heads + h, b, 0)),
            pl.BlockSpec((None, len(ATTN_PATTERNS), 3, QBLK, KBLK),
                         lambda b, h: (h, 0, 0, 0, 0)),
            pl.BlockSpec((None, 1, LANES), lambda b, h: (h, 0, 0)),
            pl.BlockSpec(perm.shape, lambda b, h: (0, 0)),
        ],
        out_specs=pl.BlockSpec((None, seq, LANES), lambda b, h: (h, b, 0)),
        scratch_shapes=[
            pltpu.VMEM((seq, LANES), F32),
            pltpu.VMEM((pad_rows, LANES), BF16),
            pltpu.VMEM((pad_rows, LANES), BF16),
            pltpu.VMEM((pad_rows, LANES), BF16),
            pltpu.VMEM((pad_rows, LANES), BF16),
            pltpu.VMEM((seq, 2 * LANES), F32),
            pltpu.VMEM((seq, LANES), F32),
        ],
        compiler_params=_cparams(("parallel", "parallel")),
        name="attention",
    )(u_p, u_p, u_p, bias_tab, attn_out_norm.reshape(heads, 1, LANES), perm)


def _filter_features(seq):
    pos = np.arange(seq, dtype=np.float32)
    t = pos / np.float32(max(seq - 1, 1))
    bands = (HY_EMB - 1) // 2
    fr = np.linspace(1e-4, bands - 1, bands, dtype=np.float32)
    ang = np.float32(2.0 * math.pi / seq) * pos[:, None] * fr[None, :]
    z = np.concatenate([t[:, None], np.cos(ang), -np.sin(ang)], axis=-1).astype(np.float32)
    zp = np.zeros((seq, LANES), np.float32)
    zp[:, :HY_EMB] = z
    offs = (np.abs(pos - (seq // 2)) / np.float32(seq / 2)).astype(np.float32)
    return zp, offs[:, None]


def _filter_kernel(z_ref, offs_ref, w1_ref, b1_ref, wi_ref, bi_ref, wo_ref, fq_ref,
                   decay_ref, o_ref):
    hp = lax.Precision.HIGHEST
    fq = fq_ref[...]
    h = jnp.sin(fq * (jnp.dot(z_ref[...], w1_ref[...], precision=hp,
                              preferred_element_type=F32) + b1_ref[...]))
    for j in range(HY_INNER):
        h = jnp.sin(fq * (jnp.dot(h, wi_ref[j], precision=hp,
                                  preferred_element_type=F32) + bi_ref[j]))
    h_hi, h_lo = _split_bf16(h)
    w_hi, w_lo = _split_bf16(wo_ref[...])
    dot = functools.partial(jnp.dot, preferred_element_type=F32)
    filt = dot(h_hi, w_hi) + (dot(h_hi, w_lo) + dot(h_lo, w_hi))
    o_ref[...] = filt * jnp.exp(-offs_ref[...] * jnp.abs(decay_ref[...]))


def _hyena_filter(seq, hy_w, w1, b1, wi, bi, wo, freq, decay):
    zp, offs = _filter_features(seq)
    w1p = jnp.zeros((LANES, HY_FILTER_WIDTH), F32).at[:HY_EMB].set(w1.astype(F32))
    tr = 512
    fw = HY_FILTER_WIDTH
    const = lambda *shape: pl.BlockSpec(shape, lambda i: (0,) * len(shape))
    return pl.pallas_call(
        _filter_kernel,
        out_shape=jax.ShapeDtypeStruct((seq, hy_w), F32),
        grid=(seq // tr,),
        in_specs=[
            pl.BlockSpec((tr, LANES), lambda i: (i, 0)),
            pl.BlockSpec((tr, 1), lambda i: (i, 0)),
            const(LANES, fw), const(1, fw), const(HY_INNER, fw, fw),
            const(HY_INNER, 1, fw), const(fw, hy_w), const(1, fw), const(1, hy_w),
        ],
        out_specs=pl.BlockSpec((tr, hy_w), lambda i: (i, 0)),
        compiler_params=_cparams(("parallel",)),
        name="hyena_filter",
    )(jnp.asarray(zp), jnp.asarray(offs), w1p, b1.reshape(1, fw).astype(F32),
      wi.astype(F32), bi.reshape(HY_INNER, 1, fw).astype(F32), wo.astype(F32),
      freq.reshape(1, fw).astype(F32), decay.reshape(1, hy_w).astype(F32))


def _stack(c):
    return np.block([[c.real, -c.imag], [c.imag, c.real]])


def _fft_len(seq):
    return 3 * seq // 2


@functools.lru_cache(maxsize=None)
def _dft_constants(seq):
    n = _fft_len(seq)
    r = FFT_R
    a_n = n // r
    a_in = seq // r
    a_out = (seq // 2) // r
    ar = np.arange(a_n)
    br = np.arange(r)
    f1 = np.exp(-2j * np.pi * np.outer(ar, ar) / a_n)
    f1_fwd = _stack(f1[:, :a_in])
    f1_fwd_real = np.concatenate([f1[:, :a_in].real, f1[:, :a_in].imag], 0)
    f1_inv = _stack(np.conj(f1).T[a_out:a_out + a_in] / n)
    ph = (br[None, None, :] * br[None, :, None] / r
          + br[None, None, :] * ar[:, None, None] / n)
    g = np.exp(-2j * np.pi * ph)
    g_fwd = np.stack([_stack(g[k]) for k in range(a_n)])
    g_inv = np.stack([_stack(np.conj(g[k]).T) for k in range(a_n)])
    return (f1_fwd.astype(np.float32), f1_fwd_real.astype(np.float32),
            f1_inv.astype(np.float32), g_fwd.astype(np.float32), g_inv.astype(np.float32))


def _split_bf16(x):
    if isinstance(x, np.ndarray):
        hi = x.astype(BF16)
        lo = (x - hi.astype(np.float32)).astype(BF16)
        return jnp.asarray(hi), jnp.asarray(lo)
    hi = x.astype(BF16)
    return hi, (x - hi.astype(F32)).astype(BF16)


def _dot3(m_hi, m_lo, x):
    x_hi, x_lo = _split_bf16(x)
    dot = functools.partial(jnp.dot, preferred_element_type=F32)
    return dot(m_hi, x_hi) + (dot(m_hi, x_lo) + dot(m_lo, x_hi))


def _spectrum_kernel(filt_ref, f1h_ref, f1l_ref, gh_ref, gl_ref, hr_ref, hi_ref,
                     zs_ref, ts_ref, *, seq):
    r = FFT_R
    a_half = seq // r
    a_n = _fft_len(seq) // r
    for a in range(a_half):
        zs_ref[pl.ds(a * FFT_PITCH, r), :] = filt_ref[pl.ds(a * r, r), :]

    def stage1(bp, carry):
        rhs = jnp.concatenate(
            [zs_ref[pl.ds(2 * bp + e, a_half, stride=FFT_PITCH), :] for e in range(2)], axis=1)
        t = _dot3(f1h_ref[...], f1l_ref[...], rhs)
        for e in range(2):
            for c in range(2):
                ts_ref[c, pl.ds(2 * bp + e, a_n, stride=FFT_PITCH), :] = (
                    t[c * a_n:(c + 1) * a_n, e * LANES:(e + 1) * LANES])
        return carry

    lax.fori_loop(0, r // 2, stage1, 0, unroll=4)

    def stage2(ka, carry):
        rows = pl.ds(pl.multiple_of(ka * FFT_PITCH, 8), r)
        rhs = jnp.concatenate([ts_ref[0, rows, :], ts_ref[1, rows, :]], axis=0)
        s = _dot3(gh_ref[ka], gl_ref[ka], rhs)
        out = pl.ds(pl.multiple_of(ka * r, r), r)
        hr_ref[out, :] = s[:r]
        hi_ref[out, :] = s[r:]
        return carry

    lax.fori_loop(0, a_n, stage2, 0, unroll=16)


def _filter_spectrum(filt, seq):
    hy_w = filt.shape[1]
    n = _fft_len(seq)
    _, f1_real, _, g_fwd, _ = _dft_constants(seq)
    f1h, f1l = _split_bf16(f1_real)
    gh, gl = _split_bf16(g_fwd)
    a_n = n // FFT_R
    out = jax.ShapeDtypeStruct((n, hy_w), F32)
    gspec = pl.BlockSpec(g_fwd.shape, lambda c: (0, 0, 0), pipeline_mode=pl.Buffered(1))
    return pl.pallas_call(
        functools.partial(_spectrum_kernel, seq=seq),
        out_shape=(out, out),
        grid=(hy_w // LANES,),
        in_specs=[
            pl.BlockSpec((seq, LANES), lambda c: (0, c)),
            pl.BlockSpec(f1_real.shape, lambda c: (0, 0)),
            pl.BlockSpec(f1_real.shape, lambda c: (0, 0)),
            gspec, gspec,
        ],
        out_specs=(pl.BlockSpec((n, LANES), lambda c: (0, c)),
                   pl.BlockSpec((n, LANES), lambda c: (0, c))),
        scratch_shapes=[
            pltpu.VMEM((seq // FFT_R * FFT_PITCH, LANES), F32),
            pltpu.VMEM((2, a_n * FFT_PITCH, LANES), F32),
        ],
        compiler_params=_cparams(("parallel",)),
        name="filter_spectrum",
    )(filt, f1h, f1l, gh, gl)


def _hy_conv_kernel(z_ref, x_ref, hr_ref, hi_ref, f1_ref, f1i_ref,
                    g_ref, gi_ref, bias_ref, gain_ref, o_ref,
                    zs_ref, ts_ref, ys_ref, *, seq):
    r = FFT_R
    a_half = seq // r
    a_n = _fft_len(seq) // r
    for c in range(2):
        for a in range(a_half):
            zs_ref[c, pl.ds(a * FFT_PITCH, r), :] = (
                z_ref[pl.ds(c * seq + a * r, r), :].astype(F32))

    def stage1(bp, carry):
        cols = []
        for e in range(2):
            rows = pl.ds(2 * bp + e, a_half, stride=FFT_PITCH)
            cols.append(jnp.concatenate([zs_ref[0, rows, :], zs_ref[1, rows, :]], axis=0))
        rhs = jnp.concatenate(cols, axis=1).astype(BF16)
        t = jnp.dot(f1_ref[...], rhs, preferred_element_type=F32)
        for e in range(2):
            for c in range(2):
                ts_ref[c, pl.ds(2 * bp + e, a_n, stride=FFT_PITCH), :] = (
                    t[c * a_n:(c + 1) * a_n, e * LANES:(e + 1) * LANES])
        return carry

    lax.fori_loop(0, r // 2, stage1, 0, unroll=8)

    def stage2(ka, carry):
        rows = pl.ds(pl.multiple_of(ka * FFT_PITCH, 8), r)
        rhs = jnp.concatenate([ts_ref[0, rows, :], ts_ref[1, rows, :]], axis=0).astype(BF16)
        s = jnp.dot(g_ref[ka], rhs, preferred_element_type=F32)
        hrows = pl.ds(pl.multiple_of(ka * r, r), r)
        hr = hr_ref[hrows, :]
        hi = hi_ref[hrows, :]
        sr, si = s[:r], s[r:]
        prod = jnp.concatenate([sr * hr - si * hi, sr * hi + si * hr], axis=0).astype(BF16)
        u = jnp.dot(gi_ref[ka], prod, preferred_element_type=F32)
        ts_ref[0, rows, :] = u[:r]
        ts_ref[1, rows, :] = u[r:]
        return carry

    lax.fori_loop(0, a_n, stage2, 0, unroll=32)

    def stage3(bp, carry):
        cols = []
        for e in range(2):
            rows = pl.ds(2 * bp + e, a_n, stride=FFT_PITCH)
            cols.append(jnp.concatenate([ts_ref[0, rows, :], ts_ref[1, rows, :]], axis=0))
        rhs = jnp.concatenate(cols, axis=1).astype(BF16)
        y = jnp.dot(f1i_ref[...], rhs, preferred_element_type=F32)
        for e in range(2):
            for c in range(2):
                ys_ref[c, pl.ds(2 * bp + e, a_half, stride=FFT_PITCH), :] = (
                    y[c * a_half:(c + 1) * a_half, e * LANES:(e + 1) * LANES])
        return carry

    lax.fori_loop(0, r // 2, stage3, 0, unroll=8)

    def finish(a, carry):
        prow = pl.ds(pl.multiple_of(a * FFT_PITCH, 8), r)
        for c in range(2):
            orow = pl.ds(pl.multiple_of(c * seq + a * r, r), r)
            z = ys_ref[c, prow, :] + zs_ref[c, prow, :] * bias_ref[...]
            y = z * x_ref[orow, :].astype(F32)
            o_ref[orow, :] = _rms(y, gain_ref[...]).astype(BF16)
        return carry

    lax.fori_loop(0, a_half, finish, 0, unroll=8)


def _hyena_conv(z_t, x0c_t, h_re, h_im, hy_bias, hy_out_norm, batch, seq, groups):
    n = _fft_len(seq)
    f1_fwd, _, f1_inv, g_fwd, g_inv = _dft_constants(seq)
    a_n = n // FFT_R
    zspec = pl.BlockSpec((None, 2 * seq, LANES), lambda g, p: (g, p, 0))
    hspec = pl.BlockSpec((n, LANES), lambda g, p: (0, g))
    cspec = lambda arr: pl.BlockSpec(arr.shape, lambda g, p: (0,) * arr.ndim,
                                     pipeline_mode=pl.Buffered(1))
    vspec = pl.BlockSpec((None, 1, LANES), lambda g, p: (g, 0, 0))
    out = jax.ShapeDtypeStruct((groups, batch * seq, LANES), BF16)
    return pl.pallas_call(
        functools.partial(_hy_conv_kernel, seq=seq),
        out_shape=out,
        grid=(groups, batch // 2),
        in_specs=[zspec, zspec, hspec, hspec,
                  cspec(f1_fwd), cspec(f1_inv), cspec(g_fwd), cspec(g_inv), vspec, vspec],
        out_specs=zspec,
        scratch_shapes=[
            pltpu.VMEM((2, seq // FFT_R * FFT_PITCH, LANES), F32),
            pltpu.VMEM((2, a_n * FFT_PITCH, LANES), F32),
            pltpu.VMEM((2, seq // FFT_R * FFT_PITCH, LANES), F32),
        ],
        compiler_params=_cparams(("parallel", "arbitrary")),
        name="hyena_conv",
    )(z_t, x0c_t, h_re, h_im,
      jnp.asarray(f1_fwd, BF16), jnp.asarray(f1_inv, BF16),
      jnp.asarray(g_fwd, BF16), jnp.asarray(g_inv, BF16),
      hy_bias.astype(F32).reshape(groups, 1, LANES),
      hy_out_norm.astype(F32).reshape(groups, 1, LANES))


def _out_proj_kernel(ya_ref, yh_ref, x_ref, w_ref, o_ref, *, heads, groups):
    lhs = jnp.concatenate([ya_ref[g] for g in range(heads)]
                          + [yh_ref[g] for g in range(groups)], axis=-1)
    o_ref[...] = x_ref[...] + jnp.dot(lhs, w_ref[...], preferred_element_type=F32)


def _out_proj(ya_t, yh_t, x2, w_out, tm):
    m, d = x2.shape
    heads, groups = ya_t.shape[0], yh_t.shape[0]
    return pl.pallas_call(
        functools.partial(_out_proj_kernel, heads=heads, groups=groups),
        out_shape=jax.ShapeDtypeStruct((m, d), F32),
        grid=(m // tm,),
        in_specs=[
            pl.BlockSpec((heads, tm, LANES), lambda i: (0, i, 0)),
            pl.BlockSpec((groups, tm, LANES), lambda i: (0, i, 0)),
            pl.BlockSpec((tm, d), lambda i: (i, 0)),
            pl.BlockSpec(w_out.shape, lambda i: (0, 0), pipeline_mode=pl.Buffered(1)),
        ],
        out_specs=pl.BlockSpec((tm, d), lambda i: (i, 0)),
        compiler_params=_cparams(("parallel",)),
        name="out_proj",
    )(ya_t, yh_t, x2, w_out.astype(BF16))


def _ffn_kernel(h_ref, g_ref, wa_ref, wg_ref, wd_ref, o_ref, xn_ref):
    def hidden_tile(xn):
        a = jnp.dot(xn, wa_ref[...], preferred_element_type=F32)
        g = jnp.dot(xn, wg_ref[...], preferred_element_type=F32)
        act = (a * jax.nn.sigmoid(a) * g).astype(BF16)
        return jnp.dot(act, wd_ref[...], preferred_element_type=F32)

    @pl.when(pl.program_id(1) == 0)
    def _():
        chunk = 256
        for c in range(h_ref.shape[0] // chunk):
            rows = pl.ds(c * chunk, chunk)
            h = h_ref[rows, :]
            xn = _rms(h, g_ref[...]).astype(BF16)
            xn_ref[rows, :] = xn
            o_ref[rows, :] = h + hidden_tile(xn)

    @pl.when(pl.program_id(1) > 0)
    def _():
        o_ref[...] += hidden_tile(xn_ref[...])


def _ffn(h1, norm2, w_gu, w_down, tm, th):
    m, d = h1.shape
    hidden = w_down.shape[0]
    nh = hidden // th
    w_gu = w_gu.astype(BF16)
    return pl.pallas_call(
        _ffn_kernel,
        out_shape=jax.ShapeDtypeStruct((m, d), F32),
        grid=(m // tm, nh),
        in_specs=[
            pl.BlockSpec((tm, d), lambda i, j: (i, 0)),
            pl.BlockSpec((1, d), lambda i, j: (0, 0)),
            pl.BlockSpec((d, th), lambda i, j: (0, j)),
            pl.BlockSpec((d, th), lambda i, j: (0, nh + j)),
            pl.BlockSpec((th, d), lambda i, j: (j, 0)),
        ],
        out_specs=pl.BlockSpec((tm, d), lambda i, j: (i, 0)),
        scratch_shapes=[pltpu.VMEM((tm, d), BF16)],
        compiler_params=_cparams(("parallel", "arbitrary")),
        name="ffn",
    )(h1, norm2.reshape(1, d), w_gu, w_gu, w_down.astype(BF16))


def _ple_kernel(h_ref, p_ref, gn_ref, wg_ref, wp_ref, pn_ref, o_ref):
    chunk = 256
    for c in range(h_ref.shape[0] // chunk):
        rows = pl.ds(c * chunk, chunk)
        h = h_ref[rows, :]
        e = _rms(jnp.dot(p_ref[rows, :].astype(BF16), wp_ref[...],
                         preferred_element_type=F32), pn_ref[...])
        inv = lax.rsqrt(jnp.mean(h * h, axis=-1, keepdims=True) + EPS)
        logits = jnp.dot((h * gn_ref[...]).astype(BF16), wg_ref[...],
                         preferred_element_type=F32)
        o_ref[rows, :] = h + jax.nn.sigmoid(logits * inv) * e


def _ple(h2, p2, ple_norm, w_gate, w_proj, ple_post_norm, tm):
    m, d = h2.shape
    pd = p2.shape[1]
    return pl.pallas_call(
        _ple_kernel,
        out_shape=jax.ShapeDtypeStruct((m, d), F32),
        grid=(m // tm,),
        in_specs=[
            pl.BlockSpec((tm, d), lambda i: (i, 0)),
            pl.BlockSpec((tm, pd), lambda i: (i, 0)),
            pl.BlockSpec((1, d), lambda i: (0, 0)),
            pl.BlockSpec((d, d), lambda i: (0, 0), pipeline_mode=pl.Buffered(1)),
            pl.BlockSpec((pd, d), lambda i: (0, 0), pipeline_mode=pl.Buffered(1)),
            pl.BlockSpec((1, d), lambda i: (0, 0)),
        ],
        out_specs=pl.BlockSpec((tm, d), lambda i: (i, 0)),
        compiler_params=_cparams(("parallel",)),
        name="ple",
    )(h2, p2, ple_norm.reshape(1, d), w_gate.astype(BF16), w_proj.astype(BF16),
      ple_post_norm.reshape(1, d))


def kernel(x, p, rel_bias, norm1, w_in, q_norm, k_norm, conv_w, conv_b, hy_w1, hy_b1, hy_wi, hy_bi, hy_wo, hy_freq, hy_decay, hy_bias, attn_out_norm, hy_out_norm, w_out, norm2, w_gu, w_down, ple_norm, w_ple_gate, w_ple_proj, ple_post_norm):
    batch, seq, d = x.shape
    attn_w = d // 2
    heads = attn_w // HEAD_DIM
    groups = heads
    hy_w = groups * LANES
    m = batch * seq
    tm = min(1024, m)
    th = 512
    h = x.reshape(m, d)
    bias_tab = _attn_bias_tables(rel_bias)
    for i in range(norm1.shape[0]):
        h3 = h.reshape(batch, seq, d)
        u_p, z_t, x0c_t = _in_proj(h3, norm1[i], w_in[i], q_norm[i], k_norm[i], conv_w[i],
                                   conv_b[i], attn_w, tm // 2)
        ya_t = _attention(u_p, bias_tab, attn_out_norm[i], batch, seq, heads)
        filt = _hyena_filter(seq, hy_w, hy_w1[i], hy_b1[i], hy_wi[i], hy_bi[i], hy_wo[i],
                             hy_freq[i], hy_decay[i])
        h_re, h_im = _filter_spectrum(filt, seq)
        yh_t = _hyena_conv(z_t, x0c_t, h_re, h_im, hy_bias[i], hy_out_norm[i],
                           batch, seq, groups)
        h = _out_proj(ya_t, yh_t, h, w_out[i], tm)
        h = _ffn(h, norm2[i], w_gu[i], w_down[i], tm, th)
        h = _ple(h, p[i].reshape(m, PLE_DIM), ple_norm[i], w_ple_gate[i], w_ple_proj[i],
                 ple_post_norm[i], tm)
    return h.reshape(batch, seq, d)
```

```python
import functools
import math

import jax
import jax.numpy as jnp
import numpy as np
from jax import lax
from jax.experimental import pallas as pl
from jax.experimental.pallas import tpu as pltpu

F32 = jnp.float32
BF16 = jnp.bfloat16

LANES = 128
HEAD_DIM = 128
ATTN_PATTERNS = ((128, 1), (512, 4), (2048, 16))
PERM = 16
HALO = 16
BAND = 64
QBLK = 128
KBLK = QBLK + 2 * BAND
N_BUCKETS = 32
REL_MAX_DIST = 1024
HY_EMB = 33
HY_FILTER_WIDTH = 64
HY_INNER = 2
EPS = 1e-6
NEG = -1e30
PLE_DIM = 256

FFT_R = 64
FFT_PITCH = FFT_R + 8
VMEM_LIMIT = 56 * 1024 * 1024


def _cparams(sem, vmem=VMEM_LIMIT):
    return pltpu.CompilerParams(dimension_semantics=sem, vmem_limit_bytes=vmem)


def _rms(x, gain):
    ms = jnp.mean(x * x, axis=-1, keepdims=True)
    return x * lax.rsqrt(ms + EPS) * gain


def _in_proj_kernel(x_ref, xa_ref, xb_ref, g_ref, w_ref, qg_ref, kg_ref, perm_ref,
                    cw_ref, cb_ref, oq_ref, z_ref, x0_ref, ext_ref, x1_ref, *,
                    attn_w, tn, tiles_per_seq):
    tm = x_ref.shape[0]
    chunk = PERM * PERM
    hpt = tn // LANES
    n_qkv = 3 * attn_w // tn
    dot = functools.partial(jnp.dot, preferred_element_type=F32)

    xn, xp = [], []
    for c in range(tm // chunk):
        xn.append(_rms(x_ref[pl.ds(c * chunk, chunk), :], g_ref[...]).astype(BF16))
        xp.append(dot(perm_ref[...], xn[c]).astype(BF16))

    for t in range(n_qkv):
        w = w_ref[:, t * tn:(t + 1) * tn]
        gain = (qg_ref[...] * (HEAD_DIM ** -0.5) if t * tn < attn_w
                else kg_ref[...] if t * tn < 2 * attn_w else None)
        for c in range(tm // chunk):
            acc = dot(xp[c], w)
            for hh in range(hpt):
                a = acc[:, hh * LANES:(hh + 1) * LANES]
                if gain is not None:
                    a = _rms(a, gain)
                a = a.astype(BF16)
                for r in range(PERM):
                    oq_ref[t * hpt + hh, r, pl.ds(c * PERM, PERM), :] = (
                        a[r * PERM:(r + 1) * PERM])

    tile = pl.program_id(0) % tiles_per_seq
    before = _rms(xa_ref[...], g_ref[...]) * jnp.where(tile == 0, 0.0, 1.0)
    after = _rms(xb_ref[...], g_ref[...]) * jnp.where(tile == tiles_per_seq - 1, 0.0, 1.0)
    xe = jnp.concatenate([before.astype(BF16)] + xn + [after.astype(BF16)], axis=0)
    for kind in range(3):
        col = (n_qkv + kind) * tn
        acc = dot(xe, w_ref[:, col:col + tn])
        for hh in range(hpt):
            lanes = slice(hh * LANES, (hh + 1) * LANES)
            ext_ref[hh] = acc[:, lanes]
            taps = cw_ref[:, kind * tn + hh * LANES:kind * tn + (hh + 1) * LANES]
            y = (ext_ref[hh, pl.ds(HALO - 1, tm), :] * taps[0:1]
                 + ext_ref[hh, pl.ds(HALO, tm), :] * taps[1:2]
                 + ext_ref[hh, pl.ds(HALO + 1, tm), :] * taps[2:3]
                 + cb_ref[:, kind * tn + hh * LANES:kind * tn + (hh + 1) * LANES])
            if kind == 0:
                x0_ref[hh] = y.astype(BF16)
            elif kind == 1:
                x1_ref[:, lanes] = y
            else:
                z_ref[hh] = (y * x1_ref[:, lanes]).astype(BF16)


def _in_proj(x3, norm1, w_in, q_norm, k_norm, conv_w, conv_b, attn_w, tm):
    batch, seq, d = x3.shape
    m = batch * seq
    in_w = w_in.shape[1]
    tn = min(1024, attn_w)
    g_qkv = 3 * attn_w // LANES
    groups = tn // LANES
    tiles_per_seq = seq // tm
    assert tm % (PERM * PERM) == 0 and seq % tm == 0 and tm % HALO == 0
    assert in_w == 3 * attn_w + 3 * tn, "one column tile per Hyena operand"
    perm = jnp.asarray(_to_sequence_order_matrix().T, BF16)
    const = lambda shape: pl.BlockSpec(shape, lambda i: (0,) * len(shape))
    halo_blocks = tm // HALO
    x2 = x3.reshape(m, d)
    hy_out = jax.ShapeDtypeStruct((groups, m, LANES), BF16)
    hy_spec = pl.BlockSpec((groups, tm, LANES), lambda i: (0, i, 0))
    u_q, z_t, x0c_t = pl.pallas_call(
        functools.partial(_in_proj_kernel, attn_w=attn_w, tn=tn, tiles_per_seq=tiles_per_seq),
        out_shape=(jax.ShapeDtypeStruct((g_qkv, batch, PERM, seq // PERM, LANES), BF16),
                   hy_out, hy_out),
        grid=(m // tm,),
        in_specs=[
            pl.BlockSpec((tm, d), lambda i: (i, 0)),
            pl.BlockSpec((HALO, d), lambda i: (jnp.maximum(i * halo_blocks - 1, 0), 0)),
            pl.BlockSpec((HALO, d),
                         lambda i: (jnp.minimum((i + 1) * halo_blocks, m // HALO - 1), 0)),
            const((1, d)),
            pl.BlockSpec((d, in_w), lambda i: (0, 0), pipeline_mode=pl.Buffered(1)),
            const((1, LANES)), const((1, LANES)), const(perm.shape),
            const(conv_w.shape), const((1, conv_b.shape[0])),
        ],
        out_specs=(
            pl.BlockSpec((g_qkv, None, PERM, tm // PERM, LANES),
                         lambda i: (0, i // tiles_per_seq, 0, i % tiles_per_seq, 0)),
            hy_spec, hy_spec,
        ),
        scratch_shapes=[pltpu.VMEM((groups, tm + 2 * HALO, LANES), F32),
                        pltpu.VMEM((tm, tn), F32)],
        compiler_params=_cparams(("parallel",)),
        name="in_proj",
    )(x2, x2, x2, norm1.reshape(1, d), w_in.astype(BF16), q_norm.reshape(1, LANES),
      k_norm.reshape(1, LANES), perm, conv_w.astype(F32), conv_b.astype(F32).reshape(1, -1))
    return u_q.reshape(g_qkv, m, LANES), z_t, x0c_t


def _t5_bucket(rel):
    half = N_BUCKETS // 2
    exact = half // 2
    n = np.abs(rel)
    large = exact + (np.log(np.maximum(n, 1).astype(np.float32) / np.float32(exact))
                     / np.float32(math.log(REL_MAX_DIST / exact))
                     * np.float32(half - exact)).astype(np.int32)
    large = np.minimum(large, half - 1)
    return np.where(rel > 0, half, 0) + np.where(n < exact, n, large)


def _block_orders(dil):
    sub = PERM // dil
    e, i = np.divmod(np.arange(QBLK), QBLK // sub)
    q_off = sub * i + e
    if dil == 1:
        k_off = np.arange(KBLK) - BAND
    else:
        e, i = np.divmod(np.arange(KBLK), KBLK // sub)
        k_off = sub * i + e - BAND
    return q_off, k_off


def _to_sequence_order_matrix():
    n = PERM * PERM
    mat = np.zeros((n, n), np.float32)
    r, i = np.divmod(np.arange(n), PERM)
    mat[PERM * i + r, np.arange(n)] = 1.0
    return mat


def _attn_bias_tables(rel_bias):
    tabs = []
    for _, dil in ATTN_PATTERNS:
        q_off, k_off = _block_orders(dil)
        rel = k_off[None, :] - q_off[:, None]
        band = np.abs(rel) <= BAND
        first_ok = np.broadcast_to(k_off[None, :] >= 0, rel.shape)
        last_ok = np.broadcast_to(k_off[None, :] < QBLK, rel.shape)
        onehot = np.eye(N_BUCKETS, dtype=np.float32)[_t5_bucket(rel * dil)]
        bias = jnp.einsum("qkn,nh->hqk", jnp.asarray(onehot), rel_bias.astype(F32),
                          precision=lax.Precision.HIGHEST)
        variants = [jnp.where(jnp.asarray(band & ok), bias, NEG)
                    for ok in (first_ok, np.ones_like(band), last_ok)]
        tabs.append(jnp.stack(variants, axis=1))
    return jnp.stack(tabs, axis=1)


def _attn_kernel(*refs, seq, n_cast):
    q_ref, k_ref, v_ref, bias_ref, gain_ref, perm_ref = refs[:6]
    cast_in = refs[6:6 + n_cast]
    o_ref = refs[6 + n_cast]
    cast_out = refs[7 + n_cast:7 + 2 * n_cast]
    qf_ref, kp_ref, vp_ref, kn_ref, vn_ref, acc_ref, m_ref = refs[7 + 2 * n_cast:]

    for src, dst in zip(cast_in, cast_out):
        dst[...] = src[...].astype(BF16)

    run = seq // PERM
    chunk = PERM * PERM
    ones = jnp.ones((KBLK, LANES), BF16)
    zpad = jnp.zeros((BAND, LANES), BF16)

    for ref in (kp_ref, vp_ref, kn_ref, vn_ref):
        ref[pl.ds(0, BAND), :] = zpad
        ref[pl.ds(BAND + seq, BAND), :] = zpad
    kp_ref[pl.ds(BAND, seq), :] = k_ref[...]
    vp_ref[pl.ds(BAND, seq), :] = v_ref[...]
    qf_ref[...] = q_ref[...].astype(F32)

    def runs(ref, base, n):
        return jnp.concatenate(
            [ref[pl.ds(pl.multiple_of(r * run + base, n), n), :] for r in range(PERM)], axis=0)

    def to_sequence_order(c, carry):
        base = c * PERM
        kv = jnp.concatenate([runs(k_ref, base, PERM), runs(v_ref, base, PERM)], axis=1)
        nat = jnp.dot(perm_ref[...], kv, preferred_element_type=F32).astype(BF16)
        rows = pl.ds(pl.multiple_of(BAND + c * chunk, BAND), chunk)
        kn_ref[rows, :] = nat[:, :LANES]
        vn_ref[rows, :] = nat[:, LANES:]
        return carry

    lax.fori_loop(0, seq // chunk, to_sequence_order, 0, unroll=8)

    def softmax_block(q, k, v, bias):
        s = lax.dot_general(q, k, (((1,), (1,)), ((), ())), preferred_element_type=F32) + bias
        m_blk = jnp.max(s, axis=-1, keepdims=True)
        p = jnp.exp(s - m_blk).astype(BF16)
        a_l = jnp.dot(p, jnp.concatenate([v, ones], axis=1), preferred_element_type=F32)
        return a_l, jnp.broadcast_to(m_blk, (QBLK, LANES))

    def variant(qb, nb):
        return jnp.where(qb == 0, 0, jnp.where(qb == nb - 1, 2, 1))

    def merge(row_slices, a_l, m_b):
        m_old = jnp.concatenate([m_ref[rows, :] for rows in row_slices], axis=0)
        a_old = jnp.concatenate([acc_ref[rows, :] for rows in row_slices], axis=0)
        m_new = jnp.maximum(m_old, m_b)
        w_old = jnp.exp(m_old - m_new)
        w_blk = jnp.exp(m_b - m_new)
        a_new = (a_old * jnp.concatenate([w_old, w_old], axis=1)
                 + a_l * jnp.concatenate([w_blk, w_blk], axis=1))
        n = QBLK // len(row_slices)
        for e, rows in enumerate(row_slices):
            acc_ref[rows, :] = a_new[e * n:(e + 1) * n]
            m_ref[rows, :] = m_new[e * n:(e + 1) * n]

    nb16 = run // QBLK

    def block16(t, carry):
        r, qb = t // nb16, t % nb16
        off = pl.multiple_of(r * run + qb * QBLK, QBLK)
        a_l, m_b = softmax_block(q_ref[pl.ds(off, QBLK), :], kp_ref[pl.ds(off, KBLK), :],
                                 vp_ref[pl.ds(off, KBLK), :], bias_ref[2, variant(qb, nb16)])
        acc_ref[pl.ds(off, QBLK), :] = a_l
        m_ref[pl.ds(off, QBLK), :] = m_b
        return carry

    lax.fori_loop(0, PERM * nb16, block16, 0, unroll=16)

    dil = ATTN_PATTERNS[1][1]
    sub = PERM // dil
    nb4 = (seq // dil) // QBLK
    qn, kn = QBLK // sub, KBLK // sub

    def block4(t, carry):
        r, qb = t // nb4, t % nb4
        q_rows = [pl.ds(pl.multiple_of((dil * e + r) * run + qb * qn, qn), qn)
                  for e in range(sub)]
        k_rows = [pl.ds(pl.multiple_of((dil * e + r) * run + qb * qn + BAND - BAND // sub,
                                       BAND // sub), kn) for e in range(sub)]
        q = jnp.concatenate([q_ref[rows, :] for rows in q_rows], axis=0)
        k = jnp.concatenate([kp_ref[rows, :] for rows in k_rows], axis=0)
        v = jnp.concatenate([vp_ref[rows, :] for rows in k_rows], axis=0)
        a_l, m_b = softmax_block(q, k, v, bias_ref[1, variant(qb, nb4)])
        merge(q_rows, a_l, m_b)
        return carry

    lax.fori_loop(0, dil * nb4, block4, 0, unroll=16)

    nb1 = seq // QBLK
    qn1 = QBLK // PERM

    def block1(qb, carry):
        q_rows = [pl.ds(pl.multiple_of(r * run + qb * qn1, qn1), qn1) for r in range(PERM)]
        q = jnp.concatenate([qf_ref[rows, :] for rows in q_rows], axis=0).astype(BF16)
        k_rows = pl.ds(pl.multiple_of(qb * QBLK, QBLK), KBLK)
        a_l, m_b = softmax_block(q, kn_ref[k_rows, :], vn_ref[k_rows, :],
                                 bias_ref[0, variant(qb, nb1)])
        merge(q_rows, a_l, m_b)
        return carry

    lax.fori_loop(0, nb1, block1, 0, unroll=16)

    def finish(c, carry):
        st = runs(acc_ref, c * PERM, PERM)
        y = _rms(st[:, :LANES] / st[:, LANES:], gain_ref[...]).astype(BF16)
        nat = jnp.dot(perm_ref[...], y, preferred_element_type=F32)
        o_ref[pl.ds(pl.multiple_of(c * chunk, chunk), chunk), :] = nat.astype(BF16)
        return carry

    lax.fori_loop(0, seq // chunk, finish, 0, unroll=8)


def _cast_slices(weights, n_steps, step_of):
    specs = []
    for w in weights:
        rows, hold = w.shape[0], 1
        while (rows * hold) % n_steps or (rows * hold // n_steps) % 16:
            hold *= 2
        specs.append(pl.BlockSpec((rows * hold // n_steps, w.shape[1]),
                                  lambda *ids, hold=hold: (step_of(*ids) // hold, 0)))
    return specs


def _attention(u_p, bias_tab, attn_out_norm, batch, seq, heads, cast_weights):
    assert ATTN_PATTERNS[0][1] == 1 and ATTN_PATTERNS[2][1] == PERM
    assert seq % (PERM * QBLK) == 0
    perm = jnp.asarray(_to_sequence_order_matrix(), BF16)
    pad_rows = seq + 2 * BAND
    cast_specs = _cast_slices(cast_weights, batch * heads, lambda b, h: b * heads + h)
    outs = pl.pallas_call(
        functools.partial(_attn_kernel, seq=seq, n_cast=len(cast_weights)),
        out_shape=[jax.ShapeDtypeStruct((heads, batch * seq, LANES), BF16)]
        + [jax.ShapeDtypeStruct(w.shape, BF16) for w in cast_weights],
        grid=(batch, heads),
        in_specs=[
            pl.BlockSpec((None, seq, LANES), lambda b, h: (h, b, 0)),
            pl.BlockSpec((None, seq, LANES), lambda b, h: (heads + h, b, 0)),
            pl.BlockSpec((None, seq, LANES), lambda b, h: (2 * heads + h, b, 0)),
            pl.BlockSpec((None, len(ATTN_PATTERNS), 3, QBLK, KBLK),
                         lambda b, h: (h, 0, 0, 0, 0)),
            pl.BlockSpec((None, 1, LANES), lambda b, h: (h, 0, 0)),
            pl.BlockSpec(perm.shape, lambda b, h: (0, 0)),
        ] + cast_specs,
        out_specs=[pl.BlockSpec((None, seq, LANES), lambda b, h: (h, b, 0))] + cast_specs,
        scratch_shapes=[
            pltpu.VMEM((seq, LANES), F32),
            pltpu.VMEM((pad_rows, LANES), BF16),
            pltpu.VMEM((pad_rows, LANES), BF16),
            pltpu.VMEM((pad_rows, LANES), BF16),
            pltpu.VMEM((pad_rows, LANES), BF16),
            pltpu.VMEM((seq, 2 * LANES), F32),
            pltpu.VMEM((seq, LANES), F32),
        ],
        compiler_params=_cparams(("arbitrary", "arbitrary")),
        name="attention",
    )(u_p, u_p, u_p, bias_tab, attn_out_norm.reshape(heads, 1, LANES), perm, *cast_weights)
    return outs[0], outs[1:]


def _filter_features(seq):
    pos = np.arange(seq, dtype=np.float32)
    t = pos / np.float32(max(seq - 1, 1))
    bands = (HY_EMB - 1) // 2
    fr = np.linspace(1e-4, bands - 1, bands, dtype=np.float32)
    ang = np.float32(2.0 * math.pi / seq) * pos[:, None] * fr[None, :]
    z = np.concatenate([t[:, None], np.cos(ang), -np.sin(ang)], axis=-1).astype(np.float32)
    zp = np.zeros((seq, LANES), np.float32)
    zp[:, :HY_EMB] = z
    offs = (np.abs(pos - (seq // 2)) / np.float32(seq / 2)).astype(np.float32)
    return zp, offs[:, None]


def _filter_kernel(z_ref, offs_ref, w1_ref, b1_ref, wi_ref, bi_ref, wo_ref, fq_ref,
                   decay_ref, o_ref):
    hp = lax.Precision.HIGHEST
    fq = fq_ref[...]
    h = jnp.sin(fq * (jnp.dot(z_ref[...], w1_ref[...], precision=hp,
                              preferred_element_type=F32) + b1_ref[...]))
    for j in range(HY_INNER):
        h = jnp.sin(fq * (jnp.dot(h, wi_ref[j], precision=hp,
                                  preferred_element_type=F32) + bi_ref[j]))
    h_hi, h_lo = _split_bf16(h)
    w_hi, w_lo = _split_bf16(wo_ref[...])
    dot = functools.partial(jnp.dot, preferred_element_type=F32)
    filt = dot(h_hi, w_hi) + (dot(h_hi, w_lo) + dot(h_lo, w_hi))
    o_ref[...] = filt * jnp.exp(-offs_ref[...] * jnp.abs(decay_ref[...]))


def _hyena_filter(seq, hy_w, w1, b1, wi, bi, wo, freq, decay):
    zp, offs = _filter_features(seq)
    w1p = jnp.zeros((LANES, HY_FILTER_WIDTH), F32).at[:HY_EMB].set(w1.astype(F32))
    tr = 512
    fw = HY_FILTER_WIDTH
    const = lambda *shape: pl.BlockSpec(shape, lambda i: (0,) * len(shape))
    return pl.pallas_call(
        _filter_kernel,
        out_shape=jax.ShapeDtypeStruct((seq, hy_w), F32),
        grid=(seq // tr,),
        in_specs=[
            pl.BlockSpec((tr, LANES), lambda i: (i, 0)),
            pl.BlockSpec((tr, 1), lambda i: (i, 0)),
            const(LANES, fw), const(1, fw), const(HY_INNER, fw, fw),
            const(HY_INNER, 1, fw), const(fw, hy_w), const(1, fw), const(1, hy_w),
        ],
        out_specs=pl.BlockSpec((tr, hy_w), lambda i: (i, 0)),
        compiler_params=_cparams(("parallel",)),
        name="hyena_filter",
    )(jnp.asarray(zp), jnp.asarray(offs), w1p, b1.reshape(1, fw).astype(F32),
      wi.astype(F32), bi.reshape(HY_INNER, 1, fw).astype(F32), wo.astype(F32),
      freq.reshape(1, fw).astype(F32), decay.reshape(1, hy_w).astype(F32))


def _stack(c):
    return np.block([[c.real, -c.imag], [c.imag, c.real]])


def _fft_len(seq):
    return 3 * seq // 2


@functools.lru_cache(maxsize=None)
def _dft_constants(seq):
    n = _fft_len(seq)
    r = FFT_R
    a_n = n // r
    a_in = seq // r
    a_out = (seq // 2) // r
    ar = np.arange(a_n)
    br = np.arange(r)
    f1 = np.exp(-2j * np.pi * np.outer(ar, ar) / a_n)
    f1_fwd = _stack(f1[:, :a_in])
    f1_fwd_real = np.concatenate([f1[:, :a_in].real, f1[:, :a_in].imag], 0)
    f1_inv = _stack(np.conj(f1).T[a_out:a_out + a_in] / n)
    ph = (br[None, None, :] * br[None, :, None] / r
          + br[None, None, :] * ar[:, None, None] / n)
    g = np.exp(-2j * np.pi * ph)
    g_fwd = np.stack([_stack(g[k]) for k in range(a_n)])
    g_inv = np.stack([_stack(np.conj(g[k]).T) for k in range(a_n)])
    return (f1_fwd.astype(np.float32), f1_fwd_real.astype(np.float32),
            f1_inv.astype(np.float32), g_fwd.astype(np.float32), g_inv.astype(np.float32))


def _split_bf16(x):
    if isinstance(x, np.ndarray):
        hi = x.astype(BF16)
        lo = (x - hi.astype(np.float32)).astype(BF16)
        return jnp.asarray(hi), jnp.asarray(lo)
    hi = x.astype(BF16)
    return hi, (x - hi.astype(F32)).astype(BF16)


def _dot3(m_hi, m_lo, x):
    x_hi, x_lo = _split_bf16(x)
    dot = functools.partial(jnp.dot, preferred_element_type=F32)
    return dot(m_hi, x_hi) + (dot(m_hi, x_lo) + dot(m_lo, x_hi))


def _spectrum_kernel(filt_ref, f1h_ref, f1l_ref, gh_ref, gl_ref, hr_ref, hi_ref,
                     zs_ref, ts_ref, *, seq):
    r = FFT_R
    a_half = seq // r
    a_n = _fft_len(seq) // r
    for a in range(a_half):
        zs_ref[pl.ds(a * FFT_PITCH, r), :] = filt_ref[pl.ds(a * r, r), :]

    def stage1(bp, carry):
        rhs = jnp.concatenate(
            [zs_ref[pl.ds(2 * bp + e, a_half, stride=FFT_PITCH), :] for e in range(2)], axis=1)
        t = _dot3(f1h_ref[...], f1l_ref[...], rhs)
        for e in range(2):
            for c in range(2):
                ts_ref[c, pl.ds(2 * bp + e, a_n, stride=FFT_PITCH), :] = (
                    t[c * a_n:(c + 1) * a_n, e * LANES:(e + 1) * LANES])
        return carry

    lax.fori_loop(0, r // 2, stage1, 0, unroll=4)

    def stage2(ka, carry):
        rows = pl.ds(pl.multiple_of(ka * FFT_PITCH, 8), r)
        rhs = jnp.concatenate([ts_ref[0, rows, :], ts_ref[1, rows, :]], axis=0)
        s = _dot3(gh_ref[ka], gl_ref[ka], rhs)
        out = pl.ds(pl.multiple_of(ka * r, r), r)
        hr_ref[out, :] = s[:r]
        hi_ref[out, :] = s[r:]
        return carry

    lax.fori_loop(0, a_n, stage2, 0, unroll=16)


def _filter_spectrum(filt, seq):
    hy_w = filt.shape[1]
    n = _fft_len(seq)
    _, f1_real, _, g_fwd, _ = _dft_constants(seq)
    f1h, f1l = _split_bf16(f1_real)
    gh, gl = _split_bf16(g_fwd)
    a_n = n // FFT_R
    out = jax.ShapeDtypeStruct((n, hy_w), F32)
    gspec = pl.BlockSpec(g_fwd.shape, lambda c: (0, 0, 0), pipeline_mode=pl.Buffered(1))
    return pl.pallas_call(
        functools.partial(_spectrum_kernel, seq=seq),
        out_shape=(out, out),
        grid=(hy_w // LANES,),
        in_specs=[
            pl.BlockSpec((seq, LANES), lambda c: (0, c)),
            pl.BlockSpec(f1_real.shape, lambda c: (0, 0)),
            pl.BlockSpec(f1_real.shape, lambda c: (0, 0)),
            gspec, gspec,
        ],
        out_specs=(pl.BlockSpec((n, LANES), lambda c: (0, c)),
                   pl.BlockSpec((n, LANES), lambda c: (0, c))),
        scratch_shapes=[
            pltpu.VMEM((seq // FFT_R * FFT_PITCH, LANES), F32),
            pltpu.VMEM((2, a_n * FFT_PITCH, LANES), F32),
        ],
        compiler_params=_cparams(("parallel",)),
        name="filter_spectrum",
    )(filt, f1h, f1l, gh, gl)


def _hy_conv_kernel(z_ref, x_ref, hr_ref, hi_ref, f1_ref, f1i_ref,
                    g_ref, gi_ref, bias_ref, gain_ref, o_ref,
                    zs_ref, ts_ref, ys_ref, *, seq):
    r = FFT_R
    a_half = seq // r
    a_n = _fft_len(seq) // r
    for c in range(2):
        for a in range(a_half):
            zs_ref[c, pl.ds(a * FFT_PITCH, r), :] = (
                z_ref[pl.ds(c * seq + a * r, r), :].astype(F32))

    def stage1(bp, carry):
        cols = []
        for e in range(2):
            rows = pl.ds(2 * bp + e, a_half, stride=FFT_PITCH)
            cols.append(jnp.concatenate([zs_ref[0, rows, :], zs_ref[1, rows, :]], axis=0))
        rhs = jnp.concatenate(cols, axis=1).astype(BF16)
        t = jnp.dot(f1_ref[...], rhs, preferred_element_type=F32)
        for e in range(2):
            for c in range(2):
                ts_ref[c, pl.ds(2 * bp + e, a_n, stride=FFT_PITCH), :] = (
                    t[c * a_n:(c + 1) * a_n, e * LANES:(e + 1) * LANES])
        return carry

    lax.fori_loop(0, r // 2, stage1, 0, unroll=8)

    def stage2(ka, carry):
        rows = pl.ds(pl.multiple_of(ka * FFT_PITCH, 8), r)
        rhs = jnp.concatenate([ts_ref[0, rows, :], ts_ref[1, rows, :]], axis=0).astype(BF16)
        s = jnp.dot(g_ref[ka], rhs, preferred_element_type=F32)
        hrows = pl.ds(pl.multiple_of(ka * r, r), r)
        hr = hr_ref[hrows, :]
        hi = hi_ref[hrows, :]
        sr, si = s[:r], s[r:]
        prod = jnp.concatenate([sr * hr - si * hi, sr * hi + si * hr], axis=0).astype(BF16)
        u = jnp.dot(gi_ref[ka], prod, preferred_element_type=F32)
        ts_ref[0, rows, :] = u[:r]
        ts_ref[1, rows, :] = u[r:]
        return carry

    lax.fori_loop(0, a_n, stage2, 0, unroll=32)

    def stage3(bp, carry):
        cols = []
        for e in range(2):
            rows = pl.ds(2 * bp + e, a_n, stride=FFT_PITCH)
            cols.append(jnp.concatenate([ts_ref[0, rows, :], ts_ref[1, rows, :]], axis=0))
        rhs = jnp.concatenate(cols, axis=1).astype(BF16)
        y = jnp.dot(f1i_ref[...], rhs, preferred_element_type=F32)
        for e in range(2):
            for c in range(2):
                ys_ref[c, pl.ds(2 * bp + e, a_half, stride=FFT_PITCH), :] = (
                    y[c * a_half:(c + 1) * a_half, e * LANES:(e + 1) * LANES])
        return carry

    lax.fori_loop(0, r // 2, stage3, 0, unroll=8)

    def finish(a, carry):
        prow = pl.ds(pl.multiple_of(a * FFT_PITCH, 8), r)
        for c in range(2):
            orow = pl.ds(pl.multiple_of(c * seq + a * r, r), r)
            z = ys_ref[c, prow, :] + zs_ref[c, prow, :] * bias_ref[...]
            y = z * x_ref[orow, :].astype(F32)
            o_ref[orow, :] = _rms(y, gain_ref[...]).astype(BF16)
        return carry

    lax.fori_loop(0, a_half, finish, 0, unroll=8)


def _hyena_conv(z_t, x0c_t, h_re, h_im, hy_bias, hy_out_norm, batch, seq, groups):
    n = _fft_len(seq)
    f1_fwd, _, f1_inv, g_fwd, g_inv = _dft_constants(seq)
    a_n = n // FFT_R
    zspec = pl.BlockSpec((None, 2 * seq, LANES), lambda g, p: (g, p, 0))
    hspec = pl.BlockSpec((n, LANES), lambda g, p: (0, g))
    cspec = lambda arr: pl.BlockSpec(arr.shape, lambda g, p: (0,) * arr.ndim,
                                     pipeline_mode=pl.Buffered(1))
    vspec = pl.BlockSpec((None, 1, LANES), lambda g, p: (g, 0, 0))
    out = jax.ShapeDtypeStruct((groups, batch * seq, LANES), BF16)
    return pl.pallas_call(
        functools.partial(_hy_conv_kernel, seq=seq),
        out_shape=out,
        grid=(groups, batch // 2),
        in_specs=[zspec, zspec, hspec, hspec,
                  cspec(f1_fwd), cspec(f1_inv), cspec(g_fwd), cspec(g_inv), vspec, vspec],
        out_specs=zspec,
        scratch_shapes=[
            pltpu.VMEM((2, seq // FFT_R * FFT_PITCH, LANES), F32),
            pltpu.VMEM((2, a_n * FFT_PITCH, LANES), F32),
            pltpu.VMEM((2, seq // FFT_R * FFT_PITCH, LANES), F32),
        ],
        compiler_params=_cparams(("parallel", "arbitrary")),
        name="hyena_conv",
    )(z_t, x0c_t, h_re, h_im,
      jnp.asarray(f1_fwd, BF16), jnp.asarray(f1_inv, BF16),
      jnp.asarray(g_fwd, BF16), jnp.asarray(g_inv, BF16),
      hy_bias.astype(F32).reshape(groups, 1, LANES),
      hy_out_norm.astype(F32).reshape(groups, 1, LANES))


def _out_proj_kernel(ya_ref, yh_ref, x_ref, w_ref, o_ref, *, heads, groups):
    lhs = jnp.concatenate([ya_ref[g] for g in range(heads)]
                          + [yh_ref[g] for g in range(groups)], axis=-1)
    o_ref[...] = x_ref[...] + jnp.dot(lhs, w_ref[...], preferred_element_type=F32)


def _out_proj(ya_t, yh_t, x2, w_out, tm):
    m, d = x2.shape
    heads, groups = ya_t.shape[0], yh_t.shape[0]
    return pl.pallas_call(
        functools.partial(_out_proj_kernel, heads=heads, groups=groups),
        out_shape=jax.ShapeDtypeStruct((m, d), F32),
        grid=(m // tm,),
        in_specs=[
            pl.BlockSpec((heads, tm, LANES), lambda i: (0, i, 0)),
            pl.BlockSpec((groups, tm, LANES), lambda i: (0, i, 0)),
            pl.BlockSpec((tm, d), lambda i: (i, 0)),
            pl.BlockSpec(w_out.shape, lambda i: (0, 0), pipeline_mode=pl.Buffered(1)),
        ],
        out_specs=pl.BlockSpec((tm, d), lambda i: (i, 0)),
        compiler_params=_cparams(("parallel",)),
        name="out_proj",
    )(ya_t, yh_t, x2, w_out.astype(BF16))


def _ffn_kernel(h_ref, g_ref, wa_ref, wg_ref, wd_ref, o_ref, xn_ref):
    def hidden_tile(xn):
        a = jnp.dot(xn, wa_ref[...], preferred_element_type=F32)
        g = jnp.dot(xn, wg_ref[...], preferred_element_type=F32)
        act = (a * jax.nn.sigmoid(a) * g).astype(BF16)
        return jnp.dot(act, wd_ref[...], preferred_element_type=F32)

    @pl.when(pl.program_id(1) == 0)
    def _():
        chunk = 256
        for c in range(h_ref.shape[0] // chunk):
            rows = pl.ds(c * chunk, chunk)
            h = h_ref[rows, :]
            xn = _rms(h, g_ref[...]).astype(BF16)
            xn_ref[rows, :] = xn
            o_ref[rows, :] = h + hidden_tile(xn)

    @pl.when(pl.program_id(1) > 0)
    def _():
        o_ref[...] += hidden_tile(xn_ref[...])


def _ffn(h1, norm2, w_gu, w_down, tm, th):
    m, d = h1.shape
    hidden = w_down.shape[0]
    nh = hidden // th
    w_gu = w_gu.astype(BF16)
    return pl.pallas_call(
        _ffn_kernel,
        out_shape=jax.ShapeDtypeStruct((m, d), F32),
        grid=(m // tm, nh),
        in_specs=[
            pl.BlockSpec((tm, d), lambda i, j: (i, 0)),
            pl.BlockSpec((1, d), lambda i, j: (0, 0)),
            pl.BlockSpec((d, th), lambda i, j: (0, j)),
            pl.BlockSpec((d, th), lambda i, j: (0, nh + j)),
            pl.BlockSpec((th, d), lambda i, j: (j, 0)),
        ],
        out_specs=pl.BlockSpec((tm, d), lambda i, j: (i, 0)),
        scratch_shapes=[pltpu.VMEM((tm, d), BF16)],
        compiler_params=_cparams(("parallel", "arbitrary")),
        name="ffn",
    )(h1, norm2.reshape(1, d), w_gu, w_gu, w_down.astype(BF16))


def _ple_kernel(h_ref, p_ref, gn_ref, wg_ref, wp_ref, pn_ref, o_ref):
    chunk = 256
    for c in range(h_ref.shape[0] // chunk):
        rows = pl.ds(c * chunk, chunk)
        h = h_ref[rows, :]
        e = _rms(jnp.dot(p_ref[rows, :].astype(BF16), wp_ref[...],
                         preferred_element_type=F32), pn_ref[...])
        inv = lax.rsqrt(jnp.mean(h * h, axis=-1, keepdims=True) + EPS)
        logits = jnp.dot((h * gn_ref[...]).astype(BF16), wg_ref[...],
                         preferred_element_type=F32)
        o_ref[rows, :] = h + jax.nn.sigmoid(logits * inv) * e


def _ple(h2, p2, ple_norm, w_gate, w_proj, ple_post_norm, tm):
    m, d = h2.shape
    pd = p2.shape[1]
    return pl.pallas_call(
        _ple_kernel,
        out_shape=jax.ShapeDtypeStruct((m, d), F32),
        grid=(m // tm,),
        in_specs=[
            pl.BlockSpec((tm, d), lambda i: (i, 0)),
            pl.BlockSpec((tm, pd), lambda i: (i, 0)),
            pl.BlockSpec((1, d), lambda i: (0, 0)),
            pl.BlockSpec((d, d), lambda i: (0, 0), pipeline_mode=pl.Buffered(1)),
            pl.BlockSpec((pd, d), lambda i: (0, 0), pipeline_mode=pl.Buffered(1)),
            pl.BlockSpec((1, d), lambda i: (0, 0)),
        ],
        out_specs=pl.BlockSpec((tm, d), lambda i: (i, 0)),
        compiler_params=_cparams(("parallel",)),
        name="ple",
    )(h2, p2, ple_norm.reshape(1, d), w_gate.astype(BF16), w_proj.astype(BF16),
      ple_post_norm.reshape(1, d))


def kernel(x, p, rel_bias, norm1, w_in, q_norm, k_norm, conv_w, conv_b, hy_w1, hy_b1, hy_wi, hy_bi, hy_wo, hy_freq, hy_decay, hy_bias, attn_out_norm, hy_out_norm, w_out, norm2, w_gu, w_down, ple_norm, w_ple_gate, w_ple_proj, ple_post_norm):
    batch, seq, d = x.shape
    attn_w = d // 2
    heads = attn_w // HEAD_DIM
    groups = heads
    hy_w = groups * LANES
    m = batch * seq
    tm = min(1024, m)
    th = 512
    h = x.reshape(m, d)
    bias_tab = _attn_bias_tables(rel_bias)
    for i in range(norm1.shape[0]):
        h3 = h.reshape(batch, seq, d)
        u_p, z_t, x0c_t = _in_proj(h3, norm1[i], w_in[i], q_norm[i], k_norm[i], conv_w[i],
                                   conv_b[i], attn_w, tm // 2)
        ya_t, (wo, wgu, wdn, wpg, wpp) = _attention(
            u_p, bias_tab, attn_out_norm[i], batch, seq, heads,
            [w_out[i], w_gu[i], w_down[i], w_ple_gate[i], w_ple_proj[i]])
        filt = _hyena_filter(seq, hy_w, hy_w1[i], hy_b1[i], hy_wi[i], hy_bi[i], hy_wo[i],
                             hy_freq[i], hy_decay[i])
        h_re, h_im = _filter_spectrum(filt, seq)
        yh_t = _hyena_conv(z_t, x0c_t, h_re, h_im, hy_bias[i], hy_out_norm[i],
                           batch, seq, groups)
        h = _out_proj(ya_t, yh_t, h, wo, tm)
        h = _ffn(h, norm2[i], wgu, wdn, tm, th)
        h = _ple(h, p[i].reshape(m, PLE_DIM), ple_norm[i], wpg, wpp, ple_post_norm[i], tm)
    return h.reshape(batch, seq, d)
```

```python
import functools
import math

import jax
import jax.numpy as jnp
import numpy as np
from jax import lax
from jax.experimental import pallas as pl
from jax.experimental.pallas import tpu as pltpu

F32 = jnp.float32
BF16 = jnp.bfloat16

LANES = 128
HEAD_DIM = 128
ATTN_PATTERNS = ((128, 1), (512, 4), (2048, 16))
PERM = 16
HALO = 16
BAND = 64
QBLK = 128
KBLK = QBLK + 2 * BAND
N_BUCKETS = 32
REL_MAX_DIST = 1024
HY_EMB = 33
HY_FILTER_WIDTH = 64
HY_INNER = 2
EPS = 1e-6
NEG = -1e30
PLE_DIM = 256

FFT_R = 64
FFT_PITCH = FFT_R + 8
VMEM_LIMIT = 56 * 1024 * 1024


def _cparams(sem, vmem=VMEM_LIMIT):
    return pltpu.CompilerParams(dimension_semantics=sem, vmem_limit_bytes=vmem)


def _rms(x, gain):
    ms = jnp.mean(x * x, axis=-1, keepdims=True)
    return x * lax.rsqrt(ms + EPS) * gain


def _in_proj_kernel(x_ref, xa_ref, xb_ref, g_ref, w_ref, qg_ref, kg_ref, perm_ref,
                    cw_ref, cb_ref, oq_ref, z_ref, x0_ref, ext_ref, x1_ref, *,
                    attn_w, tn, tiles_per_seq):
    tm = x_ref.shape[0]
    chunk = PERM * PERM
    hpt = tn // LANES
    n_qkv = 3 * attn_w // tn
    dot = functools.partial(jnp.dot, preferred_element_type=F32)

    xn, xp = [], []
    for c in range(tm // chunk):
        xn.append(_rms(x_ref[pl.ds(c * chunk, chunk), :], g_ref[...]).astype(BF16))
        xp.append(dot(perm_ref[...], xn[c]).astype(BF16))

    for t in range(n_qkv):
        w = w_ref[:, t * tn:(t + 1) * tn]
        gain = (qg_ref[...] * (HEAD_DIM ** -0.5) if t * tn < attn_w
                else kg_ref[...] if t * tn < 2 * attn_w else None)
        for c in range(tm // chunk):
            acc = dot(xp[c], w)
            for hh in range(hpt):
                a = acc[:, hh * LANES:(hh + 1) * LANES]
                if gain is not None:
                    a = _rms(a, gain)
                a = a.astype(BF16)
                for r in range(PERM):
                    oq_ref[t * hpt + hh, r, pl.ds(c * PERM, PERM), :] = (
                        a[r * PERM:(r + 1) * PERM])

    tile = pl.program_id(0) % tiles_per_seq
    before = _rms(xa_ref[...], g_ref[...]) * jnp.where(tile == 0, 0.0, 1.0)
    after = _rms(xb_ref[...], g_ref[...]) * jnp.where(tile == tiles_per_seq - 1, 0.0, 1.0)
    xe = jnp.concatenate([before.astype(BF16)] + xn + [after.astype(BF16)], axis=0)
    for kind in range(3):
        col = (n_qkv + kind) * tn
        acc = dot(xe, w_ref[:, col:col + tn])
        for hh in range(hpt):
            lanes = slice(hh * LANES, (hh + 1) * LANES)
            ext_ref[hh] = acc[:, lanes]
            taps = cw_ref[:, kind * tn + hh * LANES:kind * tn + (hh + 1) * LANES]
            y = (ext_ref[hh, pl.ds(HALO - 1, tm), :] * taps[0:1]
                 + ext_ref[hh, pl.ds(HALO, tm), :] * taps[1:2]
                 + ext_ref[hh, pl.ds(HALO + 1, tm), :] * taps[2:3]
                 + cb_ref[:, kind * tn + hh * LANES:kind * tn + (hh + 1) * LANES])
            if kind == 0:
                x0_ref[hh] = y.astype(BF16)
            elif kind == 1:
                x1_ref[:, lanes] = y
            else:
                z_ref[hh] = (y * x1_ref[:, lanes]).astype(BF16)


def _in_proj(x3, norm1, w_in, q_norm, k_norm, conv_w, conv_b, attn_w, tm):
    batch, seq, d = x3.shape
    m = batch * seq
    in_w = w_in.shape[1]
    tn = min(1024, attn_w)
    g_qkv = 3 * attn_w // LANES
    groups = tn // LANES
    tiles_per_seq = seq // tm
    assert tm % (PERM * PERM) == 0 and seq % tm == 0 and tm % HALO == 0
    assert in_w == 3 * attn_w + 3 * tn, "one column tile per Hyena operand"
    perm = jnp.asarray(_to_sequence_order_matrix().T, BF16)
    const = lambda shape: pl.BlockSpec(shape, lambda i: (0,) * len(shape))
    halo_blocks = tm // HALO
    x2 = x3.reshape(m, d)
    hy_out = jax.ShapeDtypeStruct((groups, m, LANES), BF16)
    hy_spec = pl.BlockSpec((groups, tm, LANES), lambda i: (0, i, 0))
    u_q, z_t, x0c_t = pl.pallas_call(
        functools.partial(_in_proj_kernel, attn_w=attn_w, tn=tn, tiles_per_seq=tiles_per_seq),
        out_shape=(jax.ShapeDtypeStruct((g_qkv, batch, PERM, seq // PERM, LANES), BF16),
                   hy_out, hy_out),
        grid=(m // tm,),
        in_specs=[
            pl.BlockSpec((tm, d), lambda i: (i, 0)),
            pl.BlockSpec((HALO, d), lambda i: (jnp.maximum(i * halo_blocks - 1, 0), 0)),
            pl.BlockSpec((HALO, d),
                         lambda i: (jnp.minimum((i + 1) * halo_blocks, m // HALO - 1), 0)),
            const((1, d)),
            pl.BlockSpec((d, in_w), lambda i: (0, 0), pipeline_mode=pl.Buffered(1)),
            const((1, LANES)), const((1, LANES)), const(perm.shape),
            const(conv_w.shape), const((1, conv_b.shape[0])),
        ],
        out_specs=(
            pl.BlockSpec((g_qkv, None, PERM, tm // PERM, LANES),
                         lambda i: (0, i // tiles_per_seq, 0, i % tiles_per_seq, 0)),
            hy_spec, hy_spec,
        ),
        scratch_shapes=[pltpu.VMEM((groups, tm + 2 * HALO, LANES), F32),
                        pltpu.VMEM((tm, tn), F32)],
        compiler_params=_cparams(("parallel",)),
        name="in_proj",
    )(x2, x2, x2, norm1.reshape(1, d), w_in.astype(BF16), q_norm.reshape(1, LANES),
      k_norm.reshape(1, LANES), perm, conv_w.astype(F32), conv_b.astype(F32).reshape(1, -1))
    return u_q.reshape(g_qkv, m, LANES), z_t, x0c_t


def _t5_bucket(rel):
    half = N_BUCKETS // 2
    exact = half // 2
    n = np.abs(rel)
    large = exact + (np.log(np.maximum(n, 1).astype(np.float32) / np.float32(exact))
                     / np.float32(math.log(REL_MAX_DIST / exact))
                     * np.float32(half - exact)).astype(np.int32)
    large = np.minimum(large, half - 1)
    return np.where(rel > 0, half, 0) + np.where(n < exact, n, large)


def _block_orders(dil):
    sub = PERM // dil
    e, i = np.divmod(np.arange(QBLK), QBLK // sub)
    q_off = sub * i + e
    if dil == 1:
        k_off = np.arange(KBLK) - BAND
    else:
        e, i = np.divmod(np.arange(KBLK), KBLK // sub)
        k_off = sub * i + e - BAND
    return q_off, k_off


def _to_sequence_order_matrix():
    n = PERM * PERM
    mat = np.zeros((n, n), np.float32)
    r, i = np.divmod(np.arange(n), PERM)
    mat[PERM * i + r, np.arange(n)] = 1.0
    return mat


def _attn_bias_tables(rel_bias):
    tabs = []
    for _, dil in ATTN_PATTERNS:
        q_off, k_off = _block_orders(dil)
        rel = k_off[None, :] - q_off[:, None]
        band = np.abs(rel) <= BAND
        first_ok = np.broadcast_to(k_off[None, :] >= 0, rel.shape)
        last_ok = np.broadcast_to(k_off[None, :] < QBLK, rel.shape)
        onehot = np.eye(N_BUCKETS, dtype=np.float32)[_t5_bucket(rel * dil)]
        bias = jnp.einsum("qkn,nh->hqk", jnp.asarray(onehot), rel_bias.astype(F32),
                          precision=lax.Precision.HIGHEST)
        variants = [jnp.where(jnp.asarray(band & ok), bias, NEG)
                    for ok in (first_ok, np.ones_like(band), last_ok)]
        tabs.append(jnp.stack(variants, axis=1))
    return jnp.stack(tabs, axis=1)


def _attn_kernel(*refs, seq, n_cast):
    q_ref, k_ref, v_ref, bias_ref, gain_ref, perm_ref = refs[:6]
    cast_in = refs[6:6 + n_cast]
    o_ref = refs[6 + n_cast]
    cast_out = refs[7 + n_cast:7 + 2 * n_cast]
    qf_ref, kp_ref, vp_ref, kn_ref, vn_ref, acc_ref, m_ref = refs[7 + 2 * n_cast:]

    for src, dst in zip(cast_in, cast_out):
        dst[...] = src[...].astype(BF16)

    run = seq // PERM
    chunk = PERM * PERM
    ones = jnp.ones((KBLK, LANES), BF16)
    zpad = jnp.zeros((BAND, LANES), BF16)

    for ref in (kp_ref, vp_ref, kn_ref, vn_ref):
        ref[pl.ds(0, BAND), :] = zpad
        ref[pl.ds(BAND + seq, BAND), :] = zpad
    kp_ref[pl.ds(BAND, seq), :] = k_ref[...]
    vp_ref[pl.ds(BAND, seq), :] = v_ref[...]
    qf_ref[...] = q_ref[...].astype(F32)

    def runs(ref, base, n):
        return jnp.concatenate(
            [ref[pl.ds(pl.multiple_of(r * run + base, n), n), :] for r in range(PERM)], axis=0)

    def to_sequence_order(c, carry):
        base = c * PERM
        kv = jnp.concatenate([runs(k_ref, base, PERM), runs(v_ref, base, PERM)], axis=1)
        nat = jnp.dot(perm_ref[...], kv, preferred_element_type=F32).astype(BF16)
        rows = pl.ds(pl.multiple_of(BAND + c * chunk, BAND), chunk)
        kn_ref[rows, :] = nat[:, :LANES]
        vn_ref[rows, :] = nat[:, LANES:]
        return carry

    lax.fori_loop(0, seq // chunk, to_sequence_order, 0, unroll=8)

    def softmax_block(q, k, v, bias):
        s = lax.dot_general(q, k, (((1,), (1,)), ((), ())), preferred_element_type=F32) + bias
        m_blk = jnp.max(s, axis=-1, keepdims=True)
        p = jnp.exp(s - m_blk).astype(BF16)
        a_l = jnp.dot(p, jnp.concatenate([v, ones], axis=1), preferred_element_type=F32)
        return a_l, jnp.broadcast_to(m_blk, (QBLK, LANES))

    def variant(qb, nb):
        return jnp.where(qb == 0, 0, jnp.where(qb == nb - 1, 2, 1))

    def merge(row_slices, a_l, m_b):
        m_old = jnp.concatenate([m_ref[rows, :] for rows in row_slices], axis=0)
        a_old = jnp.concatenate([acc_ref[rows, :] for rows in row_slices], axis=0)
        m_new = jnp.maximum(m_old, m_b)
        w_old = jnp.exp(m_old - m_new)
        w_blk = jnp.exp(m_b - m_new)
        a_new = (a_old * jnp.concatenate([w_old, w_old], axis=1)
                 + a_l * jnp.concatenate([w_blk, w_blk], axis=1))
        n = QBLK // len(row_slices)
        for e, rows in enumerate(row_slices):
            acc_ref[rows, :] = a_new[e * n:(e + 1) * n]
            m_ref[rows, :] = m_new[e * n:(e + 1) * n]

    nb16 = run // QBLK

    def block16(t, carry):
        r, qb = t // nb16, t % nb16
        off = pl.multiple_of(r * run + qb * QBLK, QBLK)
        a_l, m_b = softmax_block(q_ref[pl.ds(off, QBLK), :], kp_ref[pl.ds(off, KBLK), :],
                                 vp_ref[pl.ds(off, KBLK), :], bias_ref[2, variant(qb, nb16)])
        acc_ref[pl.ds(off, QBLK), :] = a_l
        m_ref[pl.ds(off, QBLK), :] = m_b
        return carry

    lax.fori_loop(0, PERM * nb16, block16, 0, unroll=32)

    dil = ATTN_PATTERNS[1][1]
    sub = PERM // dil
    nb4 = (seq // dil) // QBLK
    qn, kn = QBLK // sub, KBLK // sub

    def block4(t, carry):
        r, qb = t // nb4, t % nb4
        q_rows = [pl.ds(pl.multiple_of((dil * e + r) * run + qb * qn, qn), qn)
                  for e in range(sub)]
        k_rows = [pl.ds(pl.multiple_of((dil * e + r) * run + qb * qn + BAND - BAND // sub,
                                       BAND // sub), kn) for e in range(sub)]
        q = jnp.concatenate([q_ref[rows, :] for rows in q_rows], axis=0)
        k = jnp.concatenate([kp_ref[rows, :] for rows in k_rows], axis=0)
        v = jnp.concatenate([vp_ref[rows, :] for rows in k_rows], axis=0)
        a_l, m_b = softmax_block(q, k, v, bias_ref[1, variant(qb, nb4)])
        merge(q_rows, a_l, m_b)
        return carry

    lax.fori_loop(0, dil * nb4, block4, 0, unroll=16)

    nb1 = seq // QBLK
    qn1 = QBLK // PERM

    def block1(qb, carry):
        q_rows = [pl.ds(pl.multiple_of(r * run + qb * qn1, qn1), qn1) for r in range(PERM)]
        q = jnp.concatenate([qf_ref[rows, :] for rows in q_rows], axis=0).astype(BF16)
        k_rows = pl.ds(pl.multiple_of(qb * QBLK, QBLK), KBLK)
        a_l, m_b = softmax_block(q, kn_ref[k_rows, :], vn_ref[k_rows, :],
                                 bias_ref[0, variant(qb, nb1)])
        merge(q_rows, a_l, m_b)
        return carry

    lax.fori_loop(0, nb1, block1, 0, unroll=16)

    def finish(c, carry):
        st = runs(acc_ref, c * PERM, PERM)
        y = _rms(st[:, :LANES] / st[:, LANES:], gain_ref[...]).astype(BF16)
        nat = jnp.dot(perm_ref[...], y, preferred_element_type=F32)
        o_ref[pl.ds(pl.multiple_of(c * chunk, chunk), chunk), :] = nat.astype(BF16)
        return carry

    lax.fori_loop(0, seq // chunk, finish, 0, unroll=8)


def _cast_slices(weights, n_steps, step_of):
    specs = []
    for w in weights:
        rows, hold = w.shape[0], 1
        while (rows * hold) % n_steps or (rows * hold // n_steps) % 16:
            hold *= 2
        specs.append(pl.BlockSpec((rows * hold // n_steps, w.shape[1]),
                                  lambda *ids, hold=hold: (step_of(*ids) // hold, 0)))
    return specs


def _attention(u_p, bias_tab, attn_out_norm, batch, seq, heads, cast_weights):
    assert ATTN_PATTERNS[0][1] == 1 and ATTN_PATTERNS[2][1] == PERM
    assert seq % (PERM * QBLK) == 0
    perm = jnp.asarray(_to_sequence_order_matrix(), BF16)
    pad_rows = seq + 2 * BAND
    cast_specs = _cast_slices(cast_weights, batch * heads, lambda b, h: b * heads + h)
    outs = pl.pallas_call(
        functools.partial(_attn_kernel, seq=seq, n_cast=len(cast_weights)),
        out_shape=[jax.ShapeDtypeStruct((heads, batch * seq, LANES), BF16)]
        + [jax.ShapeDtypeStruct(w.shape, BF16) for w in cast_weights],
        grid=(batch, heads),
        in_specs=[
            pl.BlockSpec((None, seq, LANES), lambda b, h: (h, b, 0)),
            pl.BlockSpec((None, seq, LANES), lambda b, h: (heads + h, b, 0)),
            pl.BlockSpec((None, seq, LANES), lambda b, h: (2 * heads + h, b, 0)),
            pl.BlockSpec((None, len(ATTN_PATTERNS), 3, QBLK, KBLK),
                         lambda b, h: (h, 0, 0, 0, 0)),
            pl.BlockSpec((None, 1, LANES), lambda b, h: (h, 0, 0)),
            pl.BlockSpec(perm.shape, lambda b, h: (0, 0)),
        ] + cast_specs,
        out_specs=[pl.BlockSpec((None, seq, LANES), lambda b, h: (h, b, 0))] + cast_specs,
        scratch_shapes=[
            pltpu.VMEM((seq, LANES), F32),
            pltpu.VMEM((pad_rows, LANES), BF16),
            pltpu.VMEM((pad_rows, LANES), BF16),
            pltpu.VMEM((pad_rows, LANES), BF16),
            pltpu.VMEM((pad_rows, LANES), BF16),
            pltpu.VMEM((seq, 2 * LANES), F32),
            pltpu.VMEM((seq, LANES), F32),
        ],
        compiler_params=_cparams(("arbitrary", "arbitrary")),
        name="attention",
    )(u_p, u_p, u_p, bias_tab, attn_out_norm.reshape(heads, 1, LANES), perm, *cast_weights)
    return outs[0], outs[1:]


def _filter_features(seq):
    pos = np.arange(seq, dtype=np.float32)
    t = pos / np.float32(max(seq - 1, 1))
    bands = (HY_EMB - 1) // 2
    fr = np.linspace(1e-4, bands - 1, bands, dtype=np.float32)
    ang = np.float32(2.0 * math.pi / seq) * pos[:, None] * fr[None, :]
    z = np.concatenate([t[:, None], np.cos(ang), -np.sin(ang)], axis=-1).astype(np.float32)
    zp = np.zeros((seq, LANES), np.float32)
    zp[:, :HY_EMB] = z
    offs = (np.abs(pos - (seq // 2)) / np.float32(seq / 2)).astype(np.float32)
    return zp, offs[:, None]


def _filter_kernel(z_ref, offs_ref, w1_ref, b1_ref, wi_ref, bi_ref, wo_ref, fq_ref,
                   decay_ref, o_ref):
    hp = lax.Precision.HIGHEST
    fq = fq_ref[...]
    h = jnp.sin(fq * (jnp.dot(z_ref[...], w1_ref[...], precision=hp,
                              preferred_element_type=F32) + b1_ref[...]))
    for j in range(HY_INNER):
        h = jnp.sin(fq * (jnp.dot(h, wi_ref[j], precision=hp,
                                  preferred_element_type=F32) + bi_ref[j]))
    h_hi, h_lo = _split_bf16(h)
    w_hi, w_lo = _split_bf16(wo_ref[...])
    dot = functools.partial(jnp.dot, preferred_element_type=F32)
    filt = dot(h_hi, w_hi) + (dot(h_hi, w_lo) + dot(h_lo, w_hi))
    o_ref[...] = filt * jnp.exp(-offs_ref[...] * jnp.abs(decay_ref[...]))


def _hyena_filter(seq, hy_w, w1, b1, wi, bi, wo, freq, decay):
    zp, offs = _filter_features(seq)
    w1p = jnp.zeros((LANES, HY_FILTER_WIDTH), F32).at[:HY_EMB].set(w1.astype(F32))
    tr = 512
    fw = HY_FILTER_WIDTH
    const = lambda *shape: pl.BlockSpec(shape, lambda i: (0,) * len(shape))
    return pl.pallas_call(
        _filter_kernel,
        out_shape=jax.ShapeDtypeStruct((seq, hy_w), F32),
        grid=(seq // tr,),
        in_specs=[
            pl.BlockSpec((tr, LANES), lambda i: (i, 0)),
            pl.BlockSpec((tr, 1), lambda i: (i, 0)),
            const(LANES, fw), const(1, fw), const(HY_INNER, fw, fw),
            const(HY_INNER, 1, fw), const(fw, hy_w), const(1, fw), const(1, hy_w),
        ],
        out_specs=pl.BlockSpec((tr, hy_w), lambda i: (i, 0)),
        compiler_params=_cparams(("parallel",)),
        name="hyena_filter",
    )(jnp.asarray(zp), jnp.asarray(offs), w1p, b1.reshape(1, fw).astype(F32),
      wi.astype(F32), bi.reshape(HY_INNER, 1, fw).astype(F32), wo.astype(F32),
      freq.reshape(1, fw).astype(F32), decay.reshape(1, hy_w).astype(F32))


def _stack(c):
    return np.block([[c.real, -c.imag], [c.imag, c.real]])


def _fft_len(seq):
    return 3 * seq // 2


@functools.lru_cache(maxsize=None)
def _dft_constants(seq):
    n = _fft_len(seq)
    r = FFT_R
    a_n = n // r
    a_in = seq // r
    a_out = (seq // 2) // r
    ar = np.arange(a_n)
    br = np.arange(r)
    f1 = np.exp(-2j * np.pi * np.outer(ar, ar) / a_n)
    f1_fwd = _stack(f1[:, :a_in])
    f1_fwd_real = np.concatenate([f1[:, :a_in].real, f1[:, :a_in].imag], 0)
    f1_inv = _stack(np.conj(f1).T[a_out:a_out + a_in] / n)
    ph = (br[None, None, :] * br[None, :, None] / r
          + br[None, None, :] * ar[:, None, None] / n)
    g = np.exp(-2j * np.pi * ph)
    g_fwd = np.stack([_stack(g[k]) for k in range(a_n)])
    g_inv = np.stack([_stack(np.conj(g[k]).T) for k in range(a_n)])
    return (f1_fwd.astype(np.float32), f1_fwd_real.astype(np.float32),
            f1_inv.astype(np.float32), g_fwd.astype(np.float32), g_inv.astype(np.float32))


def _split_bf16(x):
    if isinstance(x, np.ndarray):
        hi = x.astype(BF16)
        lo = (x - hi.astype(np.float32)).astype(BF16)
        return jnp.asarray(hi), jnp.asarray(lo)
    hi = x.astype(BF16)
    return hi, (x - hi.astype(F32)).astype(BF16)


def _dot3(m_hi, m_lo, x):
    x_hi, x_lo = _split_bf16(x)
    dot = functools.partial(jnp.dot, preferred_element_type=F32)
    return dot(m_hi, x_hi) + (dot(m_hi, x_lo) + dot(m_lo, x_hi))


def _spectrum_kernel(filt_ref, f1h_ref, f1l_ref, gh_ref, gl_ref, hr_ref, hi_ref,
                     zs_ref, ts_ref, *, seq):
    r = FFT_R
    a_half = seq // r
    a_n = _fft_len(seq) // r
    for a in range(a_half):
        zs_ref[pl.ds(a * FFT_PITCH, r), :] = filt_ref[pl.ds(a * r, r), :]

    def stage1(bp, carry):
        rhs = jnp.concatenate(
            [zs_ref[pl.ds(2 * bp + e, a_half, stride=FFT_PITCH), :] for e in range(2)], axis=1)
        t = _dot3(f1h_ref[...], f1l_ref[...], rhs)
        for e in range(2):
            for c in range(2):
                ts_ref[c, pl.ds(2 * bp + e, a_n, stride=FFT_PITCH), :] = (
                    t[c * a_n:(c + 1) * a_n, e * LANES:(e + 1) * LANES])
        return carry

    lax.fori_loop(0, r // 2, stage1, 0, unroll=4)

    def stage2(ka, carry):
        rows = pl.ds(pl.multiple_of(ka * FFT_PITCH, 8), r)
        rhs = jnp.concatenate([ts_ref[0, rows, :], ts_ref[1, rows, :]], axis=0)
        s = _dot3(gh_ref[ka], gl_ref[ka], rhs)
        out = pl.ds(pl.multiple_of(ka * r, r), r)
        hr_ref[out, :] = s[:r]
        hi_ref[out, :] = s[r:]
        return carry

    lax.fori_loop(0, a_n, stage2, 0, unroll=16)


def _filter_spectrum(filt, seq):
    hy_w = filt.shape[1]
    n = _fft_len(seq)
    _, f1_real, _, g_fwd, _ = _dft_constants(seq)
    f1h, f1l = _split_bf16(f1_real)
    gh, gl = _split_bf16(g_fwd)
    a_n = n // FFT_R
    out = jax.ShapeDtypeStruct((n, hy_w), F32)
    gspec = pl.BlockSpec(g_fwd.shape, lambda c: (0, 0, 0), pipeline_mode=pl.Buffered(1))
    return pl.pallas_call(
        functools.partial(_spectrum_kernel, seq=seq),
        out_shape=(out, out),
        grid=(hy_w // LANES,),
        in_specs=[
            pl.BlockSpec((seq, LANES), lambda c: (0, c)),
            pl.BlockSpec(f1_real.shape, lambda c: (0, 0)),
            pl.BlockSpec(f1_real.shape, lambda c: (0, 0)),
            gspec, gspec,
        ],
        out_specs=(pl.BlockSpec((n, LANES), lambda c: (0, c)),
                   pl.BlockSpec((n, LANES), lambda c: (0, c))),
        scratch_shapes=[
            pltpu.VMEM((seq // FFT_R * FFT_PITCH, LANES), F32),
            pltpu.VMEM((2, a_n * FFT_PITCH, LANES), F32),
        ],
        compiler_params=_cparams(("parallel",)),
        name="filter_spectrum",
    )(filt, f1h, f1l, gh, gl)


def _hy_conv_kernel(z_ref, x_ref, hr_ref, hi_ref, f1_ref, f1i_ref,
                    g_ref, gi_ref, bias_ref, gain_ref, o_ref,
                    zs_ref, ts_ref, ys_ref, *, seq):
    r = FFT_R
    a_half = seq // r
    a_n = _fft_len(seq) // r
    for c in range(2):
        for a in range(a_half):
            zs_ref[c, pl.ds(a * FFT_PITCH, r), :] = (
                z_ref[pl.ds(c * seq + a * r, r), :].astype(F32))

    def stage1(bp, carry):
        cols = []
        for e in range(2):
            rows = pl.ds(2 * bp + e, a_half, stride=FFT_PITCH)
            cols.append(jnp.concatenate([zs_ref[0, rows, :], zs_ref[1, rows, :]], axis=0))
        rhs = jnp.concatenate(cols, axis=1).astype(BF16)
        t = jnp.dot(f1_ref[...], rhs, preferred_element_type=F32)
        for e in range(2):
            for c in range(2):
                ts_ref[c, pl.ds(2 * bp + e, a_n, stride=FFT_PITCH), :] = (
                    t[c * a_n:(c + 1) * a_n, e * LANES:(e + 1) * LANES])
        return carry

    lax.fori_loop(0, r // 2, stage1, 0, unroll=16)

    def stage2(ka, carry):
        rows = pl.ds(pl.multiple_of(ka * FFT_PITCH, 8), r)
        rhs = jnp.concatenate([ts_ref[0, rows, :], ts_ref[1, rows, :]], axis=0).astype(BF16)
        s = jnp.dot(g_ref[ka], rhs, preferred_element_type=F32)
        hrows = pl.ds(pl.multiple_of(ka * r, r), r)
        hr = hr_ref[hrows, :]
        hi = hi_ref[hrows, :]
        sr, si = s[:r], s[r:]
        prod = jnp.concatenate([sr * hr - si * hi, sr * hi + si * hr], axis=0).astype(BF16)
        u = jnp.dot(gi_ref[ka], prod, preferred_element_type=F32)
        ts_ref[0, rows, :] = u[:r]
        ts_ref[1, rows, :] = u[r:]
        return carry

    lax.fori_loop(0, a_n, stage2, 0, unroll=32)

    def stage3(bp, carry):
        cols = []
        for e in range(2):
            rows = pl.ds(2 * bp + e, a_n, stride=FFT_PITCH)
            cols.append(jnp.concatenate([ts_ref[0, rows, :], ts_ref[1, rows, :]], axis=0))
        rhs = jnp.concatenate(cols, axis=1).astype(BF16)
        y = jnp.dot(f1i_ref[...], rhs, preferred_element_type=F32)
        for e in range(2):
            for c in range(2):
                ys_ref[c, pl.ds(2 * bp + e, a_half, stride=FFT_PITCH), :] = (
                    y[c * a_half:(c + 1) * a_half, e * LANES:(e + 1) * LANES])
        return carry

    lax.fori_loop(0, r // 2, stage3, 0, unroll=16)

    def finish(a, carry):
        prow = pl.ds(pl.multiple_of(a * FFT_PITCH, 8), r)
        for c in range(2):
            orow = pl.ds(pl.multiple_of(c * seq + a * r, r), r)
            z = ys_ref[c, prow, :] + zs_ref[c, prow, :] * bias_ref[...]
            y = z * x_ref[orow, :].astype(F32)
            o_ref[orow, :] = _rms(y, gain_ref[...]).astype(BF16)
        return carry

    lax.fori_loop(0, a_half, finish, 0, unroll=16)


def _hyena_conv(z_t, x0c_t, h_re, h_im, hy_bias, hy_out_norm, batch, seq, groups):
    n = _fft_len(seq)
    f1_fwd, _, f1_inv, g_fwd, g_inv = _dft_constants(seq)
    a_n = n // FFT_R
    zspec = pl.BlockSpec((None, 2 * seq, LANES), lambda g, p: (g, p, 0))
    hspec = pl.BlockSpec((n, LANES), lambda g, p: (0, g))
    cspec = lambda arr: pl.BlockSpec(arr.shape, lambda g, p: (0,) * arr.ndim,
                                     pipeline_mode=pl.Buffered(1))
    vspec = pl.BlockSpec((None, 1, LANES), lambda g, p: (g, 0, 0))
    out = jax.ShapeDtypeStruct((groups, batch * seq, LANES), BF16)
    return pl.pallas_call(
        functools.partial(_hy_conv_kernel, seq=seq),
        out_shape=out,
        grid=(groups, batch // 2),
        in_specs=[zspec, zspec, hspec, hspec,
                  cspec(f1_fwd), cspec(f1_inv), cspec(g_fwd), cspec(g_inv), vspec, vspec],
        out_specs=zspec,
        scratch_shapes=[
            pltpu.VMEM((2, seq // FFT_R * FFT_PITCH, LANES), F32),
            pltpu.VMEM((2, a_n * FFT_PITCH, LANES), F32),
            pltpu.VMEM((2, seq // FFT_R * FFT_PITCH, LANES), F32),
        ],
        compiler_params=_cparams(("parallel", "arbitrary")),
        name="hyena_conv",
    )(z_t, x0c_t, h_re, h_im,
      jnp.asarray(f1_fwd, BF16), jnp.asarray(f1_inv, BF16),
      jnp.asarray(g_fwd, BF16), jnp.asarray(g_inv, BF16),
      hy_bias.astype(F32).reshape(groups, 1, LANES),
      hy_out_norm.astype(F32).reshape(groups, 1, LANES))


def _out_proj_kernel(ya_ref, yh_ref, x_ref, w_ref, o_ref, *, heads, groups):
    lhs = jnp.concatenate([ya_ref[g] for g in range(heads)]
                          + [yh_ref[g] for g in range(groups)], axis=-1)
    o_ref[...] = x_ref[...] + jnp.dot(lhs, w_ref[...], preferred_element_type=F32)


def _out_proj(ya_t, yh_t, x2, w_out, tm):
    m, d = x2.shape
    heads, groups = ya_t.shape[0], yh_t.shape[0]
    return pl.pallas_call(
        functools.partial(_out_proj_kernel, heads=heads, groups=groups),
        out_shape=jax.ShapeDtypeStruct((m, d), F32),
        grid=(m // tm,),
        in_specs=[
            pl.BlockSpec((heads, tm, LANES), lambda i: (0, i, 0)),
            pl.BlockSpec((groups, tm, LANES), lambda i: (0, i, 0)),
            pl.BlockSpec((tm, d), lambda i: (i, 0)),
            pl.BlockSpec(w_out.shape, lambda i: (0, 0), pipeline_mode=pl.Buffered(1)),
        ],
        out_specs=pl.BlockSpec((tm, d), lambda i: (i, 0)),
        compiler_params=_cparams(("parallel",)),
        name="out_proj",
    )(ya_t, yh_t, x2, w_out.astype(BF16))


def _ffn_kernel(h_ref, g_ref, wa_ref, wg_ref, wd_ref, o_ref, xn_ref):
    def hidden_tile(xn):
        a = jnp.dot(xn, wa_ref[...], preferred_element_type=F32)
        g = jnp.dot(xn, wg_ref[...], preferred_element_type=F32)
        act = (a * jax.nn.sigmoid(a) * g).astype(BF16)
        return jnp.dot(act, wd_ref[...], preferred_element_type=F32)

    @pl.when(pl.program_id(1) == 0)
    def _():
        chunk = 256
        for c in range(h_ref.shape[0] // chunk):
            rows = pl.ds(c * chunk, chunk)
            h = h_ref[rows, :]
            xn = _rms(h, g_ref[...]).astype(BF16)
            xn_ref[rows, :] = xn
            o_ref[rows, :] = h + hidden_tile(xn)

    @pl.when(pl.program_id(1) > 0)
    def _():
        o_ref[...] += hidden_tile(xn_ref[...])


def _ffn(h1, norm2, w_gu, w_down, tm, th):
    m, d = h1.shape
    hidden = w_down.shape[0]
    nh = hidden // th
    w_gu = w_gu.astype(BF16)
    return pl.pallas_call(
        _ffn_kernel,
        out_shape=jax.ShapeDtypeStruct((m, d), F32),
        grid=(m // tm, nh),
        in_specs=[
            pl.BlockSpec((tm, d), lambda i, j: (i, 0)),
            pl.BlockSpec((1, d), lambda i, j: (0, 0)),
            pl.BlockSpec((d, th), lambda i, j: (0, j)),
            pl.BlockSpec((d, th), lambda i, j: (0, nh + j)),
            pl.BlockSpec((th, d), lambda i, j: (j, 0)),
        ],
        out_specs=pl.BlockSpec((tm, d), lambda i, j: (i, 0)),
        scratch_shapes=[pltpu.VMEM((tm, d), BF16)],
        compiler_params=_cparams(("parallel", "arbitrary")),
        name="ffn",
    )(h1, norm2.reshape(1, d), w_gu, w_gu, w_down.astype(BF16))


def _ple_kernel(h_ref, p_ref, gn_ref, wg_ref, wp_ref, pn_ref, o_ref):
    chunk = 256
    for c in range(h_ref.shape[0] // chunk):
        rows = pl.ds(c * chunk, chunk)
        h = h_ref[rows, :]
        e = _rms(jnp.dot(p_ref[rows, :].astype(BF16), wp_ref[...],
                         preferred_element_type=F32), pn_ref[...])
        inv = lax.rsqrt(jnp.mean(h * h, axis=-1, keepdims=True) + EPS)
        logits = jnp.dot((h * gn_ref[...]).astype(BF16), wg_ref[...],
                         preferred_element_type=F32)
        o_ref[rows, :] = h + jax.nn.sigmoid(logits * inv) * e


def _ple(h2, p2, ple_norm, w_gate, w_proj, ple_post_norm, tm):
    m, d = h2.shape
    pd = p2.shape[1]
    return pl.pallas_call(
        _ple_kernel,
        out_shape=jax.ShapeDtypeStruct((m, d), F32),
        grid=(m // tm,),
        in_specs=[
            pl.BlockSpec((tm, d), lambda i: (i, 0)),
            pl.BlockSpec((tm, pd), lambda i: (i, 0)),
            pl.BlockSpec((1, d), lambda i: (0, 0)),
            pl.BlockSpec((d, d), lambda i: (0, 0), pipeline_mode=pl.Buffered(1)),
            pl.BlockSpec((pd, d), lambda i: (0, 0), pipeline_mode=pl.Buffered(1)),
            pl.BlockSpec((1, d), lambda i: (0, 0)),
        ],
        out_specs=pl.BlockSpec((tm, d), lambda i: (i, 0)),
        compiler_params=_cparams(("parallel",)),
        name="ple",
    )(h2, p2, ple_norm.reshape(1, d), w_gate.astype(BF16), w_proj.astype(BF16),
      ple_post_norm.reshape(1, d))


def kernel(x, p, rel_bias, norm1, w_in, q_norm, k_norm, conv_w, conv_b, hy_w1, hy_b1, hy_wi, hy_bi, hy_wo, hy_freq, hy_decay, hy_bias, attn_out_norm, hy_out_norm, w_out, norm2, w_gu, w_down, ple_norm, w_ple_gate, w_ple_proj, ple_post_norm):
    batch, seq, d = x.shape
    attn_w = d // 2
    heads = attn_w // HEAD_DIM
    groups = heads
    hy_w = groups * LANES
    m = batch * seq
    tm = min(1024, m)
    th = 512
    h = x.reshape(m, d)
    bias_tab = _attn_bias_tables(rel_bias)
    for i in range(norm1.shape[0]):
        h3 = h.reshape(batch, seq, d)
        u_p, z_t, x0c_t = _in_proj(h3, norm1[i], w_in[i], q_norm[i], k_norm[i], conv_w[i],
                                   conv_b[i], attn_w, tm // 2)
        ya_t, (wo, wgu, wdn, wpg, wpp) = _attention(
            u_p, bias_tab, attn_out_norm[i], batch, seq, heads,
            [w_out[i], w_gu[i], w_down[i], w_ple_gate[i], w_ple_proj[i]])
        filt = _hyena_filter(seq, hy_w, hy_w1[i], hy_b1[i], hy_wi[i], hy_bi[i], hy_wo[i],
                             hy_freq[i], hy_decay[i])
        h_re, h_im = _filter_spectrum(filt, seq)
        yh_t = _hyena_conv(z_t, x0c_t, h_re, h_im, hy_bias[i], hy_out_norm[i],
                           batch, seq, groups)
        h = _out_proj(ya_t, yh_t, h, wo, tm)
        h = _ffn(h, norm2[i], wgu, wdn, tm, th)
        h = _ple(h, p[i].reshape(m, PLE_DIM), ple_norm[i], wpg, wpp, ple_post_norm[i], tm)
    return h.reshape(batch, seq, d)
```

```python
import functools
import math

import jax
import jax.numpy as jnp
import numpy as np
from jax import lax
from jax.experimental import pallas as pl
from jax.experimental.pallas import tpu as pltpu

F32 = jnp.float32
BF16 = jnp.bfloat16

LANES = 128
HEAD_DIM = 128
ATTN_PATTERNS = ((128, 1), (512, 4), (2048, 16))
PERM = 16
HALO = 16
BAND = 64
QBLK = 128
KBLK = QBLK + 2 * BAND
N_BUCKETS = 32
REL_MAX_DIST = 1024
HY_EMB = 33
HY_FILTER_WIDTH = 64
HY_INNER = 2
EPS = 1e-6
NEG = -1e30
PLE_DIM = 256

FFT_R = 64
FFT_PITCH = FFT_R + 8
VMEM_LIMIT = 56 * 1024 * 1024


def _cparams(sem, vmem=VMEM_LIMIT):
    return pltpu.CompilerParams(dimension_semantics=sem, vmem_limit_bytes=vmem)


def _rms(x, gain):
    ms = jnp.mean(x * x, axis=-1, keepdims=True)
    return x * lax.rsqrt(ms + EPS) * gain


def _in_proj_kernel(x_ref, xa_ref, xb_ref, g_ref, w_ref, qg_ref, kg_ref, perm_ref,
                    cw_ref, cb_ref, oq_ref, z_ref, x0_ref, ext_ref, x1_ref, *,
                    attn_w, tn, tiles_per_seq):
    tm = x_ref.shape[0]
    chunk = PERM * PERM
    hpt = tn // LANES
    n_qkv = 3 * attn_w // tn
    dot = functools.partial(jnp.dot, preferred_element_type=F32)

    xn, xp = [], []
    for c in range(tm // chunk):
        xn.append(_rms(x_ref[pl.ds(c * chunk, chunk), :], g_ref[...]).astype(BF16))
        xp.append(dot(perm_ref[...], xn[c]).astype(BF16))

    for t in range(n_qkv):
        w = w_ref[:, t * tn:(t + 1) * tn]
        gain = (qg_ref[...] * (HEAD_DIM ** -0.5) if t * tn < attn_w
                else kg_ref[...] if t * tn < 2 * attn_w else None)
        for c in range(tm // chunk):
            acc = dot(xp[c], w)
            for hh in range(hpt):
                a = acc[:, hh * LANES:(hh + 1) * LANES]
                if gain is not None:
                    a = _rms(a, gain)
                a = a.astype(BF16)
                for r in range(PERM):
                    oq_ref[t * hpt + hh, r, pl.ds(c * PERM, PERM), :] = (
                        a[r * PERM:(r + 1) * PERM])

    tile = pl.program_id(0) % tiles_per_seq
    before = _rms(xa_ref[...], g_ref[...]) * jnp.where(tile == 0, 0.0, 1.0)
    after = _rms(xb_ref[...], g_ref[...]) * jnp.where(tile == tiles_per_seq - 1, 0.0, 1.0)
    xe = jnp.concatenate([before.astype(BF16)] + xn + [after.astype(BF16)], axis=0)
    for kind in range(3):
        col = (n_qkv + kind) * tn
        acc = dot(xe, w_ref[:, col:col + tn])
        for hh in range(hpt):
            lanes = slice(hh * LANES, (hh + 1) * LANES)
            ext_ref[hh] = acc[:, lanes]
            taps = cw_ref[:, kind * tn + hh * LANES:kind * tn + (hh + 1) * LANES]
            y = (ext_ref[hh, pl.ds(HALO - 1, tm), :] * taps[0:1]
                 + ext_ref[hh, pl.ds(HALO, tm), :] * taps[1:2]
                 + ext_ref[hh, pl.ds(HALO + 1, tm), :] * taps[2:3]
                 + cb_ref[:, kind * tn + hh * LANES:kind * tn + (hh + 1) * LANES])
            if kind == 0:
                x0_ref[hh] = y.astype(BF16)
            elif kind == 1:
                x1_ref[:, lanes] = y
            else:
                z_ref[hh] = (y * x1_ref[:, lanes]).astype(BF16)


def _in_proj(x3, norm1, w_in, q_norm, k_norm, conv_w, conv_b, attn_w, tm):
    batch, seq, d = x3.shape
    m = batch * seq
    in_w = w_in.shape[1]
    tn = min(1024, attn_w)
    g_qkv = 3 * attn_w // LANES
    groups = tn // LANES
    tiles_per_seq = seq // tm
    assert tm % (PERM * PERM) == 0 and seq % tm == 0 and tm % HALO == 0
    assert in_w == 3 * attn_w + 3 * tn, "one column tile per Hyena operand"
    perm = jnp.asarray(_to_sequence_order_matrix().T, BF16)
    const = lambda shape: pl.BlockSpec(shape, lambda i: (0,) * len(shape))
    halo_blocks = tm // HALO
    x2 = x3.reshape(m, d)
    hy_out = jax.ShapeDtypeStruct((groups, m, LANES), BF16)
    hy_spec = pl.BlockSpec((groups, tm, LANES), lambda i: (0, i, 0))
    u_q, z_t, x0c_t = pl.pallas_call(
        functools.partial(_in_proj_kernel, attn_w=attn_w, tn=tn, tiles_per_seq=tiles_per_seq),
        out_shape=(jax.ShapeDtypeStruct((g_qkv, batch, PERM, seq // PERM, LANES), BF16),
                   hy_out, hy_out),
        grid=(m // tm,),
        in_specs=[
            pl.BlockSpec((tm, d), lambda i: (i, 0)),
            pl.BlockSpec((HALO, d), lambda i: (jnp.maximum(i * halo_blocks - 1, 0), 0)),
            pl.BlockSpec((HALO, d),
                         lambda i: (jnp.minimum((i + 1) * halo_blocks, m // HALO - 1), 0)),
            const((1, d)),
            pl.BlockSpec((d, in_w), lambda i: (0, 0), pipeline_mode=pl.Buffered(1)),
            const((1, LANES)), const((1, LANES)), const(perm.shape),
            const(conv_w.shape), const((1, conv_b.shape[0])),
        ],
        out_specs=(
            pl.BlockSpec((g_qkv, None, PERM, tm // PERM, LANES),
                         lambda i: (0, i // tiles_per_seq, 0, i % tiles_per_seq, 0)),
            hy_spec, hy_spec,
        ),
        scratch_shapes=[pltpu.VMEM((groups, tm + 2 * HALO, LANES), F32),
                        pltpu.VMEM((tm, tn), F32)],
        compiler_params=_cparams(("parallel",)),
        name="in_proj",
    )(x2, x2, x2, norm1.reshape(1, d), w_in.astype(BF16), q_norm.reshape(1, LANES),
      k_norm.reshape(1, LANES), perm, conv_w.astype(F32), conv_b.astype(F32).reshape(1, -1))
    return u_q.reshape(g_qkv, m, LANES), z_t, x0c_t


def _t5_bucket(rel):
    half = N_BUCKETS // 2
    exact = half // 2
    n = np.abs(rel)
    large = exact + (np.log(np.maximum(n, 1).astype(np.float32) / np.float32(exact))
                     / np.float32(math.log(REL_MAX_DIST / exact))
                     * np.float32(half - exact)).astype(np.int32)
    large = np.minimum(large, half - 1)
    return np.where(rel > 0, half, 0) + np.where(n < exact, n, large)


def _block_orders(dil):
    sub = PERM // dil
    e, i = np.divmod(np.arange(QBLK), QBLK // sub)
    q_off = sub * i + e
    if dil == 1:
        k_off = np.arange(KBLK) - BAND
    else:
        e, i = np.divmod(np.arange(KBLK), KBLK // sub)
        k_off = sub * i + e - BAND
    return q_off, k_off


def _to_sequence_order_matrix():
    n = PERM * PERM
    mat = np.zeros((n, n), np.float32)
    r, i = np.divmod(np.arange(n), PERM)
    mat[PERM * i + r, np.arange(n)] = 1.0
    return mat


def _attn_bias_tables(rel_bias):
    tabs = []
    for _, dil in ATTN_PATTERNS:
        q_off, k_off = _block_orders(dil)
        rel = k_off[None, :] - q_off[:, None]
        band = np.abs(rel) <= BAND
        first_ok = np.broadcast_to(k_off[None, :] >= 0, rel.shape)
        last_ok = np.broadcast_to(k_off[None, :] < QBLK, rel.shape)
        onehot = np.eye(N_BUCKETS, dtype=np.float32)[_t5_bucket(rel * dil)]
        bias = jnp.einsum("qkn,nh->hqk", jnp.asarray(onehot), rel_bias.astype(F32),
                          precision=lax.Precision.HIGHEST)
        variants = [jnp.where(jnp.asarray(band & ok), bias, NEG)
                    for ok in (first_ok, np.ones_like(band), last_ok)]
        tabs.append(jnp.stack(variants, axis=1))
    return jnp.stack(tabs, axis=1)


def _attn_kernel(*refs, seq, interleave):
    n_cast = len(interleave)
    q_ref, k_ref, v_ref, bias_ref, gain_ref, perm_ref = refs[:6]
    cast_in = refs[6:6 + n_cast]
    o_ref = refs[6 + n_cast]
    cast_out = refs[7 + n_cast:7 + 2 * n_cast]
    qf_ref, kp_ref, vp_ref, kn_ref, vn_ref, acc_ref, m_ref = refs[7 + 2 * n_cast:]

    for src, dst, t in zip(cast_in, cast_out, interleave):
        if t is None:
            dst[...] = src[...].astype(BF16)
        else:
            half = src.shape[1] // 2
            for j in range(half // t):
                dst[:, 2 * j * t:(2 * j + 1) * t] = src[:, j * t:(j + 1) * t].astype(BF16)
                dst[:, (2 * j + 1) * t:(2 * j + 2) * t] = (
                    src[:, half + j * t:half + (j + 1) * t].astype(BF16))

    run = seq // PERM
    chunk = PERM * PERM
    ones = jnp.ones((KBLK, LANES), BF16)
    zpad = jnp.zeros((BAND, LANES), BF16)

    for ref in (kp_ref, vp_ref, kn_ref, vn_ref):
        ref[pl.ds(0, BAND), :] = zpad
        ref[pl.ds(BAND + seq, BAND), :] = zpad
    kp_ref[pl.ds(BAND, seq), :] = k_ref[...]
    vp_ref[pl.ds(BAND, seq), :] = v_ref[...]
    qf_ref[...] = q_ref[...].astype(F32)

    def runs(ref, base, n):
        return jnp.concatenate(
            [ref[pl.ds(pl.multiple_of(r * run + base, n), n), :] for r in range(PERM)], axis=0)

    def to_sequence_order(c, carry):
        base = c * PERM
        kv = jnp.concatenate([runs(k_ref, base, PERM), runs(v_ref, base, PERM)], axis=1)
        nat = jnp.dot(perm_ref[...], kv, preferred_element_type=F32).astype(BF16)
        rows = pl.ds(pl.multiple_of(BAND + c * chunk, BAND), chunk)
        kn_ref[rows, :] = nat[:, :LANES]
        vn_ref[rows, :] = nat[:, LANES:]
        return carry

    lax.fori_loop(0, seq // chunk, to_sequence_order, 0, unroll=16)

    def softmax_block(q, k, v, bias):
        s = lax.dot_general(q, k, (((1,), (1,)), ((), ())), preferred_element_type=F32) + bias
        m_blk = jnp.max(s, axis=-1, keepdims=True)
        p = jnp.exp(s - m_blk).astype(BF16)
        a_l = jnp.dot(p, jnp.concatenate([v, ones], axis=1), preferred_element_type=F32)
        return a_l, jnp.broadcast_to(m_blk, (QBLK, LANES))

    def variant(qb, nb):
        return jnp.where(qb == 0, 0, jnp.where(qb == nb - 1, 2, 1))

    def merge(row_slices, a_l, m_b):
        m_old = jnp.concatenate([m_ref[rows, :] for rows in row_slices], axis=0)
        a_old = jnp.concatenate([acc_ref[rows, :] for rows in row_slices], axis=0)
        m_new = jnp.maximum(m_old, m_b)
        w_old = jnp.exp(m_old - m_new)
        w_blk = jnp.exp(m_b - m_new)
        a_new = (a_old * jnp.concatenate([w_old, w_old], axis=1)
                 + a_l * jnp.concatenate([w_blk, w_blk], axis=1))
        n = QBLK // len(row_slices)
        for e, rows in enumerate(row_slices):
            acc_ref[rows, :] = a_new[e * n:(e + 1) * n]
            m_ref[rows, :] = m_new[e * n:(e + 1) * n]

    nb16 = run // QBLK

    def block16(t, carry):
        r, qb = t // nb16, t % nb16
        off = pl.multiple_of(r * run + qb * QBLK, QBLK)
        a_l, m_b = softmax_block(q_ref[pl.ds(off, QBLK), :], kp_ref[pl.ds(off, KBLK), :],
                                 vp_ref[pl.ds(off, KBLK), :], bias_ref[2, variant(qb, nb16)])
        acc_ref[pl.ds(off, QBLK), :] = a_l
        m_ref[pl.ds(off, QBLK), :] = m_b
        return carry

    lax.fori_loop(0, PERM * nb16, block16, 0, unroll=32)

    dil = ATTN_PATTERNS[1][1]
    sub = PERM // dil
    nb4 = (seq // dil) // QBLK
    qn, kn = QBLK // sub, KBLK // sub

    def block4(t, carry):
        r, qb = t // nb4, t % nb4
        q_rows = [pl.ds(pl.multiple_of((dil * e + r) * run + qb * qn, qn), qn)
                  for e in range(sub)]
        k_rows = [pl.ds(pl.multiple_of((dil * e + r) * run + qb * qn + BAND - BAND // sub,
                                       BAND // sub), kn) for e in range(sub)]
        q = jnp.concatenate([q_ref[rows, :] for rows in q_rows], axis=0)
        k = jnp.concatenate([kp_ref[rows, :] for rows in k_rows], axis=0)
        v = jnp.concatenate([vp_ref[rows, :] for rows in k_rows], axis=0)
        a_l, m_b = softmax_block(q, k, v, bias_ref[1, variant(qb, nb4)])
        merge(q_rows, a_l, m_b)
        return carry

    lax.fori_loop(0, dil * nb4, block4, 0, unroll=32)

    nb1 = seq // QBLK
    qn1 = QBLK // PERM

    def block1(qb, carry):
        q_rows = [pl.ds(pl.multiple_of(r * run + qb * qn1, qn1), qn1) for r in range(PERM)]
        q = jnp.concatenate([qf_ref[rows, :] for rows in q_rows], axis=0).astype(BF16)
        k_rows = pl.ds(pl.multiple_of(qb * QBLK, QBLK), KBLK)
        a_l, m_b = softmax_block(q, kn_ref[k_rows, :], vn_ref[k_rows, :],
                                 bias_ref[0, variant(qb, nb1)])
        merge(q_rows, a_l, m_b)
        return carry

    lax.fori_loop(0, nb1, block1, 0, unroll=32)

    def finish(c, carry):
        st = runs(acc_ref, c * PERM, PERM)
        y = _rms(st[:, :LANES] / st[:, LANES:], gain_ref[...]).astype(BF16)
        nat = jnp.dot(perm_ref[...], y, preferred_element_type=F32)
        o_ref[pl.ds(pl.multiple_of(c * chunk, chunk), chunk), :] = nat.astype(BF16)
        return carry

    lax.fori_loop(0, seq // chunk, finish, 0, unroll=16)


def _cast_slices(weights, n_steps, step_of):
    specs = []
    for w in weights:
        rows, hold = w.shape[0], 1
        while (rows * hold) % n_steps or (rows * hold // n_steps) % 16:
            hold *= 2
        specs.append(pl.BlockSpec((rows * hold // n_steps, w.shape[1]),
                                  lambda *ids, hold=hold: (step_of(*ids) // hold, 0)))
    return specs


def _attention(u_p, bias_tab, attn_out_norm, batch, seq, heads, cast_weights, interleave):
    assert ATTN_PATTERNS[0][1] == 1 and ATTN_PATTERNS[2][1] == PERM
    assert seq % (PERM * QBLK) == 0
    perm = jnp.asarray(_to_sequence_order_matrix(), BF16)
    pad_rows = seq + 2 * BAND
    cast_specs = _cast_slices(cast_weights, batch * heads, lambda b, h: b * heads + h)
    outs = pl.pallas_call(
        functools.partial(_attn_kernel, seq=seq, interleave=tuple(interleave)),
        out_shape=[jax.ShapeDtypeStruct((heads, batch * seq, LANES), BF16)]
        + [jax.ShapeDtypeStruct(w.shape, BF16) for w in cast_weights],
        grid=(batch, heads),
        in_specs=[
            pl.BlockSpec((None, seq, LANES), lambda b, h: (h, b, 0)),
            pl.BlockSpec((None, seq, LANES), lambda b, h: (heads + h, b, 0)),
            pl.BlockSpec((None, seq, LANES), lambda b, h: (2 * heads + h, b, 0)),
            pl.BlockSpec((None, len(ATTN_PATTERNS), 3, QBLK, KBLK),
                         lambda b, h: (h, 0, 0, 0, 0)),
            pl.BlockSpec((None, 1, LANES), lambda b, h: (h, 0, 0)),
            pl.BlockSpec(perm.shape, lambda b, h: (0, 0)),
        ] + cast_specs,
        out_specs=[pl.BlockSpec((None, seq, LANES), lambda b, h: (h, b, 0))] + cast_specs,
        scratch_shapes=[
            pltpu.VMEM((seq, LANES), F32),
            pltpu.VMEM((pad_rows, LANES), BF16),
            pltpu.VMEM((pad_rows, LANES), BF16),
            pltpu.VMEM((pad_rows, LANES), BF16),
            pltpu.VMEM((pad_rows, LANES), BF16),
            pltpu.VMEM((seq, 2 * LANES), F32),
            pltpu.VMEM((seq, LANES), F32),
        ],
        compiler_params=_cparams(("arbitrary", "arbitrary")),
        name="attention",
    )(u_p, u_p, u_p, bias_tab, attn_out_norm.reshape(heads, 1, LANES), perm, *cast_weights)
    return outs[0], outs[1:]


def _filter_features(seq):
    pos = np.arange(seq, dtype=np.float32)
    t = pos / np.float32(max(seq - 1, 1))
    bands = (HY_EMB - 1) // 2
    fr = np.linspace(1e-4, bands - 1, bands, dtype=np.float32)
    ang = np.float32(2.0 * math.pi / seq) * pos[:, None] * fr[None, :]
    z = np.concatenate([t[:, None], np.cos(ang), -np.sin(ang)], axis=-1).astype(np.float32)
    zp = np.zeros((seq, LANES), np.float32)
    zp[:, :HY_EMB] = z
    offs = (np.abs(pos - (seq // 2)) / np.float32(seq / 2)).astype(np.float32)
    return zp, offs[:, None]


def _filter_kernel(z_ref, offs_ref, w1_ref, b1_ref, wi_ref, bi_ref, wo_ref, fq_ref,
                   decay_ref, o_ref):
    hp = lax.Precision.HIGHEST
    fq = fq_ref[...]
    h = jnp.sin(fq * (jnp.dot(z_ref[...], w1_ref[...], precision=hp,
                              preferred_element_type=F32) + b1_ref[...]))
    for j in range(HY_INNER):
        h = jnp.sin(fq * (jnp.dot(h, wi_ref[j], precision=hp,
                                  preferred_element_type=F32) + bi_ref[j]))
    h_hi, h_lo = _split_bf16(h)
    w_hi, w_lo = _split_bf16(wo_ref[...])
    dot = functools.partial(jnp.dot, preferred_element_type=F32)
    filt = dot(h_hi, w_hi) + (dot(h_hi, w_lo) + dot(h_lo, w_hi))
    o_ref[...] = filt * jnp.exp(-offs_ref[...] * jnp.abs(decay_ref[...]))


def _hyena_filter(seq, hy_w, w1, b1, wi, bi, wo, freq, decay):
    zp, offs = _filter_features(seq)
    w1p = jnp.zeros((LANES, HY_FILTER_WIDTH), F32).at[:HY_EMB].set(w1.astype(F32))
    tr = 512
    fw = HY_FILTER_WIDTH
    const = lambda *shape: pl.BlockSpec(shape, lambda i: (0,) * len(shape))
    return pl.pallas_call(
        _filter_kernel,
        out_shape=jax.ShapeDtypeStruct((seq, hy_w), F32),
        grid=(seq // tr,),
        in_specs=[
            pl.BlockSpec((tr, LANES), lambda i: (i, 0)),
            pl.BlockSpec((tr, 1), lambda i: (i, 0)),
            const(LANES, fw), const(1, fw), const(HY_INNER, fw, fw),
            const(HY_INNER, 1, fw), const(fw, hy_w), const(1, fw), const(1, hy_w),
        ],
        out_specs=pl.BlockSpec((tr, hy_w), lambda i: (i, 0)),
        compiler_params=_cparams(("parallel",)),
        name="hyena_filter",
    )(jnp.asarray(zp), jnp.asarray(offs), w1p, b1.reshape(1, fw).astype(F32),
      wi.astype(F32), bi.reshape(HY_INNER, 1, fw).astype(F32), wo.astype(F32),
      freq.reshape(1, fw).astype(F32), decay.reshape(1, hy_w).astype(F32))


def _stack(c):
    return np.block([[c.real, -c.imag], [c.imag, c.real]])


def _fft_len(seq):
    return 3 * seq // 2


@functools.lru_cache(maxsize=None)
def _dft_constants(seq):
    n = _fft_len(seq)
    r = FFT_R
    a_n = n // r
    a_in = seq // r
    a_out = (seq // 2) // r
    ar = np.arange(a_n)
    br = np.arange(r)
    f1 = np.exp(-2j * np.pi * np.outer(ar, ar) / a_n)
    f1_fwd = _stack(f1[:, :a_in])
    f1_fwd_real = np.concatenate([f1[:, :a_in].real, f1[:, :a_in].imag], 0)
    f1_inv = _stack(np.conj(f1).T[a_out:a_out + a_in] / n)
    ph = (br[None, None, :] * br[None, :, None] / r
          + br[None, None, :] * ar[:, None, None] / n)
    g = np.exp(-2j * np.pi * ph)
    g_fwd = np.stack([_stack(g[k]) for k in range(a_n)])
    g_inv = np.stack([_stack(np.conj(g[k]).T) for k in range(a_n)])
    return (f1_fwd.astype(np.float32), f1_fwd_real.astype(np.float32),
            f1_inv.astype(np.float32), g_fwd.astype(np.float32), g_inv.astype(np.float32))


def _split_bf16(x):
    if isinstance(x, np.ndarray):
        hi = x.astype(BF16)
        lo = (x - hi.astype(np.float32)).astype(BF16)
        return jnp.asarray(hi), jnp.asarray(lo)
    hi = x.astype(BF16)
    return hi, (x - hi.astype(F32)).astype(BF16)


def _dot3(m_hi, m_lo, x):
    x_hi, x_lo = _split_bf16(x)
    dot = functools.partial(jnp.dot, preferred_element_type=F32)
    return dot(m_hi, x_hi) + (dot(m_hi, x_lo) + dot(m_lo, x_hi))


def _spectrum_kernel(filt_ref, f1h_ref, f1l_ref, gh_ref, gl_ref, hr_ref, hi_ref,
                     zs_ref, ts_ref, *, seq):
    r = FFT_R
    a_half = seq // r
    a_n = _fft_len(seq) // r
    for a in range(a_half):
        zs_ref[pl.ds(a * FFT_PITCH, r), :] = filt_ref[pl.ds(a * r, r), :]

    def stage1(bp, carry):
        rhs = jnp.concatenate(
            [zs_ref[pl.ds(2 * bp + e, a_half, stride=FFT_PITCH), :] for e in range(2)], axis=1)
        t = _dot3(f1h_ref[...], f1l_ref[...], rhs)
        for e in range(2):
            for c in range(2):
                ts_ref[c, pl.ds(2 * bp + e, a_n, stride=FFT_PITCH), :] = (
                    t[c * a_n:(c + 1) * a_n, e * LANES:(e + 1) * LANES])
        return carry

    lax.fori_loop(0, r // 2, stage1, 0, unroll=16)

    def stage2(ka, carry):
        rows = pl.ds(pl.multiple_of(ka * FFT_PITCH, 8), r)
        rhs = jnp.concatenate([ts_ref[0, rows, :], ts_ref[1, rows, :]], axis=0)
        s = _dot3(gh_ref[ka], gl_ref[ka], rhs)
        out = pl.ds(pl.multiple_of(ka * r, r), r)
        hr_ref[out, :] = s[:r]
        hi_ref[out, :] = s[r:]
        return carry

    lax.fori_loop(0, a_n, stage2, 0, unroll=48)


def _filter_spectrum(filt, seq):
    hy_w = filt.shape[1]
    n = _fft_len(seq)
    _, f1_real, _, g_fwd, _ = _dft_constants(seq)
    f1h, f1l = _split_bf16(f1_real)
    gh, gl = _split_bf16(g_fwd)
    a_n = n // FFT_R
    out = jax.ShapeDtypeStruct((n, hy_w), F32)
    gspec = pl.BlockSpec(g_fwd.shape, lambda c: (0, 0, 0), pipeline_mode=pl.Buffered(1))
    return pl.pallas_call(
        functools.partial(_spectrum_kernel, seq=seq),
        out_shape=(out, out),
        grid=(hy_w // LANES,),
        in_specs=[
            pl.BlockSpec((seq, LANES), lambda c: (0, c)),
            pl.BlockSpec(f1_real.shape, lambda c: (0, 0)),
            pl.BlockSpec(f1_real.shape, lambda c: (0, 0)),
            gspec, gspec,
        ],
        out_specs=(pl.BlockSpec((n, LANES), lambda c: (0, c)),
                   pl.BlockSpec((n, LANES), lambda c: (0, c))),
        scratch_shapes=[
            pltpu.VMEM((seq // FFT_R * FFT_PITCH, LANES), F32),
            pltpu.VMEM((2, a_n * FFT_PITCH, LANES), F32),
        ],
        compiler_params=_cparams(("parallel",)),
        name="filter_spectrum",
    )(filt, f1h, f1l, gh, gl)


def _hy_conv_kernel(z_ref, x_ref, hr_ref, hi_ref, f1_ref, f1i_ref,
                    g_ref, gi_ref, bias_ref, gain_ref, o_ref,
                    zs_ref, ts_ref, ys_ref, *, seq):
    r = FFT_R
    a_half = seq // r
    a_n = _fft_len(seq) // r
    for c in range(2):
        for a in range(a_half):
            zs_ref[c, pl.ds(a * FFT_PITCH, r), :] = (
                z_ref[pl.ds(c * seq + a * r, r), :].astype(F32))

    def stage1(bp, carry):
        cols = []
        for e in range(2):
            rows = pl.ds(2 * bp + e, a_half, stride=FFT_PITCH)
            cols.append(jnp.concatenate([zs_ref[0, rows, :], zs_ref[1, rows, :]], axis=0))
        rhs = jnp.concatenate(cols, axis=1).astype(BF16)
        t = jnp.dot(f1_ref[...], rhs, preferred_element_type=F32)
        for e in range(2):
            for c in range(2):
                ts_ref[c, pl.ds(2 * bp + e, a_n, stride=FFT_PITCH), :] = (
                    t[c * a_n:(c + 1) * a_n, e * LANES:(e + 1) * LANES])
        return carry

    lax.fori_loop(0, r // 2, stage1, 0, unroll=True)

    def stage2(ka, carry):
        rows = pl.ds(pl.multiple_of(ka * FFT_PITCH, 8), r)
        rhs = jnp.concatenate([ts_ref[0, rows, :], ts_ref[1, rows, :]], axis=0).astype(BF16)
        s = jnp.dot(g_ref[ka], rhs, preferred_element_type=F32)
        hrows = pl.ds(pl.multiple_of(ka * r, r), r)
        hr = hr_ref[hrows, :]
        hi = hi_ref[hrows, :]
        sr, si = s[:r], s[r:]
        prod = jnp.concatenate([sr * hr - si * hi, sr * hi + si * hr], axis=0).astype(BF16)
        u = jnp.dot(gi_ref[ka], prod, preferred_element_type=F32)
        ts_ref[0, rows, :] = u[:r]
        ts_ref[1, rows, :] = u[r:]
        return carry

    lax.fori_loop(0, a_n, stage2, 0, unroll=True)

    def stage3(bp, carry):
        cols = []
        for e in range(2):
            rows = pl.ds(2 * bp + e, a_n, stride=FFT_PITCH)
            cols.append(jnp.concatenate([ts_ref[0, rows, :], ts_ref[1, rows, :]], axis=0))
        rhs = jnp.concatenate(cols, axis=1).astype(BF16)
        y = jnp.dot(f1i_ref[...], rhs, preferred_element_type=F32)
        for e in range(2):
            for c in range(2):
                ys_ref[c, pl.ds(2 * bp + e, a_half, stride=FFT_PITCH), :] = (
                    y[c * a_half:(c + 1) * a_half, e * LANES:(e + 1) * LANES])
        return carry

    lax.fori_loop(0, r // 2, stage3, 0, unroll=True)

    def finish(a, carry):
        prow = pl.ds(pl.multiple_of(a * FFT_PITCH, 8), r)
        for c in range(2):
            orow = pl.ds(pl.multiple_of(c * seq + a * r, r), r)
            z = ys_ref[c, prow, :] + zs_ref[c, prow, :] * bias_ref[...]
            y = z * x_ref[orow, :].astype(F32)
            o_ref[orow, :] = _rms(y, gain_ref[...]).astype(BF16)
        return carry

    lax.fori_loop(0, a_half, finish, 0, unroll=True)


def _hyena_conv(z_t, x0c_t, h_re, h_im, hy_bias, hy_out_norm, batch, seq, groups):
    n = _fft_len(seq)
    f1_fwd, _, f1_inv, g_fwd, g_inv = _dft_constants(seq)
    a_n = n // FFT_R
    zspec = pl.BlockSpec((None, 2 * seq, LANES), lambda g, p: (g, p, 0))
    hspec = pl.BlockSpec((n, LANES), lambda g, p: (0, g))
    cspec = lambda arr: pl.BlockSpec(arr.shape, lambda g, p: (0,) * arr.ndim,
                                     pipeline_mode=pl.Buffered(1))
    vspec = pl.BlockSpec((None, 1, LANES), lambda g, p: (g, 0, 0))
    out = jax.ShapeDtypeStruct((groups, batch * seq, LANES), BF16)
    return pl.pallas_call(
        functools.partial(_hy_conv_kernel, seq=seq),
        out_shape=out,
        grid=(groups, batch // 2),
        in_specs=[zspec, zspec, hspec, hspec,
                  cspec(f1_fwd), cspec(f1_inv), cspec(g_fwd), cspec(g_inv), vspec, vspec],
        out_specs=zspec,
        scratch_shapes=[
            pltpu.VMEM((2, seq // FFT_R * FFT_PITCH, LANES), F32),
            pltpu.VMEM((2, a_n * FFT_PITCH, LANES), F32),
            pltpu.VMEM((2, seq // FFT_R * FFT_PITCH, LANES), F32),
        ],
        compiler_params=_cparams(("parallel", "arbitrary")),
        name="hyena_conv",
    )(z_t, x0c_t, h_re, h_im,
      jnp.asarray(f1_fwd, BF16), jnp.asarray(f1_inv, BF16),
      jnp.asarray(g_fwd, BF16), jnp.asarray(g_inv, BF16),
      hy_bias.astype(F32).reshape(groups, 1, LANES),
      hy_out_norm.astype(F32).reshape(groups, 1, LANES))


def _out_proj_kernel(ya_ref, yh_ref, x_ref, w_ref, o_ref, *, heads, groups):
    lhs = jnp.concatenate([ya_ref[g] for g in range(heads)]
                          + [yh_ref[g] for g in range(groups)], axis=-1)
    o_ref[...] = x_ref[...] + jnp.dot(lhs, w_ref[...], preferred_element_type=F32)


def _out_proj(ya_t, yh_t, x2, w_out, tm):
    m, d = x2.shape
    heads, groups = ya_t.shape[0], yh_t.shape[0]
    return pl.pallas_call(
        functools.partial(_out_proj_kernel, heads=heads, groups=groups),
        out_shape=jax.ShapeDtypeStruct((m, d), F32),
        grid=(m // tm,),
        in_specs=[
            pl.BlockSpec((heads, tm, LANES), lambda i: (0, i, 0)),
            pl.BlockSpec((groups, tm, LANES), lambda i: (0, i, 0)),
            pl.BlockSpec((tm, d), lambda i: (i, 0)),
            pl.BlockSpec(w_out.shape, lambda i: (0, 0), pipeline_mode=pl.Buffered(1)),
        ],
        out_specs=pl.BlockSpec((tm, d), lambda i: (i, 0)),
        compiler_params=_cparams(("parallel",)),
        name="out_proj",
    )(ya_t, yh_t, x2, w_out.astype(BF16))


def _ffn_kernel(h_ref, g_ref, wag_ref, wd_ref, o_ref, xn_ref):
    def hidden_tile(xn):
        th = wd_ref.shape[0]
        ag = jnp.dot(xn, wag_ref[...], preferred_element_type=F32)
        a, g = ag[:, :th], ag[:, th:]
        act = (a * jax.nn.sigmoid(a) * g).astype(BF16)
        return jnp.dot(act, wd_ref[...], preferred_element_type=F32)

    @pl.when(pl.program_id(1) == 0)
    def _():
        chunk = 256
        for c in range(h_ref.shape[0] // chunk):
            rows = pl.ds(c * chunk, chunk)
            h = h_ref[rows, :]
            xn = _rms(h, g_ref[...]).astype(BF16)
            xn_ref[rows, :] = xn
            o_ref[rows, :] = h + hidden_tile(xn)

    @pl.when(pl.program_id(1) > 0)
    def _():
        o_ref[...] += hidden_tile(xn_ref[...])


def _ffn(h1, norm2, w_ag, w_down, tm, th):
    m, d = h1.shape
    hidden = w_down.shape[0]
    return pl.pallas_call(
        _ffn_kernel,
        out_shape=jax.ShapeDtypeStruct((m, d), F32),
        grid=(m // tm, hidden // th),
        in_specs=[
            pl.BlockSpec((tm, d), lambda i, j: (i, 0)),
            pl.BlockSpec((1, d), lambda i, j: (0, 0)),
            pl.BlockSpec((d, 2 * th), lambda i, j: (0, j)),
            pl.BlockSpec((th, d), lambda i, j: (j, 0)),
        ],
        out_specs=pl.BlockSpec((tm, d), lambda i, j: (i, 0)),
        scratch_shapes=[pltpu.VMEM((tm, d), BF16)],
        compiler_params=_cparams(("parallel", "arbitrary")),
        name="ffn",
    )(h1, norm2.reshape(1, d), w_ag, w_down.astype(BF16))


def _ple_kernel(h_ref, p_ref, gn_ref, wg_ref, wp_ref, pn_ref, o_ref):
    chunk = 256
    for c in range(h_ref.shape[0] // chunk):
        rows = pl.ds(c * chunk, chunk)
        h = h_ref[rows, :]
        e = _rms(jnp.dot(p_ref[rows, :].astype(BF16), wp_ref[...],
                         preferred_element_type=F32), pn_ref[...])
        inv = lax.rsqrt(jnp.mean(h * h, axis=-1, keepdims=True) + EPS)
        logits = jnp.dot((h * gn_ref[...]).astype(BF16), wg_ref[...],
                         preferred_element_type=F32)
        o_ref[rows, :] = h + jax.nn.sigmoid(logits * inv) * e


def _ple(h2, p2, ple_norm, w_gate, w_proj, ple_post_norm, tm):
    m, d = h2.shape
    pd = p2.shape[1]
    return pl.pallas_call(
        _ple_kernel,
        out_shape=jax.ShapeDtypeStruct((m, d), F32),
        grid=(m // tm,),
        in_specs=[
            pl.BlockSpec((tm, d), lambda i: (i, 0)),
            pl.BlockSpec((tm, pd), lambda i: (i, 0)),
            pl.BlockSpec((1, d), lambda i: (0, 0)),
            pl.BlockSpec((d, d), lambda i: (0, 0), pipeline_mode=pl.Buffered(1)),
            pl.BlockSpec((pd, d), lambda i: (0, 0), pipeline_mode=pl.Buffered(1)),
            pl.BlockSpec((1, d), lambda i: (0, 0)),
        ],
        out_specs=pl.BlockSpec((tm, d), lambda i: (i, 0)),
        compiler_params=_cparams(("parallel",)),
        name="ple",
    )(h2, p2, ple_norm.reshape(1, d), w_gate.astype(BF16), w_proj.astype(BF16),
      ple_post_norm.reshape(1, d))


def kernel(x, p, rel_bias, norm1, w_in, q_norm, k_norm, conv_w, conv_b, hy_w1, hy_b1, hy_wi, hy_bi, hy_wo, hy_freq, hy_decay, hy_bias, attn_out_norm, hy_out_norm, w_out, norm2, w_gu, w_down, ple_norm, w_ple_gate, w_ple_proj, ple_post_norm):
    batch, seq, d = x.shape
    attn_w = d // 2
    heads = attn_w // HEAD_DIM
    groups = heads
    hy_w = groups * LANES
    m = batch * seq
    tm = min(1024, m)
    th = 512
    h = x.reshape(m, d)
    bias_tab = _attn_bias_tables(rel_bias)
    for i in range(norm1.shape[0]):
        h3 = h.reshape(batch, seq, d)
        u_p, z_t, x0c_t = _in_proj(h3, norm1[i], w_in[i], q_norm[i], k_norm[i], conv_w[i],
                                   conv_b[i], attn_w, tm // 2)
        ya_t, (wo, wgu, wdn, wpg, wpp) = _attention(
            u_p, bias_tab, attn_out_norm[i], batch, seq, heads,
            [w_out[i], w_gu[i], w_down[i], w_ple_gate[i], w_ple_proj[i]],
            [None, th, None, None, None])
        filt = _hyena_filter(seq, hy_w, hy_w1[i], hy_b1[i], hy_wi[i], hy_bi[i], hy_wo[i],
                             hy_freq[i], hy_decay[i])
        h_re, h_im = _filter_spectrum(filt, seq)
        yh_t = _hyena_conv(z_t, x0c_t, h_re, h_im, hy_bias[i], hy_out_norm[i],
                           batch, seq, groups)
        h = _out_proj(ya_t, yh_t, h, wo, tm)
        h = _ffn(h, norm2[i], wgu, wdn, tm, th)
        h = _ple(h, p[i].reshape(m, PLE_DIM), ple_norm[i], wpg, wpp, ple_post_norm[i], tm)
    return h.reshape(batch, seq, d)
```

```python
import functools
import math

import jax
import jax.numpy as jnp
import numpy as np
from jax import lax
from jax.experimental import pallas as pl
from jax.experimental.pallas import tpu as pltpu

F32 = jnp.float32
BF16 = jnp.bfloat16

LANES = 128
HEAD_DIM = 128
ATTN_PATTERNS = ((128, 1), (512, 4), (2048, 16))
PERM = 16
HALO = 16
BAND = 64
QBLK = 128
KBLK = QBLK + 2 * BAND
N_BUCKETS = 32
REL_MAX_DIST = 1024
HY_EMB = 33
HY_FILTER_WIDTH = 64
HY_INNER = 2
EPS = 1e-6
NEG = -1e30
PLE_DIM = 256

FFT_R = 64
FFT_PITCH = FFT_R + 8
VMEM_LIMIT = 56 * 1024 * 1024


def _cparams(sem, vmem=VMEM_LIMIT):
    return pltpu.CompilerParams(dimension_semantics=sem, vmem_limit_bytes=vmem)


def _rms(x, gain):
    ms = jnp.mean(x * x, axis=-1, keepdims=True)
    return x * lax.rsqrt(ms + EPS) * gain


def _in_proj_kernel(x_ref, xa_ref, xb_ref, g_ref, w_ref, qg_ref, kg_ref, perm_ref,
                    cw_ref, cb_ref, oq_ref, z_ref, x0_ref, ext_ref, x1_ref, *,
                    attn_w, tn, tiles_per_seq):
    tm = x_ref.shape[0]
    chunk = PERM * PERM
    hpt = tn // LANES
    n_qkv = 3 * attn_w // tn
    dot = functools.partial(jnp.dot, preferred_element_type=F32)

    xn, xp = [], []
    for c in range(tm // chunk):
        xn.append(_rms(x_ref[pl.ds(c * chunk, chunk), :], g_ref[...]).astype(BF16))
        xp.append(dot(perm_ref[...], xn[c]).astype(BF16))

    for t in range(n_qkv):
        w = w_ref[:, t * tn:(t + 1) * tn]
        gain = (qg_ref[...] * (HEAD_DIM ** -0.5) if t * tn < attn_w
                else kg_ref[...] if t * tn < 2 * attn_w else None)
        for c in range(tm // chunk):
            acc = dot(xp[c], w)
            for hh in range(hpt):
                a = acc[:, hh * LANES:(hh + 1) * LANES]
                if gain is not None:
                    a = _rms(a, gain)
                a = a.astype(BF16)
                for r in range(PERM):
                    oq_ref[t * hpt + hh, r, pl.ds(c * PERM, PERM), :] = (
                        a[r * PERM:(r + 1) * PERM])

    tile = pl.program_id(0) % tiles_per_seq
    before = _rms(xa_ref[...], g_ref[...]) * jnp.where(tile == 0, 0.0, 1.0)
    after = _rms(xb_ref[...], g_ref[...]) * jnp.where(tile == tiles_per_seq - 1, 0.0, 1.0)
    xe = jnp.concatenate([before.astype(BF16)] + xn + [after.astype(BF16)], axis=0)
    for kind in range(3):
        col = (n_qkv + kind) * tn
        acc = dot(xe, w_ref[:, col:col + tn])
        for hh in range(hpt):
            lanes = slice(hh * LANES, (hh + 1) * LANES)
            ext_ref[hh] = acc[:, lanes]
            taps = cw_ref[:, kind * tn + hh * LANES:kind * tn + (hh + 1) * LANES]
            y = (ext_ref[hh, pl.ds(HALO - 1, tm), :] * taps[0:1]
                 + ext_ref[hh, pl.ds(HALO, tm), :] * taps[1:2]
                 + ext_ref[hh, pl.ds(HALO + 1, tm), :] * taps[2:3]
                 + cb_ref[:, kind * tn + hh * LANES:kind * tn + (hh + 1) * LANES])
            if kind == 0:
                x0_ref[hh] = y.astype(BF16)
            elif kind == 1:
                x1_ref[:, lanes] = y
            else:
                z_ref[hh] = (y * x1_ref[:, lanes]).astype(BF16)


def _in_proj(x3, norm1, w_in, q_norm, k_norm, conv_w, conv_b, attn_w, tm):
    batch, seq, d = x3.shape
    m = batch * seq
    in_w = w_in.shape[1]
    tn = min(1024, attn_w)
    g_qkv = 3 * attn_w // LANES
    groups = tn // LANES
    tiles_per_seq = seq // tm
    assert tm % (PERM * PERM) == 0 and seq % tm == 0 and tm % HALO == 0
    assert in_w == 3 * attn_w + 3 * tn, "one column tile per Hyena operand"
    perm = jnp.asarray(_to_sequence_order_matrix().T, BF16)
    const = lambda shape: pl.BlockSpec(shape, lambda i: (0,) * len(shape))
    halo_blocks = tm // HALO
    x2 = x3.reshape(m, d)
    hy_out = jax.ShapeDtypeStruct((groups, m, LANES), BF16)
    hy_spec = pl.BlockSpec((groups, tm, LANES), lambda i: (0, i, 0))
    u_q, z_t, x0c_t = pl.pallas_call(
        functools.partial(_in_proj_kernel, attn_w=attn_w, tn=tn, tiles_per_seq=tiles_per_seq),
        out_shape=(jax.ShapeDtypeStruct((g_qkv, batch, PERM, seq // PERM, LANES), BF16),
                   hy_out, hy_out),
        grid=(m // tm,),
        in_specs=[
            pl.BlockSpec((tm, d), lambda i: (i, 0)),
            pl.BlockSpec((HALO, d), lambda i: (jnp.maximum(i * halo_blocks - 1, 0), 0)),
            pl.BlockSpec((HALO, d),
                         lambda i: (jnp.minimum((i + 1) * halo_blocks, m // HALO - 1), 0)),
            const((1, d)),
            pl.BlockSpec((d, in_w), lambda i: (0, 0), pipeline_mode=pl.Buffered(1)),
            const((1, LANES)), const((1, LANES)), const(perm.shape),
            const(conv_w.shape), const((1, conv_b.shape[0])),
        ],
        out_specs=(
            pl.BlockSpec((g_qkv, None, PERM, tm // PERM, LANES),
                         lambda i: (0, i // tiles_per_seq, 0, i % tiles_per_seq, 0)),
            hy_spec, hy_spec,
        ),
        scratch_shapes=[pltpu.VMEM((groups, tm + 2 * HALO, LANES), F32),
                        pltpu.VMEM((tm, tn), F32)],
        compiler_params=_cparams(("parallel",)),
        name="in_proj",
    )(x2, x2, x2, norm1.reshape(1, d), w_in.astype(BF16), q_norm.reshape(1, LANES),
      k_norm.reshape(1, LANES), perm, conv_w.astype(F32), conv_b.astype(F32).reshape(1, -1))
    return u_q.reshape(g_qkv, m, LANES), z_t, x0c_t


def _t5_bucket(rel):
    half = N_BUCKETS // 2
    exact = half // 2
    n = np.abs(rel)
    large = exact + (np.log(np.maximum(n, 1).astype(np.float32) / np.float32(exact))
                     / np.float32(math.log(REL_MAX_DIST / exact))
                     * np.float32(half - exact)).astype(np.int32)
    large = np.minimum(large, half - 1)
    return np.where(rel > 0, half, 0) + np.where(n < exact, n, large)


def _block_orders(dil):
    sub = PERM // dil
    e, i = np.divmod(np.arange(QBLK), QBLK // sub)
    q_off = sub * i + e
    if dil == 1:
        k_off = np.arange(KBLK) - BAND
    else:
        e, i = np.divmod(np.arange(KBLK), KBLK // sub)
        k_off = sub * i + e - BAND
    return q_off, k_off


def _to_sequence_order_matrix():
    n = PERM * PERM
    mat = np.zeros((n, n), np.float32)
    r, i = np.divmod(np.arange(n), PERM)
    mat[PERM * i + r, np.arange(n)] = 1.0
    return mat


def _attn_bias_tables(rel_bias):
    tabs = []
    for _, dil in ATTN_PATTERNS:
        q_off, k_off = _block_orders(dil)
        rel = k_off[None, :] - q_off[:, None]
        band = np.abs(rel) <= BAND
        first_ok = np.broadcast_to(k_off[None, :] >= 0, rel.shape)
        last_ok = np.broadcast_to(k_off[None, :] < QBLK, rel.shape)
        onehot = np.eye(N_BUCKETS, dtype=np.float32)[_t5_bucket(rel * dil)]
        bias = jnp.einsum("qkn,nh->hqk", jnp.asarray(onehot), rel_bias.astype(F32),
                          precision=lax.Precision.HIGHEST)
        variants = [jnp.where(jnp.asarray(band & ok), bias, NEG)
                    for ok in (first_ok, np.ones_like(band), last_ok)]
        tabs.append(jnp.stack(variants, axis=1))
    return jnp.stack(tabs, axis=1)


def _attn_kernel(*refs, seq, n_cast):
    q_ref, k_ref, v_ref, bias_ref, gain_ref, perm_ref = refs[:6]
    cast_in = refs[6:6 + n_cast]
    o_ref = refs[6 + n_cast]
    cast_out = refs[7 + n_cast:7 + 2 * n_cast]
    qf_ref, kp_ref, vp_ref, kn_ref, vn_ref, acc_ref, m_ref = refs[7 + 2 * n_cast:]

    for src, dst in zip(cast_in, cast_out):
        dst[...] = src[...].astype(BF16)

    run = seq // PERM
    chunk = PERM * PERM
    ones = jnp.ones((KBLK, LANES), BF16)
    zpad = jnp.zeros((BAND, LANES), BF16)

    for ref in (kp_ref, vp_ref, kn_ref, vn_ref):
        ref[pl.ds(0, BAND), :] = zpad
        ref[pl.ds(BAND + seq, BAND), :] = zpad
    kp_ref[pl.ds(BAND, seq), :] = k_ref[...]
    vp_ref[pl.ds(BAND, seq), :] = v_ref[...]
    qf_ref[...] = q_ref[...].astype(F32)

    def runs(ref, base, n):
        return jnp.concatenate(
            [ref[pl.ds(pl.multiple_of(r * run + base, n), n), :] for r in range(PERM)], axis=0)

    def to_sequence_order(c, carry):
        base = c * PERM
        kv = jnp.concatenate([runs(k_ref, base, PERM), runs(v_ref, base, PERM)], axis=1)
        nat = jnp.dot(perm_ref[...], kv, preferred_element_type=F32).astype(BF16)
        rows = pl.ds(pl.multiple_of(BAND + c * chunk, BAND), chunk)
        kn_ref[rows, :] = nat[:, :LANES]
        vn_ref[rows, :] = nat[:, LANES:]
        return carry

    lax.fori_loop(0, seq // chunk, to_sequence_order, 0, unroll=16)

    def softmax_block(q, k, v, bias):
        s = lax.dot_general(q, k, (((1,), (1,)), ((), ())), preferred_element_type=F32) + bias
        m_blk = jnp.max(s, axis=-1, keepdims=True)
        p = jnp.exp(s - m_blk).astype(BF16)
        a_l = jnp.dot(p, jnp.concatenate([v, ones], axis=1), preferred_element_type=F32)
        return a_l, jnp.broadcast_to(m_blk, (QBLK, LANES))

    def variant(qb, nb):
        return jnp.where(qb == 0, 0, jnp.where(qb == nb - 1, 2, 1))

    def merge(row_slices, a_l, m_b):
        m_old = jnp.concatenate([m_ref[rows, :] for rows in row_slices], axis=0)
        a_old = jnp.concatenate([acc_ref[rows, :] for rows in row_slices], axis=0)
        m_new = jnp.maximum(m_old, m_b)
        w_old = jnp.exp(m_old - m_new)
        w_blk = jnp.exp(m_b - m_new)
        a_new = (a_old * jnp.concatenate([w_old, w_old], axis=1)
                 + a_l * jnp.concatenate([w_blk, w_blk], axis=1))
        n = QBLK // len(row_slices)
        for e, rows in enumerate(row_slices):
            acc_ref[rows, :] = a_new[e * n:(e + 1) * n]
            m_ref[rows, :] = m_new[e * n:(e + 1) * n]

    nb16 = run // QBLK

    def block16(t, carry):
        r, qb = t // nb16, t % nb16
        off = pl.multiple_of(r * run + qb * QBLK, QBLK)
        a_l, m_b = softmax_block(q_ref[pl.ds(off, QBLK), :], kp_ref[pl.ds(off, KBLK), :],
                                 vp_ref[pl.ds(off, KBLK), :], bias_ref[2, variant(qb, nb16)])
        acc_ref[pl.ds(off, QBLK), :] = a_l
        m_ref[pl.ds(off, QBLK), :] = m_b
        return carry

    lax.fori_loop(0, PERM * nb16, block16, 0, unroll=32)

    dil = ATTN_PATTERNS[1][1]
    sub = PERM // dil
    nb4 = (seq // dil) // QBLK
    qn, kn = QBLK // sub, KBLK // sub

    def block4(t, carry):
        r, qb = t // nb4, t % nb4
        q_rows = [pl.ds(pl.multiple_of((dil * e + r) * run + qb * qn, qn), qn)
                  for e in range(sub)]
        k_rows = [pl.ds(pl.multiple_of((dil * e + r) * run + qb * qn + BAND - BAND // sub,
                                       BAND // sub), kn) for e in range(sub)]
        q = jnp.concatenate([q_ref[rows, :] for rows in q_rows], axis=0)
        k = jnp.concatenate([kp_ref[rows, :] for rows in k_rows], axis=0)
        v = jnp.concatenate([vp_ref[rows, :] for rows in k_rows], axis=0)
        a_l, m_b = softmax_block(q, k, v, bias_ref[1, variant(qb, nb4)])
        merge(q_rows, a_l, m_b)
        return carry

    lax.fori_loop(0, dil * nb4, block4, 0, unroll=32)

    nb1 = seq // QBLK
    qn1 = QBLK // PERM

    def block1(qb, carry):
        q_rows = [pl.ds(pl.multiple_of(r * run + qb * qn1, qn1), qn1) for r in range(PERM)]
        q = jnp.concatenate([qf_ref[rows, :] for rows in q_rows], axis=0).astype(BF16)
        k_rows = pl.ds(pl.multiple_of(qb * QBLK, QBLK), KBLK)
        a_l, m_b = softmax_block(q, kn_ref[k_rows, :], vn_ref[k_rows, :],
                                 bias_ref[0, variant(qb, nb1)])
        merge(q_rows, a_l, m_b)
        return carry

    lax.fori_loop(0, nb1, block1, 0, unroll=32)

    def finish(c, carry):
        st = runs(acc_ref, c * PERM, PERM)
        y = _rms(st[:, :LANES] / st[:, LANES:], gain_ref[...]).astype(BF16)
        nat = jnp.dot(perm_ref[...], y, preferred_element_type=F32)
        o_ref[pl.ds(pl.multiple_of(c * chunk, chunk), chunk), :] = nat.astype(BF16)
        return carry

    lax.fori_loop(0, seq // chunk, finish, 0, unroll=16)


def _cast_slices(weights, n_steps, step_of):
    specs = []
    for w in weights:
        rows, hold = w.shape[0], 1
        while (rows * hold) % n_steps or (rows * hold // n_steps) % 16:
            hold *= 2
        specs.append(pl.BlockSpec((rows * hold // n_steps, w.shape[1]),
                                  lambda *ids, hold=hold: (step_of(*ids) // hold, 0)))
    return specs


def _attention(u_p, bias_tab, attn_out_norm, batch, seq, heads, cast_weights):
    assert ATTN_PATTERNS[0][1] == 1 and ATTN_PATTERNS[2][1] == PERM
    assert seq % (PERM * QBLK) == 0
    perm = jnp.asarray(_to_sequence_order_matrix(), BF16)
    pad_rows = seq + 2 * BAND
    cast_specs = _cast_slices(cast_weights, batch * heads, lambda b, h: b * heads + h)
    outs = pl.pallas_call(
        functools.partial(_attn_kernel, seq=seq, n_cast=len(cast_weights)),
        out_shape=[jax.ShapeDtypeStruct((heads, batch * seq, LANES), BF16)]
        + [jax.ShapeDtypeStruct(w.shape, BF16) for w in cast_weights],
        grid=(batch, heads),
        in_specs=[
            pl.BlockSpec((None, seq, LANES), lambda b, h: (h, b, 0)),
            pl.BlockSpec((None, seq, LANES), lambda b, h: (heads + h, b, 0)),
            pl.BlockSpec((None, seq, LANES), lambda b, h: (2 * heads + h, b, 0)),
            pl.BlockSpec((None, len(ATTN_PATTERNS), 3, QBLK, KBLK),
                         lambda b, h: (h, 0, 0, 0, 0)),
            pl.BlockSpec((None, 1, LANES), lambda b, h: (h, 0, 0)),
            pl.BlockSpec(perm.shape, lambda b, h: (0, 0)),
        ] + cast_specs,
        out_specs=[pl.BlockSpec((None, seq, LANES), lambda b, h: (h, b, 0))] + cast_specs,
        scratch_shapes=[
            pltpu.VMEM((seq, LANES), F32),
            pltpu.VMEM((pad_rows, LANES), BF16),
            pltpu.VMEM((pad_rows, LANES), BF16),
            pltpu.VMEM((pad_rows, LANES), BF16),
            pltpu.VMEM((pad_rows, LANES), BF16),
            pltpu.VMEM((seq, 2 * LANES), F32),
            pltpu.VMEM((seq, LANES), F32),
        ],
        compiler_params=_cparams(("arbitrary", "arbitrary")),
        name="attention",
    )(u_p, u_p, u_p, bias_tab, attn_out_norm.reshape(heads, 1, LANES), perm, *cast_weights)
    return outs[0], outs[1:]


def _filter_features(seq):
    pos = np.arange(seq, dtype=np.float32)
    t = pos / np.float32(max(seq - 1, 1))
    bands = (HY_EMB - 1) // 2
    fr = np.linspace(1e-4, bands - 1, bands, dtype=np.float32)
    ang = np.float32(2.0 * math.pi / seq) * pos[:, None] * fr[None, :]
    z = np.concatenate([t[:, None], np.cos(ang), -np.sin(ang)], axis=-1).astype(np.float32)
    zp = np.zeros((seq, LANES), np.float32)
    zp[:, :HY_EMB] = z
    offs = (np.abs(pos - (seq // 2)) / np.float32(seq / 2)).astype(np.float32)
    return zp, offs[:, None]


def _filter_kernel(z_ref, offs_ref, w1_ref, b1_ref, wi_ref, bi_ref, wo_ref, fq_ref,
                   decay_ref, o_ref):
    hp = lax.Precision.HIGHEST
    fq = fq_ref[...]
    h = jnp.sin(fq * (jnp.dot(z_ref[...], w1_ref[...], precision=hp,
                              preferred_element_type=F32) + b1_ref[...]))
    for j in range(HY_INNER):
        h = jnp.sin(fq * (jnp.dot(h, wi_ref[j], precision=hp,
                                  preferred_element_type=F32) + bi_ref[j]))
    h_hi, h_lo = _split_bf16(h)
    w_hi, w_lo = _split_bf16(wo_ref[...])
    dot = functools.partial(jnp.dot, preferred_element_type=F32)
    filt = dot(h_hi, w_hi) + (dot(h_hi, w_lo) + dot(h_lo, w_hi))
    o_ref[...] = filt * jnp.exp(-offs_ref[...] * jnp.abs(decay_ref[...]))


def _hyena_filter(seq, hy_w, w1, b1, wi, bi, wo, freq, decay):
    zp, offs = _filter_features(seq)
    w1p = jnp.zeros((LANES, HY_FILTER_WIDTH), F32).at[:HY_EMB].set(w1.astype(F32))
    tr = 512
    fw = HY_FILTER_WIDTH
    const = lambda *shape: pl.BlockSpec(shape, lambda i: (0,) * len(shape))
    return pl.pallas_call(
        _filter_kernel,
        out_shape=jax.ShapeDtypeStruct((seq, hy_w), F32),
        grid=(seq // tr,),
        in_specs=[
            pl.BlockSpec((tr, LANES), lambda i: (i, 0)),
            pl.BlockSpec((tr, 1), lambda i: (i, 0)),
            const(LANES, fw), const(1, fw), const(HY_INNER, fw, fw),
            const(HY_INNER, 1, fw), const(fw, hy_w), const(1, fw), const(1, hy_w),
        ],
        out_specs=pl.BlockSpec((tr, hy_w), lambda i: (i, 0)),
        compiler_params=_cparams(("parallel",)),
        name="hyena_filter",
    )(jnp.asarray(zp), jnp.asarray(offs), w1p, b1.reshape(1, fw).astype(F32),
      wi.astype(F32), bi.reshape(HY_INNER, 1, fw).astype(F32), wo.astype(F32),
      freq.reshape(1, fw).astype(F32), decay.reshape(1, hy_w).astype(F32))


def _stack(c):
    return np.block([[c.real, -c.imag], [c.imag, c.real]])


def _fft_len(seq):
    return 3 * seq // 2


@functools.lru_cache(maxsize=None)
def _dft_constants(seq):
    n = _fft_len(seq)
    r = FFT_R
    a_n = n // r
    a_in = seq // r
    a_out = (seq // 2) // r
    ar = np.arange(a_n)
    br = np.arange(r)
    f1 = np.exp(-2j * np.pi * np.outer(ar, ar) / a_n)
    f1_fwd = _stack(f1[:, :a_in])
    f1_fwd_real = np.concatenate([f1[:, :a_in].real, f1[:, :a_in].imag], 0)
    f1_inv = _stack(np.conj(f1).T[a_out:a_out + a_in] / n)
    ph = (br[None, None, :] * br[None, :, None] / r
          + br[None, None, :] * ar[:, None, None] / n)
    g = np.exp(-2j * np.pi * ph)
    g_fwd = np.stack([_stack(g[k]) for k in range(a_n)])
    g_inv = np.stack([_stack(np.conj(g[k]).T) for k in range(a_n)])
    return (f1_fwd.astype(np.float32), f1_fwd_real.astype(np.float32),
            f1_inv.astype(np.float32), g_fwd.astype(np.float32), g_inv.astype(np.float32))


def _split_bf16(x):
    if isinstance(x, np.ndarray):
        hi = x.astype(BF16)
        lo = (x - hi.astype(np.float32)).astype(BF16)
        return jnp.asarray(hi), jnp.asarray(lo)
    hi = x.astype(BF16)
    return hi, (x - hi.astype(F32)).astype(BF16)


def _dot3(m_hi, m_lo, x):
    x_hi, x_lo = _split_bf16(x)
    dot = functools.partial(jnp.dot, preferred_element_type=F32)
    return dot(m_hi, x_hi) + (dot(m_hi, x_lo) + dot(m_lo, x_hi))


def _spectrum_kernel(filt_ref, f1h_ref, f1l_ref, gh_ref, gl_ref, hr_ref, hi_ref,
                     zs_ref, ts_ref, *, seq):
    r = FFT_R
    a_half = seq // r
    a_n = _fft_len(seq) // r
    for a in range(a_half):
        zs_ref[pl.ds(a * FFT_PITCH, r), :] = filt_ref[pl.ds(a * r, r), :]

    def stage1(bp, carry):
        rhs = jnp.concatenate(
            [zs_ref[pl.ds(2 * bp + e, a_half, stride=FFT_PITCH), :] for e in range(2)], axis=1)
        t = _dot3(f1h_ref[...], f1l_ref[...], rhs)
        for e in range(2):
            for c in range(2):
                ts_ref[c, pl.ds(2 * bp + e, a_n, stride=FFT_PITCH), :] = (
                    t[c * a_n:(c + 1) * a_n, e * LANES:(e + 1) * LANES])
        return carry

    lax.fori_loop(0, r // 2, stage1, 0, unroll=16)

    def stage2(ka, carry):
        rows = pl.ds(pl.multiple_of(ka * FFT_PITCH, 8), r)
        rhs = jnp.concatenate([ts_ref[0, rows, :], ts_ref[1, rows, :]], axis=0)
        s = _dot3(gh_ref[ka], gl_ref[ka], rhs)
        out = pl.ds(pl.multiple_of(ka * r, r), r)
        hr_ref[out, :] = s[:r]
        hi_ref[out, :] = s[r:]
        return carry

    lax.fori_loop(0, a_n, stage2, 0, unroll=48)


def _filter_spectrum(filt, seq):
    hy_w = filt.shape[1]
    n = _fft_len(seq)
    _, f1_real, _, g_fwd, _ = _dft_constants(seq)
    f1h, f1l = _split_bf16(f1_real)
    gh, gl = _split_bf16(g_fwd)
    a_n = n // FFT_R
    out = jax.ShapeDtypeStruct((n, hy_w), F32)
    gspec = pl.BlockSpec(g_fwd.shape, lambda c: (0, 0, 0), pipeline_mode=pl.Buffered(1))
    return pl.pallas_call(
        functools.partial(_spectrum_kernel, seq=seq),
        out_shape=(out, out),
        grid=(hy_w // LANES,),
        in_specs=[
            pl.BlockSpec((seq, LANES), lambda c: (0, c)),
            pl.BlockSpec(f1_real.shape, lambda c: (0, 0)),
            pl.BlockSpec(f1_real.shape, lambda c: (0, 0)),
            gspec, gspec,
        ],
        out_specs=(pl.BlockSpec((n, LANES), lambda c: (0, c)),
                   pl.BlockSpec((n, LANES), lambda c: (0, c))),
        scratch_shapes=[
            pltpu.VMEM((seq // FFT_R * FFT_PITCH, LANES), F32),
            pltpu.VMEM((2, a_n * FFT_PITCH, LANES), F32),
        ],
        compiler_params=_cparams(("parallel",)),
        name="filter_spectrum",
    )(filt, f1h, f1l, gh, gl)


def _hy_conv_kernel(z_ref, x_ref, hr_ref, hi_ref, f1_ref, f1i_ref,
                    g_ref, gi_ref, bias_ref, gain_ref, o_ref,
                    zs_ref, ts_ref, ys_ref, *, seq):
    r = FFT_R
    a_half = seq // r
    a_n = _fft_len(seq) // r
    for c in range(2):
        for a in range(a_half):
            zs_ref[c, pl.ds(a * FFT_PITCH, r), :] = (
                z_ref[pl.ds(c * seq + a * r, r), :].astype(F32))

    def stage1(bp, carry):
        cols = []
        for e in range(2):
            rows = pl.ds(2 * bp + e, a_half, stride=FFT_PITCH)
            cols.append(jnp.concatenate([zs_ref[0, rows, :], zs_ref[1, rows, :]], axis=0))
        rhs = jnp.concatenate(cols, axis=1).astype(BF16)
        t = jnp.dot(f1_ref[...], rhs, preferred_element_type=F32)
        for e in range(2):
            for c in range(2):
                ts_ref[c, pl.ds(2 * bp + e, a_n, stride=FFT_PITCH), :] = (
                    t[c * a_n:(c + 1) * a_n, e * LANES:(e + 1) * LANES])
        return carry

    lax.fori_loop(0, r // 2, stage1, 0, unroll=True)

    def stage2(ka, carry):
        rows = pl.ds(pl.multiple_of(ka * FFT_PITCH, 8), r)
        rhs = jnp.concatenate([ts_ref[0, rows, :], ts_ref[1, rows, :]], axis=0).astype(BF16)
        s = jnp.dot(g_ref[ka], rhs, preferred_element_type=F32)
        hrows = pl.ds(pl.multiple_of(ka * r, r), r)
        hr = hr_ref[hrows, :]
        hi = hi_ref[hrows, :]
        sr, si = s[:r], s[r:]
        prod = jnp.concatenate([sr * hr - si * hi, sr * hi + si * hr], axis=0).astype(BF16)
        u = jnp.dot(gi_ref[ka], prod, preferred_element_type=F32)
        ts_ref[0, rows, :] = u[:r]
        ts_ref[1, rows, :] = u[r:]
        return carry

    lax.fori_loop(0, a_n, stage2, 0, unroll=True)

    def stage3(bp, carry):
        cols = []
        for e in range(2):
            rows = pl.ds(2 * bp + e, a_n, stride=FFT_PITCH)
            cols.append(jnp.concatenate([ts_ref[0, rows, :], ts_ref[1, rows, :]], axis=0))
        rhs = jnp.concatenate(cols, axis=1).astype(BF16)
        y = jnp.dot(f1i_ref[...], rhs, preferred_element_type=F32)
        for e in range(2):
            for c in range(2):
                ys_ref[c, pl.ds(2 * bp + e, a_half, stride=FFT_PITCH), :] = (
                    y[c * a_half:(c + 1) * a_half, e * LANES:(e + 1) * LANES])
        return carry

    lax.fori_loop(0, r // 2, stage3, 0, unroll=True)

    def finish(a, carry):
        prow = pl.ds(pl.multiple_of(a * FFT_PITCH, 8), r)
        for c in range(2):
            orow = pl.ds(pl.multiple_of(c * seq + a * r, r), r)
            z = ys_ref[c, prow, :] + zs_ref[c, prow, :] * bias_ref[...]
            y = z * x_ref[orow, :].astype(F32)
            o_ref[orow, :] = _rms(y, gain_ref[...]).astype(BF16)
        return carry

    lax.fori_loop(0, a_half, finish, 0, unroll=True)


def _hyena_conv(z_t, x0c_t, h_re, h_im, hy_bias, hy_out_norm, batch, seq, groups):
    n = _fft_len(seq)
    f1_fwd, _, f1_inv, g_fwd, g_inv = _dft_constants(seq)
    a_n = n // FFT_R
    zspec = pl.BlockSpec((None, 2 * seq, LANES), lambda g, p: (g, p, 0))
    hspec = pl.BlockSpec((n, LANES), lambda g, p: (0, g))
    cspec = lambda arr: pl.BlockSpec(arr.shape, lambda g, p: (0,) * arr.ndim,
                                     pipeline_mode=pl.Buffered(1))
    vspec = pl.BlockSpec((None, 1, LANES), lambda g, p: (g, 0, 0))
    out = jax.ShapeDtypeStruct((groups, batch * seq, LANES), BF16)
    return pl.pallas_call(
        functools.partial(_hy_conv_kernel, seq=seq),
        out_shape=out,
        grid=(groups, batch // 2),
        in_specs=[zspec, zspec, hspec, hspec,
                  cspec(f1_fwd), cspec(f1_inv), cspec(g_fwd), cspec(g_inv), vspec, vspec],
        out_specs=zspec,
        scratch_shapes=[
            pltpu.VMEM((2, seq // FFT_R * FFT_PITCH, LANES), F32),
            pltpu.VMEM((2, a_n * FFT_PITCH, LANES), F32),
            pltpu.VMEM((2, seq // FFT_R * FFT_PITCH, LANES), F32),
        ],
        compiler_params=_cparams(("parallel", "arbitrary")),
        name="hyena_conv",
    )(z_t, x0c_t, h_re, h_im,
      jnp.asarray(f1_fwd, BF16), jnp.asarray(f1_inv, BF16),
      jnp.asarray(g_fwd, BF16), jnp.asarray(g_inv, BF16),
      hy_bias.astype(F32).reshape(groups, 1, LANES),
      hy_out_norm.astype(F32).reshape(groups, 1, LANES))


def _out_proj_kernel(ya_ref, yh_ref, x_ref, w_ref, o_ref, *, heads, groups):
    lhs = jnp.concatenate([ya_ref[g] for g in range(heads)]
                          + [yh_ref[g] for g in range(groups)], axis=-1)
    o_ref[...] = x_ref[...] + jnp.dot(lhs, w_ref[...], preferred_element_type=F32)


def _out_proj(ya_t, yh_t, x2, w_out, tm):
    m, d = x2.shape
    heads, groups = ya_t.shape[0], yh_t.shape[0]
    return pl.pallas_call(
        functools.partial(_out_proj_kernel, heads=heads, groups=groups),
        out_shape=jax.ShapeDtypeStruct((m, d), F32),
        grid=(m // tm,),
        in_specs=[
            pl.BlockSpec((heads, tm, LANES), lambda i: (0, i, 0)),
            pl.BlockSpec((groups, tm, LANES), lambda i: (0, i, 0)),
            pl.BlockSpec((tm, d), lambda i: (i, 0)),
            pl.BlockSpec(w_out.shape, lambda i: (0, 0), pipeline_mode=pl.Buffered(1)),
        ],
        out_specs=pl.BlockSpec((tm, d), lambda i: (i, 0)),
        compiler_params=_cparams(("parallel",)),
        name="out_proj",
    )(ya_t, yh_t, x2, w_out.astype(BF16))


def _ffn_kernel(h_ref, g_ref, wa_ref, wg_ref, wd_ref, o_ref, xn_ref):
    def hidden_tile(xn):
        a = jnp.dot(xn, wa_ref[...], preferred_element_type=F32)
        g = jnp.dot(xn, wg_ref[...], preferred_element_type=F32)
        act = (a * jax.nn.sigmoid(a) * g).astype(BF16)
        return jnp.dot(act, wd_ref[...], preferred_element_type=F32)

    @pl.when(pl.program_id(1) == 0)
    def _():
        chunk = 256
        for c in range(h_ref.shape[0] // chunk):
            rows = pl.ds(c * chunk, chunk)
            h = h_ref[rows, :]
            xn = _rms(h, g_ref[...]).astype(BF16)
            xn_ref[rows, :] = xn
            o_ref[rows, :] = h + hidden_tile(xn)

    @pl.when(pl.program_id(1) > 0)
    def _():
        o_ref[...] += hidden_tile(xn_ref[...])


def _ffn(h1, norm2, w_gu, w_down, tm, th):
    m, d = h1.shape
    hidden = w_down.shape[0]
    nh = hidden // th
    w_gu = w_gu.astype(BF16)
    return pl.pallas_call(
        _ffn_kernel,
        out_shape=jax.ShapeDtypeStruct((m, d), F32),
        grid=(m // tm, nh),
        in_specs=[
            pl.BlockSpec((tm, d), lambda i, j: (i, 0)),
            pl.BlockSpec((1, d), lambda i, j: (0, 0)),
            pl.BlockSpec((d, th), lambda i, j: (0, j)),
            pl.BlockSpec((d, th), lambda i, j: (0, nh + j)),
            pl.BlockSpec((th, d), lambda i, j: (j, 0)),
        ],
        out_specs=pl.BlockSpec((tm, d), lambda i, j: (i, 0)),
        scratch_shapes=[pltpu.VMEM((tm, d), BF16)],
        compiler_params=_cparams(("parallel", "arbitrary")),
        name="ffn",
    )(h1, norm2.reshape(1, d), w_gu, w_gu, w_down.astype(BF16))


def _ple_kernel(h_ref, p_ref, gn_ref, wg_ref, wp_ref, pn_ref, o_ref):
    chunk = 256
    for c in range(h_ref.shape[0] // chunk):
        rows = pl.ds(c * chunk, chunk)
        h = h_ref[rows, :]
        e = _rms(jnp.dot(p_ref[rows, :].astype(BF16), wp_ref[...],
                         preferred_element_type=F32), pn_ref[...])
        inv = lax.rsqrt(jnp.mean(h * h, axis=-1, keepdims=True) + EPS)
        logits = jnp.dot((h * gn_ref[...]).astype(BF16), wg_ref[...],
                         preferred_element_type=F32)
        o_ref[rows, :] = h + jax.nn.sigmoid(logits * inv) * e


def _ple(h2, p2, ple_norm, w_gate, w_proj, ple_post_norm, tm):
    m, d = h2.shape
    pd = p2.shape[1]
    return pl.pallas_call(
        _ple_kernel,
        out_shape=jax.ShapeDtypeStruct((m, d), F32),
        grid=(m // tm,),
        in_specs=[
            pl.BlockSpec((tm, d), lambda i: (i, 0)),
            pl.BlockSpec((tm, pd), lambda i: (i, 0)),
            pl.BlockSpec((1, d), lambda i: (0, 0)),
            pl.BlockSpec((d, d), lambda i: (0, 0), pipeline_mode=pl.Buffered(1)),
            pl.BlockSpec((pd, d), lambda i: (0, 0), pipeline_mode=pl.Buffered(1)),
            pl.BlockSpec((1, d), lambda i: (0, 0)),
        ],
        out_specs=pl.BlockSpec((tm, d), lambda i: (i, 0)),
        compiler_params=_cparams(("parallel",)),
        name="ple",
    )(h2, p2, ple_norm.reshape(1, d), w_gate.astype(BF16), w_proj.astype(BF16),
      ple_post_norm.reshape(1, d))


def kernel(x, p, rel_bias, norm1, w_in, q_norm, k_norm, conv_w, conv_b, hy_w1, hy_b1, hy_wi, hy_bi, hy_wo, hy_freq, hy_decay, hy_bias, attn_out_norm, hy_out_norm, w_out, norm2, w_gu, w_down, ple_norm, w_ple_gate, w_ple_proj, ple_post_norm):
    batch, seq, d = x.shape
    attn_w = d // 2
    heads = attn_w // HEAD_DIM
    groups = heads
    hy_w = groups * LANES
    m = batch * seq
    tm = min(1024, m)
    th = 512
    h = x.reshape(m, d)
    bias_tab = _attn_bias_tables(rel_bias)
    for i in range(norm1.shape[0]):
        h3 = h.reshape(batch, seq, d)
        u_p, z_t, x0c_t = _in_proj(h3, norm1[i], w_in[i], q_norm[i], k_norm[i], conv_w[i],
                                   conv_b[i], attn_w, tm // 2)
        ya_t, (wo, wgu, wdn, wpg, wpp) = _attention(
            u_p, bias_tab, attn_out_norm[i], batch, seq, heads,
            [w_out[i], w_gu[i], w_down[i], w_ple_gate[i], w_ple_proj[i]])
        filt = _hyena_filter(seq, hy_w, hy_w1[i], hy_b1[i], hy_wi[i], hy_bi[i], hy_wo[i],
                             hy_freq[i], hy_decay[i])
        h_re, h_im = _filter_spectrum(filt, seq)
        yh_t = _hyena_conv(z_t, x0c_t, h_re, h_im, hy_bias[i], hy_out_norm[i],
                           batch, seq, groups)
        h = _out_proj(ya_t, yh_t, h, wo, tm)
        h = _ffn(h, norm2[i], wgu, wdn, tm, th)
        h = _ple(h, p[i].reshape(m, PLE_DIM), ple_norm[i], wpg, wpp, ple_post_norm[i], tm)
    return h.reshape(batch, seq, d)
```

```python
import functools
import math

import jax
import jax.numpy as jnp
import numpy as np
from jax import lax
from jax.experimental import pallas as pl
from jax.experimental.pallas import tpu as pltpu

F32 = jnp.float32
BF16 = jnp.bfloat16

LANES = 128
HEAD_DIM = 128
ATTN_PATTERNS = ((128, 1), (512, 4), (2048, 16))
PERM = 16
HALO = 16
BAND = 64
QBLK = 128
KBLK = QBLK + 2 * BAND
N_BUCKETS = 32
REL_MAX_DIST = 1024
HY_EMB = 33
HY_FILTER_WIDTH = 64
HY_INNER = 2
EPS = 1e-6
NEG = -1e30
PLE_DIM = 256

FFT_R = 64
FFT_PITCH = FFT_R + 8
VMEM_LIMIT = 56 * 1024 * 1024


def _cparams(sem, vmem=VMEM_LIMIT):
    return pltpu.CompilerParams(dimension_semantics=sem, vmem_limit_bytes=vmem)


def _rms(x, gain):
    ms = jnp.mean(x * x, axis=-1, keepdims=True)
    return x * lax.rsqrt(ms + EPS) * gain


def _in_proj_kernel(x_ref, xa_ref, xb_ref, g_ref, w_ref, qg_ref, kg_ref, perm_ref,
                    cw_ref, cb_ref, oq_ref, z_ref, x0_ref, ext_ref, x1_ref, *,
                    attn_w, tn, tiles_per_seq):
    tm = x_ref.shape[0]
    chunk = PERM * PERM
    hpt = tn // LANES
    n_qkv = 3 * attn_w // tn
    dot = functools.partial(jnp.dot, preferred_element_type=F32)

    xn, xp = [], []
    for c in range(tm // chunk):
        xn.append(_rms(x_ref[pl.ds(c * chunk, chunk), :], g_ref[...]).astype(BF16))
        xp.append(dot(perm_ref[...], xn[c]).astype(BF16))

    for t in range(n_qkv):
        w = w_ref[:, t * tn:(t + 1) * tn]
        gain = (qg_ref[...] * (HEAD_DIM ** -0.5) if t * tn < attn_w
                else kg_ref[...] if t * tn < 2 * attn_w else None)
        for c in range(tm // chunk):
            acc = dot(xp[c], w)
            for hh in range(hpt):
                a = acc[:, hh * LANES:(hh + 1) * LANES]
                if gain is not None:
                    a = _rms(a, gain)
                a = a.astype(BF16)
                for r in range(PERM):
                    oq_ref[t * hpt + hh, r, pl.ds(c * PERM, PERM), :] = (
                        a[r * PERM:(r + 1) * PERM])

    tile = pl.program_id(0) % tiles_per_seq
    before = _rms(xa_ref[...], g_ref[...]) * jnp.where(tile == 0, 0.0, 1.0)
    after = _rms(xb_ref[...], g_ref[...]) * jnp.where(tile == tiles_per_seq - 1, 0.0, 1.0)
    xe = jnp.concatenate([before.astype(BF16)] + xn + [after.astype(BF16)], axis=0)
    for kind in range(3):
        col = (n_qkv + kind) * tn
        acc = dot(xe, w_ref[:, col:col + tn])
        for hh in range(hpt):
            lanes = slice(hh * LANES, (hh + 1) * LANES)
            ext_ref[hh] = acc[:, lanes]
            taps = cw_ref[:, kind * tn + hh * LANES:kind * tn + (hh + 1) * LANES]
            y = (ext_ref[hh, pl.ds(HALO - 1, tm), :] * taps[0:1]
                 + ext_ref[hh, pl.ds(HALO, tm), :] * taps[1:2]
                 + ext_ref[hh, pl.ds(HALO + 1, tm), :] * taps[2:3]
                 + cb_ref[:, kind * tn + hh * LANES:kind * tn + (hh + 1) * LANES])
            if kind == 0:
                x0_ref[hh] = y.astype(BF16)
            elif kind == 1:
                x1_ref[:, lanes] = y
            else:
                z_ref[hh] = (y * x1_ref[:, lanes]).astype(BF16)


def _in_proj(x3, norm1, w_in, q_norm, k_norm, conv_w, conv_b, attn_w, tm):
    batch, seq, d = x3.shape
    m = batch * seq
    in_w = w_in.shape[1]
    tn = min(1024, attn_w)
    g_qkv = 3 * attn_w // LANES
    groups = tn // LANES
    tiles_per_seq = seq // tm
    assert tm % (PERM * PERM) == 0 and seq % tm == 0 and tm % HALO == 0
    assert in_w == 3 * attn_w + 3 * tn, "one column tile per Hyena operand"
    perm = jnp.asarray(_to_sequence_order_matrix().T, BF16)
    const = lambda shape: pl.BlockSpec(shape, lambda i: (0,) * len(shape))
    halo_blocks = tm // HALO
    x2 = x3.reshape(m, d)
    hy_out = jax.ShapeDtypeStruct((groups, m, LANES), BF16)
    hy_spec = pl.BlockSpec((groups, tm, LANES), lambda i: (0, i, 0))
    u_q, z_t, x0c_t = pl.pallas_call(
        functools.partial(_in_proj_kernel, attn_w=attn_w, tn=tn, tiles_per_seq=tiles_per_seq),
        out_shape=(jax.ShapeDtypeStruct((g_qkv, batch, PERM, seq // PERM, LANES), BF16),
                   hy_out, hy_out),
        grid=(m // tm,),
        in_specs=[
            pl.BlockSpec((tm, d), lambda i: (i, 0)),
            pl.BlockSpec((HALO, d), lambda i: (jnp.maximum(i * halo_blocks - 1, 0), 0)),
            pl.BlockSpec((HALO, d),
                         lambda i: (jnp.minimum((i + 1) * halo_blocks, m // HALO - 1), 0)),
            const((1, d)),
            pl.BlockSpec((d, in_w), lambda i: (0, 0), pipeline_mode=pl.Buffered(1)),
            const((1, LANES)), const((1, LANES)), const(perm.shape),
            const(conv_w.shape), const((1, conv_b.shape[0])),
        ],
        out_specs=(
            pl.BlockSpec((g_qkv, None, PERM, tm // PERM, LANES),
                         lambda i: (0, i // tiles_per_seq, 0, i % tiles_per_seq, 0)),
            hy_spec, hy_spec,
        ),
        scratch_shapes=[pltpu.VMEM((groups, tm + 2 * HALO, LANES), F32),
                        pltpu.VMEM((tm, tn), F32)],
        compiler_params=_cparams(("parallel",)),
        name="in_proj",
    )(x2, x2, x2, norm1.reshape(1, d), w_in.astype(BF16), q_norm.reshape(1, LANES),
      k_norm.reshape(1, LANES), perm, conv_w.astype(F32), conv_b.astype(F32).reshape(1, -1))
    return u_q.reshape(g_qkv, m, LANES), z_t, x0c_t


def _t5_bucket(rel):
    half = N_BUCKETS // 2
    exact = half // 2
    n = np.abs(rel)
    large = exact + (np.log(np.maximum(n, 1).astype(np.float32) / np.float32(exact))
                     / np.float32(math.log(REL_MAX_DIST / exact))
                     * np.float32(half - exact)).astype(np.int32)
    large = np.minimum(large, half - 1)
    return np.where(rel > 0, half, 0) + np.where(n < exact, n, large)


def _block_orders(dil):
    sub = PERM // dil
    e, i = np.divmod(np.arange(QBLK), QBLK // sub)
    q_off = sub * i + e
    if dil == 1:
        k_off = np.arange(KBLK) - BAND
    else:
        e, i = np.divmod(np.arange(KBLK), KBLK // sub)
        k_off = sub * i + e - BAND
    return q_off, k_off


def _to_sequence_order_matrix():
    n = PERM * PERM
    mat = np.zeros((n, n), np.float32)
    r, i = np.divmod(np.arange(n), PERM)
    mat[PERM * i + r, np.arange(n)] = 1.0
    return mat


def _attn_bias_tables(rel_bias):
    tabs = []
    for _, dil in ATTN_PATTERNS:
        q_off, k_off = _block_orders(dil)
        rel = k_off[None, :] - q_off[:, None]
        band = np.abs(rel) <= BAND
        first_ok = np.broadcast_to(k_off[None, :] >= 0, rel.shape)
        last_ok = np.broadcast_to(k_off[None, :] < QBLK, rel.shape)
        onehot = np.eye(N_BUCKETS, dtype=np.float32)[_t5_bucket(rel * dil)]
        bias = jnp.einsum("qkn,nh->hqk", jnp.asarray(onehot), rel_bias.astype(F32),
                          precision=lax.Precision.HIGHEST)
        variants = [jnp.where(jnp.asarray(band & ok), bias, NEG)
                    for ok in (first_ok, np.ones_like(band), last_ok)]
        tabs.append(jnp.stack(variants, axis=1))
    return jnp.stack(tabs, axis=1)


def _attn_kernel(*refs, seq, n_cast):
    q_ref, k_ref, v_ref, bias_ref, gain_ref, perm_ref = refs[:6]
    cast_in = refs[6:6 + n_cast]
    o_ref = refs[6 + n_cast]
    cast_out = refs[7 + n_cast:7 + 2 * n_cast]
    qf_ref, kp_ref, vp_ref, kn_ref, vn_ref, acc_ref, m_ref = refs[7 + 2 * n_cast:]

    for src, dst in zip(cast_in, cast_out):
        dst[...] = src[...].astype(BF16)

    run = seq // PERM
    chunk = PERM * PERM
    ones = jnp.ones((KBLK, LANES), BF16)
    zpad = jnp.zeros((BAND, LANES), BF16)

    for ref in (kp_ref, vp_ref, kn_ref, vn_ref):
        ref[pl.ds(0, BAND), :] = zpad
        ref[pl.ds(BAND + seq, BAND), :] = zpad
    kp_ref[pl.ds(BAND, seq), :] = k_ref[...]
    vp_ref[pl.ds(BAND, seq), :] = v_ref[...]
    qf_ref[...] = q_ref[...].astype(F32)

    def runs(ref, base, n):
        return jnp.concatenate(
            [ref[pl.ds(pl.multiple_of(r * run + base, n), n), :] for r in range(PERM)], axis=0)

    def to_sequence_order(c, carry):
        base = c * PERM
        kv = jnp.concatenate([runs(k_ref, base, PERM), runs(v_ref, base, PERM)], axis=1)
        nat = jnp.dot(perm_ref[...], kv, preferred_element_type=F32).astype(BF16)
        rows = pl.ds(pl.multiple_of(BAND + c * chunk, BAND), chunk)
        kn_ref[rows, :] = nat[:, :LANES]
        vn_ref[rows, :] = nat[:, LANES:]
        return carry

    lax.fori_loop(0, seq // chunk, to_sequence_order, 0, unroll=16)

    def softmax_block(q, k, v, bias):
        s = lax.dot_general(q, k, (((1,), (1,)), ((), ())), preferred_element_type=F32) + bias
        m_blk = jnp.max(s, axis=-1, keepdims=True)
        p = jnp.exp(s - m_blk).astype(BF16)
        a_l = jnp.dot(p, jnp.concatenate([v, ones], axis=1), preferred_element_type=F32)
        return a_l, jnp.broadcast_to(m_blk, (QBLK, LANES))

    def variant(qb, nb):
        return jnp.where(qb == 0, 0, jnp.where(qb == nb - 1, 2, 1))

    def merge(row_slices, a_l, m_b):
        m_old = jnp.concatenate([m_ref[rows, :] for rows in row_slices], axis=0)
        a_old = jnp.concatenate([acc_ref[rows, :] for rows in row_slices], axis=0)
        m_new = jnp.maximum(m_old, m_b)
        w_old = jnp.exp(m_old - m_new)
        w_blk = jnp.exp(m_b - m_new)
        a_new = (a_old * jnp.concatenate([w_old, w_old], axis=1)
                 + a_l * jnp.concatenate([w_blk, w_blk], axis=1))
        n = QBLK // len(row_slices)
        for e, rows in enumerate(row_slices):
            acc_ref[rows, :] = a_new[e * n:(e + 1) * n]
            m_ref[rows, :] = m_new[e * n:(e + 1) * n]

    nb16 = run // QBLK

    def block16(t, carry):
        r, qb = t // nb16, t % nb16
        off = pl.multiple_of(r * run + qb * QBLK, QBLK)
        a_l, m_b = softmax_block(q_ref[pl.ds(off, QBLK), :], kp_ref[pl.ds(off, KBLK), :],
                                 vp_ref[pl.ds(off, KBLK), :], bias_ref[2, variant(qb, nb16)])
        acc_ref[pl.ds(off, QBLK), :] = a_l
        m_ref[pl.ds(off, QBLK), :] = m_b
        return carry

    lax.fori_loop(0, PERM * nb16, block16, 0, unroll=32)

    dil = ATTN_PATTERNS[1][1]
    sub = PERM // dil
    nb4 = (seq // dil) // QBLK
    qn, kn = QBLK // sub, KBLK // sub

    def block4(t, carry):
        r, qb = t // nb4, t % nb4
        q_rows = [pl.ds(pl.multiple_of((dil * e + r) * run + qb * qn, qn), qn)
                  for e in range(sub)]
        k_rows = [pl.ds(pl.multiple_of((dil * e + r) * run + qb * qn + BAND - BAND // sub,
                                       BAND // sub), kn) for e in range(sub)]
        q = jnp.concatenate([q_ref[rows, :] for rows in q_rows], axis=0)
        k = jnp.concatenate([kp_ref[rows, :] for rows in k_rows], axis=0)
        v = jnp.concatenate([vp_ref[rows, :] for rows in k_rows], axis=0)
        a_l, m_b = softmax_block(q, k, v, bias_ref[1, variant(qb, nb4)])
        merge(q_rows, a_l, m_b)
        return carry

    lax.fori_loop(0, dil * nb4, block4, 0, unroll=32)

    nb1 = seq // QBLK
    qn1 = QBLK // PERM

    def block1(qb, carry):
        q_rows = [pl.ds(pl.multiple_of(r * run + qb * qn1, qn1), qn1) for r in range(PERM)]
        q = jnp.concatenate([qf_ref[rows, :] for rows in q_rows], axis=0).astype(BF16)
        k_rows = pl.ds(pl.multiple_of(qb * QBLK, QBLK), KBLK)
        a_l, m_b = softmax_block(q, kn_ref[k_rows, :], vn_ref[k_rows, :],
                                 bias_ref[0, variant(qb, nb1)])
        merge(q_rows, a_l, m_b)
        return carry

    lax.fori_loop(0, nb1, block1, 0, unroll=32)

    def finish(c, carry):
        st = runs(acc_ref, c * PERM, PERM)
        y = _rms(st[:, :LANES] / st[:, LANES:], gain_ref[...]).astype(BF16)
        nat = jnp.dot(perm_ref[...], y, preferred_element_type=F32)
        o_ref[pl.ds(pl.multiple_of(c * chunk, chunk), chunk), :] = nat.astype(BF16)
        return carry

    lax.fori_loop(0, seq // chunk, finish, 0, unroll=16)


def _cast_slices(weights, n_steps, step_of):
    specs = []
    for w in weights:
        rows, hold = w.shape[0], 1
        while (rows * hold) % n_steps or (rows * hold // n_steps) % 16:
            hold *= 2
        specs.append(pl.BlockSpec((rows * hold // n_steps, w.shape[1]),
                                  lambda *ids, hold=hold: (step_of(*ids) // hold, 0)))
    return specs


def _attention(u_p, bias_tab, attn_out_norm, batch, seq, heads, cast_weights):
    assert ATTN_PATTERNS[0][1] == 1 and ATTN_PATTERNS[2][1] == PERM
    assert seq % (PERM * QBLK) == 0
    perm = jnp.asarray(_to_sequence_order_matrix(), BF16)
    pad_rows = seq + 2 * BAND
    cast_specs = _cast_slices(cast_weights, batch * heads, lambda h, b: h * batch + b)
    outs = pl.pallas_call(
        functools.partial(_attn_kernel, seq=seq, n_cast=len(cast_weights)),
        out_shape=[jax.ShapeDtypeStruct((heads, batch * seq, LANES), BF16)]
        + [jax.ShapeDtypeStruct(w.shape, BF16) for w in cast_weights],
        grid=(heads, batch),
        in_specs=[
            pl.BlockSpec((None, seq, LANES), lambda h, b: (h, b, 0)),
            pl.BlockSpec((None, seq, LANES), lambda h, b: (heads + h, b, 0)),
            pl.BlockSpec((None, seq, LANES), lambda h, b: (2 * heads + h, b, 0)),
            pl.BlockSpec((None, len(ATTN_PATTERNS), 3, QBLK, KBLK),
                         lambda h, b: (h, 0, 0, 0, 0)),
            pl.BlockSpec((None, 1, LANES), lambda h, b: (h, 0, 0)),
            pl.BlockSpec(perm.shape, lambda h, b: (0, 0)),
        ] + cast_specs,
        out_specs=[pl.BlockSpec((None, seq, LANES), lambda h, b: (h, b, 0))] + cast_specs,
        scratch_shapes=[
            pltpu.VMEM((seq, LANES), F32),
            pltpu.VMEM((pad_rows, LANES), BF16),
            pltpu.VMEM((pad_rows, LANES), BF16),
            pltpu.VMEM((pad_rows, LANES), BF16),
            pltpu.VMEM((pad_rows, LANES), BF16),
            pltpu.VMEM((seq, 2 * LANES), F32),
            pltpu.VMEM((seq, LANES), F32),
        ],
        compiler_params=_cparams(("arbitrary", "arbitrary")),
        name="attention",
    )(u_p, u_p, u_p, bias_tab, attn_out_norm.reshape(heads, 1, LANES), perm, *cast_weights)
    return outs[0], outs[1:]


def _filter_features(seq):
    pos = np.arange(seq, dtype=np.float32)
    t = pos / np.float32(max(seq - 1, 1))
    bands = (HY_EMB - 1) // 2
    fr = np.linspace(1e-4, bands - 1, bands, dtype=np.float32)
    ang = np.float32(2.0 * math.pi / seq) * pos[:, None] * fr[None, :]
    z = np.concatenate([t[:, None], np.cos(ang), -np.sin(ang)], axis=-1).astype(np.float32)
    zp = np.zeros((seq, LANES), np.float32)
    zp[:, :HY_EMB] = z
    offs = (np.abs(pos - (seq // 2)) / np.float32(seq / 2)).astype(np.float32)
    return zp, offs[:, None]


def _filter_kernel(z_ref, offs_ref, w1_ref, b1_ref, wi_ref, bi_ref, wo_ref, fq_ref,
                   decay_ref, cast_ref, o_ref, cast_out_ref):
    cast_out_ref[...] = cast_ref[...].astype(BF16)
    hp = lax.Precision.HIGHEST
    fq = fq_ref[...]
    h = jnp.sin(fq * (jnp.dot(z_ref[...], w1_ref[...], precision=hp,
                              preferred_element_type=F32) + b1_ref[...]))
    for j in range(HY_INNER):
        h = jnp.sin(fq * (jnp.dot(h, wi_ref[j], precision=hp,
                                  preferred_element_type=F32) + bi_ref[j]))
    h_hi, h_lo = _split_bf16(h)
    w_hi, w_lo = _split_bf16(wo_ref[...])
    dot = functools.partial(jnp.dot, preferred_element_type=F32)
    filt = dot(h_hi, w_hi) + (dot(h_hi, w_lo) + dot(h_lo, w_hi))
    o_ref[...] = filt * jnp.exp(-offs_ref[...] * jnp.abs(decay_ref[...]))


def _hyena_filter(seq, hy_w, w1, b1, wi, bi, wo, freq, decay, cast_weight):
    zp, offs = _filter_features(seq)
    w1p = jnp.zeros((LANES, HY_FILTER_WIDTH), F32).at[:HY_EMB].set(w1.astype(F32))
    tr = 512
    fw = HY_FILTER_WIDTH
    const = lambda *shape: pl.BlockSpec(shape, lambda i: (0,) * len(shape))
    cast_spec, = _cast_slices([cast_weight], seq // tr, lambda i: i)
    return pl.pallas_call(
        _filter_kernel,
        out_shape=(jax.ShapeDtypeStruct((seq, hy_w), F32),
                   jax.ShapeDtypeStruct(cast_weight.shape, BF16)),
        grid=(seq // tr,),
        in_specs=[
            pl.BlockSpec((tr, LANES), lambda i: (i, 0)),
            pl.BlockSpec((tr, 1), lambda i: (i, 0)),
            const(LANES, fw), const(1, fw), const(HY_INNER, fw, fw),
            const(HY_INNER, 1, fw), const(fw, hy_w), const(1, fw), const(1, hy_w),
            cast_spec,
        ],
        out_specs=(pl.BlockSpec((tr, hy_w), lambda i: (i, 0)), cast_spec),
        compiler_params=_cparams(("arbitrary",)),
        name="hyena_filter",
    )(jnp.asarray(zp), jnp.asarray(offs), w1p, b1.reshape(1, fw).astype(F32),
      wi.astype(F32), bi.reshape(HY_INNER, 1, fw).astype(F32), wo.astype(F32),
      freq.reshape(1, fw).astype(F32), decay.reshape(1, hy_w).astype(F32), cast_weight)


def _stack(c):
    return np.block([[c.real, -c.imag], [c.imag, c.real]])


def _fft_len(seq):
    return 3 * seq // 2


@functools.lru_cache(maxsize=None)
def _dft_constants(seq):
    n = _fft_len(seq)
    r = FFT_R
    a_n = n // r
    a_in = seq // r
    a_out = (seq // 2) // r
    ar = np.arange(a_n)
    br = np.arange(r)
    f1 = np.exp(-2j * np.pi * np.outer(ar, ar) / a_n)
    f1_fwd = _stack(f1[:, :a_in])
    f1_fwd_real = np.concatenate([f1[:, :a_in].real, f1[:, :a_in].imag], 0)
    f1_inv = _stack(np.conj(f1).T[a_out:a_out + a_in] / n)
    ph = (br[None, None, :] * br[None, :, None] / r
          + br[None, None, :] * ar[:, None, None] / n)
    g = np.exp(-2j * np.pi * ph)
    g_fwd = np.stack([_stack(g[k]) for k in range(a_n)])
    g_inv = np.stack([_stack(np.conj(g[k]).T) for k in range(a_n)])
    return (f1_fwd.astype(np.float32), f1_fwd_real.astype(np.float32),
            f1_inv.astype(np.float32), g_fwd.astype(np.float32), g_inv.astype(np.float32))


def _split_bf16(x):
    if isinstance(x, np.ndarray):
        hi = x.astype(BF16)
        lo = (x - hi.astype(np.float32)).astype(BF16)
        return jnp.asarray(hi), jnp.asarray(lo)
    hi = x.astype(BF16)
    return hi, (x - hi.astype(F32)).astype(BF16)


def _dot3(m_hi, m_lo, x):
    x_hi, x_lo = _split_bf16(x)
    dot = functools.partial(jnp.dot, preferred_element_type=F32)
    return dot(m_hi, x_hi) + (dot(m_hi, x_lo) + dot(m_lo, x_hi))


def _spectrum_kernel(filt_ref, f1h_ref, f1l_ref, gh_ref, gl_ref, hr_ref, hi_ref,
                     zs_ref, ts_ref, *, seq):
    r = FFT_R
    a_half = seq // r
    a_n = _fft_len(seq) // r
    for a in range(a_half):
        zs_ref[pl.ds(a * FFT_PITCH, r), :] = filt_ref[pl.ds(a * r, r), :]

    def stage1(bp, carry):
        rhs = jnp.concatenate(
            [zs_ref[pl.ds(2 * bp + e, a_half, stride=FFT_PITCH), :] for e in range(2)], axis=1)
        t = _dot3(f1h_ref[...], f1l_ref[...], rhs)
        for e in range(2):
            for c in range(2):
                ts_ref[c, pl.ds(2 * bp + e, a_n, stride=FFT_PITCH), :] = (
                    t[c * a_n:(c + 1) * a_n, e * LANES:(e + 1) * LANES])
        return carry

    lax.fori_loop(0, r // 2, stage1, 0, unroll=True)

    def stage2(ka, carry):
        rows = pl.ds(pl.multiple_of(ka * FFT_PITCH, 8), r)
        rhs = jnp.concatenate([ts_ref[0, rows, :], ts_ref[1, rows, :]], axis=0)
        s = _dot3(gh_ref[ka], gl_ref[ka], rhs)
        out = pl.ds(pl.multiple_of(ka * r, r), r)
        hr_ref[out, :] = s[:r]
        hi_ref[out, :] = s[r:]
        return carry

    lax.fori_loop(0, a_n, stage2, 0, unroll=True)


def _filter_spectrum(filt, seq):
    hy_w = filt.shape[1]
    n = _fft_len(seq)
    _, f1_real, _, g_fwd, _ = _dft_constants(seq)
    f1h, f1l = _split_bf16(f1_real)
    gh, gl = _split_bf16(g_fwd)
    a_n = n // FFT_R
    out = jax.ShapeDtypeStruct((n, hy_w), F32)
    gspec = pl.BlockSpec(g_fwd.shape, lambda c: (0, 0, 0), pipeline_mode=pl.Buffered(1))
    return pl.pallas_call(
        functools.partial(_spectrum_kernel, seq=seq),
        out_shape=(out, out),
        grid=(hy_w // LANES,),
        in_specs=[
            pl.BlockSpec((seq, LANES), lambda c: (0, c)),
            pl.BlockSpec(f1_real.shape, lambda c: (0, 0)),
            pl.BlockSpec(f1_real.shape, lambda c: (0, 0)),
            gspec, gspec,
        ],
        out_specs=(pl.BlockSpec((n, LANES), lambda c: (0, c)),
                   pl.BlockSpec((n, LANES), lambda c: (0, c))),
        scratch_shapes=[
            pltpu.VMEM((seq // FFT_R * FFT_PITCH, LANES), F32),
            pltpu.VMEM((2, a_n * FFT_PITCH, LANES), F32),
        ],
        compiler_params=_cparams(("parallel",)),
        name="filter_spectrum",
    )(filt, f1h, f1l, gh, gl)


def _hy_conv_kernel(z_ref, x_ref, hr_ref, hi_ref, f1_ref, f1i_ref,
                    g_ref, gi_ref, bias_ref, gain_ref, o_ref,
                    zs_ref, ts_ref, ys_ref, *, seq):
    r = FFT_R
    a_half = seq // r
    a_n = _fft_len(seq) // r
    for c in range(2):
        for a in range(a_half):
            zs_ref[c, pl.ds(a * FFT_PITCH, r), :] = (
                z_ref[pl.ds(c * seq + a * r, r), :].astype(F32))

    def stage1(bp, carry):
        cols = []
        for e in range(2):
            rows = pl.ds(2 * bp + e, a_half, stride=FFT_PITCH)
            cols.append(jnp.concatenate([zs_ref[0, rows, :], zs_ref[1, rows, :]], axis=0))
        rhs = jnp.concatenate(cols, axis=1).astype(BF16)
        t = jnp.dot(f1_ref[...], rhs, preferred_element_type=F32)
        for e in range(2):
            for c in range(2):
                ts_ref[c, pl.ds(2 * bp + e, a_n, stride=FFT_PITCH), :] = (
                    t[c * a_n:(c + 1) * a_n, e * LANES:(e + 1) * LANES])
        return carry

    lax.fori_loop(0, r // 2, stage1, 0, unroll=True)

    def stage2(ka, carry):
        rows = pl.ds(pl.multiple_of(ka * FFT_PITCH, 8), r)
        rhs = jnp.concatenate([ts_ref[0, rows, :], ts_ref[1, rows, :]], axis=0).astype(BF16)
        s = jnp.dot(g_ref[ka], rhs, preferred_element_type=F32)
        hrows = pl.ds(pl.multiple_of(ka * r, r), r)
        hr = hr_ref[hrows, :]
        hi = hi_ref[hrows, :]
        sr, si = s[:r], s[r:]
        prod = jnp.concatenate([sr * hr - si * hi, sr * hi + si * hr], axis=0).astype(BF16)
        u = jnp.dot(gi_ref[ka], prod, preferred_element_type=F32)
        ts_ref[0, rows, :] = u[:r]
        ts_ref[1, rows, :] = u[r:]
        return carry

    lax.fori_loop(0, a_n, stage2, 0, unroll=True)

    def stage3(bp, carry):
        cols = []
        for e in range(2):
            rows = pl.ds(2 * bp + e, a_n, stride=FFT_PITCH)
            cols.append(jnp.concatenate([ts_ref[0, rows, :], ts_ref[1, rows, :]], axis=0))
        rhs = jnp.concatenate(cols, axis=1).astype(BF16)
        y = jnp.dot(f1i_ref[...], rhs, preferred_element_type=F32)
        for e in range(2):
            for c in range(2):
                ys_ref[c, pl.ds(2 * bp + e, a_half, stride=FFT_PITCH), :] = (
                    y[c * a_half:(c + 1) * a_half, e * LANES:(e + 1) * LANES])
        return carry

    lax.fori_loop(0, r // 2, stage3, 0, unroll=True)

    def finish(a, carry):
        prow = pl.ds(pl.multiple_of(a * FFT_PITCH, 8), r)
        for c in range(2):
            orow = pl.ds(pl.multiple_of(c * seq + a * r, r), r)
            z = ys_ref[c, prow, :] + zs_ref[c, prow, :] * bias_ref[...]
            y = z * x_ref[orow, :].astype(F32)
            o_ref[orow, :] = _rms(y, gain_ref[...]).astype(BF16)
        return carry

    lax.fori_loop(0, a_half, finish, 0, unroll=True)


def _hyena_conv(z_t, x0c_t, h_re, h_im, hy_bias, hy_out_norm, batch, seq, groups):
    n = _fft_len(seq)
    f1_fwd, _, f1_inv, g_fwd, g_inv = _dft_constants(seq)
    a_n = n // FFT_R
    zspec = pl.BlockSpec((None, 2 * seq, LANES), lambda g, p: (g, p, 0))
    hspec = pl.BlockSpec((n, LANES), lambda g, p: (0, g))
    cspec = lambda arr: pl.BlockSpec(arr.shape, lambda g, p: (0,) * arr.ndim,
                                     pipeline_mode=pl.Buffered(1))
    vspec = pl.BlockSpec((None, 1, LANES), lambda g, p: (g, 0, 0))
    out = jax.ShapeDtypeStruct((groups, batch * seq, LANES), BF16)
    return pl.pallas_call(
        functools.partial(_hy_conv_kernel, seq=seq),
        out_shape=out,
        grid=(groups, batch // 2),
        in_specs=[zspec, zspec, hspec, hspec,
                  cspec(f1_fwd), cspec(f1_inv), cspec(g_fwd), cspec(g_inv), vspec, vspec],
        out_specs=zspec,
        scratch_shapes=[
            pltpu.VMEM((2, seq // FFT_R * FFT_PITCH, LANES), F32),
            pltpu.VMEM((2, a_n * FFT_PITCH, LANES), F32),
            pltpu.VMEM((2, seq // FFT_R * FFT_PITCH, LANES), F32),
        ],
        compiler_params=_cparams(("parallel", "arbitrary")),
        name="hyena_conv",
    )(z_t, x0c_t, h_re, h_im,
      jnp.asarray(f1_fwd, BF16), jnp.asarray(f1_inv, BF16),
      jnp.asarray(g_fwd, BF16), jnp.asarray(g_inv, BF16),
      hy_bias.astype(F32).reshape(groups, 1, LANES),
      hy_out_norm.astype(F32).reshape(groups, 1, LANES))


def _out_proj_kernel(ya_ref, yh_ref, x_ref, w_ref, o_ref, *, heads, groups):
    lhs = jnp.concatenate([ya_ref[g] for g in range(heads)]
                          + [yh_ref[g] for g in range(groups)], axis=-1)
    o_ref[...] = x_ref[...] + jnp.dot(lhs, w_ref[...], preferred_element_type=F32)


def _out_proj(ya_t, yh_t, x2, w_out, tm):
    m, d = x2.shape
    heads, groups = ya_t.shape[0], yh_t.shape[0]
    return pl.pallas_call(
        functools.partial(_out_proj_kernel, heads=heads, groups=groups),
        out_shape=jax.ShapeDtypeStruct((m, d), F32),
        grid=(m // tm,),
        in_specs=[
            pl.BlockSpec((heads, tm, LANES), lambda i: (0, i, 0)),
            pl.BlockSpec((groups, tm, LANES), lambda i: (0, i, 0)),
            pl.BlockSpec((tm, d), lambda i: (i, 0)),
            pl.BlockSpec(w_out.shape, lambda i: (0, 0), pipeline_mode=pl.Buffered(1)),
        ],
        out_specs=pl.BlockSpec((tm, d), lambda i: (i, 0)),
        compiler_params=_cparams(("parallel",)),
        name="out_proj",
    )(ya_t, yh_t, x2, w_out.astype(BF16))


def _ffn_kernel(h_ref, g_ref, wa_ref, wg_ref, wd_ref, o_ref, xn_ref):
    def hidden_tile(xn):
        a = jnp.dot(xn, wa_ref[...], preferred_element_type=F32)
        g = jnp.dot(xn, wg_ref[...], preferred_element_type=F32)
        act = (a * jax.nn.sigmoid(a) * g).astype(BF16)
        return jnp.dot(act, wd_ref[...], preferred_element_type=F32)

    @pl.when(pl.program_id(1) == 0)
    def _():
        chunk = 256
        for c in range(h_ref.shape[0] // chunk):
            rows = pl.ds(c * chunk, chunk)
            h = h_ref[rows, :]
            xn = _rms(h, g_ref[...]).astype(BF16)
            xn_ref[rows, :] = xn
            o_ref[rows, :] = h + hidden_tile(xn)

    @pl.when(pl.program_id(1) > 0)
    def _():
        o_ref[...] += hidden_tile(xn_ref[...])


def _ffn(h1, norm2, w_gu, w_down, tm, th):
    m, d = h1.shape
    hidden = w_down.shape[0]
    nh = hidden // th
    w_gu = w_gu.astype(BF16)
    return pl.pallas_call(
        _ffn_kernel,
        out_shape=jax.ShapeDtypeStruct((m, d), F32),
        grid=(m // tm, nh),
        in_specs=[
            pl.BlockSpec((tm, d), lambda i, j: (i, 0)),
            pl.BlockSpec((1, d), lambda i, j: (0, 0)),
            pl.BlockSpec((d, th), lambda i, j: (0, j)),
            pl.BlockSpec((d, th), lambda i, j: (0, nh + j)),
            pl.BlockSpec((th, d), lambda i, j: (j, 0)),
        ],
        out_specs=pl.BlockSpec((tm, d), lambda i, j: (i, 0)),
        scratch_shapes=[pltpu.VMEM((tm, d), BF16)],
        compiler_params=_cparams(("parallel", "arbitrary")),
        name="ffn",
    )(h1, norm2.reshape(1, d), w_gu, w_gu, w_down.astype(BF16))


def _ple_kernel(h_ref, p_ref, gn_ref, wg_ref, wp_ref, pn_ref, o_ref):
    chunk = 256
    for c in range(h_ref.shape[0] // chunk):
        rows = pl.ds(c * chunk, chunk)
        h = h_ref[rows, :]
        e = _rms(jnp.dot(p_ref[rows, :].astype(BF16), wp_ref[...],
                         preferred_element_type=F32), pn_ref[...])
        inv = lax.rsqrt(jnp.mean(h * h, axis=-1, keepdims=True) + EPS)
        logits = jnp.dot((h * gn_ref[...]).astype(BF16), wg_ref[...],
                         preferred_element_type=F32)
        o_ref[rows, :] = h + jax.nn.sigmoid(logits * inv) * e


def _ple(h2, p2, ple_norm, w_gate, w_proj, ple_post_norm, tm):
    m, d = h2.shape
    pd = p2.shape[1]
    return pl.pallas_call(
        _ple_kernel,
        out_shape=jax.ShapeDtypeStruct((m, d), F32),
        grid=(m // tm,),
        in_specs=[
            pl.BlockSpec((tm, d), lambda i: (i, 0)),
            pl.BlockSpec((tm, pd), lambda i: (i, 0)),
            pl.BlockSpec((1, d), lambda i: (0, 0)),
            pl.BlockSpec((d, d), lambda i: (0, 0), pipeline_mode=pl.Buffered(1)),
            pl.BlockSpec((pd, d), lambda i: (0, 0), pipeline_mode=pl.Buffered(1)),
            pl.BlockSpec((1, d), lambda i: (0, 0)),
        ],
        out_specs=pl.BlockSpec((tm, d), lambda i: (i, 0)),
        compiler_params=_cparams(("parallel",)),
        name="ple",
    )(h2, p2, ple_norm.reshape(1, d), w_gate.astype(BF16), w_proj.astype(BF16),
      ple_post_norm.reshape(1, d))


def kernel(x, p, rel_bias, norm1, w_in, q_norm, k_norm, conv_w, conv_b, hy_w1, hy_b1, hy_wi, hy_bi, hy_wo, hy_freq, hy_decay, hy_bias, attn_out_norm, hy_out_norm, w_out, norm2, w_gu, w_down, ple_norm, w_ple_gate, w_ple_proj, ple_post_norm):
    batch, seq, d = x.shape
    attn_w = d // 2
    heads = attn_w // HEAD_DIM
    groups = heads
    hy_w = groups * LANES
    m = batch * seq
    tm = min(1024, m)
    th = 512
    h = x.reshape(m, d)
    bias_tab = _attn_bias_tables(rel_bias)
    for i in range(norm1.shape[0]):
        h3 = h.reshape(batch, seq, d)
        filt, w_in_b = _hyena_filter(seq, hy_w, hy_w1[i], hy_b1[i], hy_wi[i], hy_bi[i],
                                     hy_wo[i], hy_freq[i], hy_decay[i], w_in[i])
        h_re, h_im = _filter_spectrum(filt, seq)
        u_p, z_t, x0c_t = _in_proj(h3, norm1[i], w_in_b, q_norm[i], k_norm[i], conv_w[i],
                                   conv_b[i], attn_w, tm // 2)
        ya_t, (wo, wgu, wdn, wpg, wpp) = _attention(
            u_p, bias_tab, attn_out_norm[i], batch, seq, heads,
            [w_out[i], w_gu[i], w_down[i], w_ple_gate[i], w_ple_proj[i]])
        yh_t = _hyena_conv(z_t, x0c_t, h_re, h_im, hy_bias[i], hy_out_norm[i],
                           batch, seq, groups)
        h = _out_proj(ya_t, yh_t, h, wo, tm)
        h = _ffn(h, norm2[i], wgu, wdn, tm, th)
        h = _ple(h, p[i].reshape(m, PLE_DIM), ple_norm[i], wpg, wpp, ple_post_norm[i], tm)
    return h.reshape(batch, seq, d)
```
